```python
import math
import jax
import jax.numpy as jnp
from jax import lax
import numpy as np

D_MODEL = 1024
BATCH = 8
SEQ = 2048
DEPTH = 4
DEC_BATCH = 128
DEC_SEQ = 8
PAST_LEN = 8192
PAGE_SIZE = 128

HEAD_DIM = 64
N_HEADS = D_MODEL // 128
N_KV_HEADS = N_HEADS // 4
GQA_GROUP = N_HEADS // N_KV_HEADS
ATTN_W = N_HEADS * HEAD_DIM
WINDOW = 128
ATTN_BLOCK = WINDOW
ATTN_SCALE = 1.0 / math.sqrt(HEAD_DIM)
SSM_W = D_MODEL // 4
SSM_GROUP_CH = 16
SSM_GROUPS = SSM_W // SSM_GROUP_CH
SSM_STATE = 64
GM_W = D_MODEL // 4
GM_HEADS = 4
GM_HEAD_DIM = GM_W // GM_HEADS
CHUNK = 128
MIX_W = ATTN_W + SSM_W + GM_W
KV_W = N_KV_HEADS * HEAD_DIM
Q_END = ATTN_W
K_END = Q_END + KV_W
V_END = K_END + KV_W
S_END = V_END + SSM_W
IN_COLS = S_END + 2 * GM_W
D_FF = 2816
N_EXPERTS = 8
TOP_K = 2
E_FF = D_FF // 2
N_DENSE = (DEPTH + 1) // 2
N_MOE = DEPTH // 2
EPS = 1e-6

kernel_name = 'hymba_swa_s5_gmlp_decode_step'


def rms_norm(x, g):
    xf = x.astype(jnp.float32)
    y = xf * lax.rsqrt(jnp.mean(xf * xf, axis=-1, keepdims=True) + EPS)
    return (y * g.astype(jnp.float32)).astype(x.dtype)


def sink_softmax(s, mask, sink):
    s = jnp.where(mask, s, -jnp.inf)
    m = jnp.maximum(jnp.max(s, axis=-1, keepdims=True), sink)
    e = jnp.exp(s - m)
    return e / (jnp.sum(e, axis=-1, keepdims=True) + jnp.exp(sink - m))


def swa_prompt(q, k, v, sinks):
    b, l = q.shape[:2]
    nb = l // ATTN_BLOCK
    qb = q.reshape(b, nb, ATTN_BLOCK, N_KV_HEADS, GQA_GROUP, HEAD_DIM)
    kb = k.reshape(b, nb, ATTN_BLOCK, N_KV_HEADS, HEAD_DIM)
    vb = v.reshape(b, nb, ATTN_BLOCK, N_KV_HEADS, HEAD_DIM)

    def with_prev(t):
        prev = jnp.concatenate([jnp.zeros_like(t[:, :1]), t[:, :-1]], axis=1)
        return jnp.concatenate([prev, t], axis=2)

    kk, vv = with_prev(kb), with_prev(vb)
    s = jnp.einsum('bnqkgd,bnskd->bnkgqs', qb, kk, preferred_element_type=jnp.float32) * ATTN_SCALE
    qi = jnp.arange(ATTN_BLOCK)[:, None]
    si = jnp.arange(2 * ATTN_BLOCK)[None, :]
    dist = ATTN_BLOCK + qi - si
    band = (dist >= 0) & (dist < WINDOW)
    has_prev = (jnp.arange(nb) > 0)[:, None, None]
    mask = band[None] & (has_prev | (si >= ATTN_BLOCK)[None])
    sink = sinks.astype(jnp.float32).reshape(N_KV_HEADS, GQA_GROUP, 1, 1)
    p = sink_softmax(s, mask[None, :, None, None], sink)
    o = jnp.einsum('bnkgqs,bnskd->bnqkgd', p.astype(v.dtype), vv)
    return o.reshape(b, l, ATTN_W)


def swa_sample(q, k, v, k_buf, v_buf, sinks):
    b, t = q.shape[:2]
    w = k_buf.shape[1]
    kk = jnp.concatenate([k_buf.astype(k.dtype), k], axis=1)
    vv = jnp.concatenate([v_buf.astype(v.dtype), v], axis=1)
    qg = q.reshape(b, t, N_KV_HEADS, GQA_GROUP, HEAD_DIM)
    s = jnp.einsum('btkgd,bskd->bkgts', qg, kk, preferred_element_type=jnp.float32) * ATTN_SCALE
    dist = w + jnp.arange(t)[:, None] - jnp.arange(w + t)[None, :]
    mask = (dist >= 0) & (dist < WINDOW)
    sink = sinks.astype(jnp.float32).reshape(N_KV_HEADS, GQA_GROUP, 1, 1)
    p = sink_softmax(s, mask, sink)
    o = jnp.einsum('bkgts,bskd->btkgd', p.astype(v.dtype), vv)
    return o.reshape(b, t, ATTN_W), kk[:, t:], vv[:, t:]


def _complex_linear_combine(e1, e2):
    a1r, a1i, b1r, b1i = e1
    a2r, a2i, b2r, b2i = e2
    ar = a1r * a2r - a1i * a2i
    ai = a1r * a2i + a1i * a2r
    br = a2r * b1r - a2i * b1i + b2r
    bi = a2r * b1i + a2i * b1r + b2i
    return ar, ai, br, bi


def s5_mix(u, h0_re, h0_im, a_re, a_im, log_dt, b_re, b_im, c_re, c_im, d, w_glu, b_glu):
    f32 = jnp.float32
    bsz, l = u.shape[:2]
    ar = a_re.astype(f32)
    ai = a_im.astype(f32)
    dt = jnp.exp(log_dt.astype(f32))[:, None]
    decay = jnp.exp(dt * ar)
    abar_re = decay * jnp.cos(dt * ai)
    abar_im = decay * jnp.sin(dt * ai)
    den = ar * ar + ai * ai
    nr = abar_re - 1.0
    ni = abar_im
    f_re = (nr * ar + ni * ai) / den
    f_im = (ni * ar - nr * ai) / den
    br = b_re.astype(f32)
    bi = b_im.astype(f32)
    bbar_re = f_re[..., None] * br - f_im[..., None] * bi
    bbar_im = f_re[..., None] * bi + f_im[..., None] * br
    uf = u.astype(f32)
    ug = uf.reshape(bsz, l, SSM_GROUPS, SSM_GROUP_CH)
    bu_re = jnp.einsum('blgc,gpc->blgp', ug, bbar_re)
    bu_im = jnp.einsum('blgc,gpc->blgp', ug, bbar_im)
    hr = h0_re.astype(f32)
    hi = h0_im.astype(f32)
    bu_re = bu_re.at[:, 0].add(abar_re * hr - abar_im * hi)
    bu_im = bu_im.at[:, 0].add(abar_re * hi + abar_im * hr)
    a_seq_re = jnp.broadcast_to(abar_re, bu_re.shape)
    a_seq_im = jnp.broadcast_to(abar_im, bu_im.shape)
    _, _, x_re, x_im = lax.associative_scan(
        _complex_linear_combine, (a_seq_re, a_seq_im, bu_re, bu_im), axis=1)
    y = (jnp.einsum('blgp,gcp->blgc', x_re, c_re.astype(f32))
         - jnp.einsum('blgp,gcp->blgc', x_im, c_im.astype(f32)))
    y = y.reshape(bsz, l, SSM_W) + d.astype(f32) * uf
    y = jax.nn.gelu(y)
    y = y * jax.nn.sigmoid(y @ w_glu.astype(f32) + b_glu.astype(f32))
    return y.astype(u.dtype), x_re[:, -1], x_im[:, -1]


def gmlp_spatial_gate(u, v, g_v, w_s, b_s):
    f32 = jnp.float32
    b, l = u.shape[:2]
    c = min(l, CHUNK)
    nc = l // c
    vf = v.astype(f32).reshape(b, nc, c, GM_HEADS, GM_HEAD_DIM)
    mu = jnp.mean(vf, axis=-1, keepdims=True)
    var = jnp.mean(jnp.square(vf - mu), axis=-1, keepdims=True)
    vn = (vf - mu) * lax.rsqrt(var + EPS) * g_v.astype(f32).reshape(GM_HEADS, GM_HEAD_DIM)
    ws = jnp.tril(w_s[:, :c, :c].astype(f32))
    z = jnp.einsum('hts,bnshd->bnthd', ws, vn) + b_s[:, :c].astype(f32).T[:, :, None]
    out = u.astype(f32).reshape(b, nc, c, GM_HEADS, GM_HEAD_DIM) * z
    return (out.reshape(b, l, GM_W).astype(u.dtype),
            vn.reshape(b, l, GM_HEADS, GM_HEAD_DIM).astype(u.dtype))


def swiglu(h, wg, wu, wd):
    return (jax.nn.silu(h @ wg) * (h @ wu)) @ wd


def moe_swiglu(h, w_r, b_r, wg, wu, wd):
    logits = (h @ w_r).astype(jnp.float32) + b_r.astype(jnp.float32)
    top_v, top_i = lax.top_k(logits, TOP_K)
    gates = jax.nn.softmax(top_v, axis=-1)
    comb = jnp.sum(jax.nn.one_hot(top_i, N_EXPERTS, dtype=jnp.float32) * gates[..., None], axis=-2)
    comb = comb.astype(h.dtype)
    y = jnp.zeros_like(h)
    for e in range(N_EXPERTS):
        y = y + comb[..., e:e + 1] * swiglu(h, wg[e], wu[e], wd[e])
    return y


def trunk(x, p, past):
    b, l = x.shape[:2]
    new = {'k': [], 'v': [], 're': [], 'im': [], 'gv': []}
    for i in range(DEPTH):
        h = rms_norm(x, p['norm_mix_g'][i])
        z = h @ p['w_in'][i]
        q = z[..., :Q_END].reshape(b, l, N_HEADS, HEAD_DIM)
        k = z[..., Q_END:K_END].reshape(b, l, N_KV_HEADS, HEAD_DIM)
        v = z[..., K_END:V_END].reshape(b, l, N_KV_HEADS, HEAD_DIM)
        u_ssm = z[..., V_END:S_END]
        g = jax.nn.gelu(z[..., S_END:])
        u_gm, v_gm = g[..., :GM_W], g[..., GM_W:]
        if past is None:
            o_attn = swa_prompt(q, k, v, p['attn_sinks'][i])
            nw = min(WINDOW, l)
            k_win, v_win = k[:, l - nw:], v[:, l - nw:]
            h0_re = jnp.zeros((b, SSM_GROUPS, SSM_STATE), jnp.float32)
            h0_im = h0_re
        else:
            o_attn, k_win, v_win = swa_sample(q, k, v, past[0][i], past[1][i], p['attn_sinks'][i])
            h0_re, h0_im = past[2][i], past[3][i]
        o_ssm, s_re, s_im = s5_mix(
            u_ssm, h0_re, h0_im, p['ssm_a_re'][i], p['ssm_a_im'][i], p['ssm_log_dt'][i],
            p['ssm_b_re'][i], p['ssm_b_im'][i], p['ssm_c_re'][i], p['ssm_c_im'][i],
            p['ssm_d'][i], p['ssm_w_glu'][i], p['ssm_b_glu'][i])
        o_gm, v_rows = gmlp_spatial_gate(u_gm, v_gm, p['gmlp_v_norm_g'][i],
                                         p['gmlp_w_s'][i], p['gmlp_b_s'][i])
        g_out = p['mix_out_norm_g'][i]
        o = jnp.concatenate([
            rms_norm(o_attn, g_out[:ATTN_W]),
            rms_norm(o_ssm, g_out[ATTN_W:ATTN_W + SSM_W]),
            rms_norm(o_gm, g_out[ATTN_W + SSM_W:])], axis=-1)
        x = x + o @ p['w_out'][i]
        h2 = rms_norm(x, p['norm_ffn_g'][i])
        j = i // 2
        if i % 2 == 0:
            x = x + swiglu(h2, p['ffn_w_gate'][j], p['ffn_w_up'][j], p['ffn_w_down'][j])
        else:
            x = x + moe_swiglu(h2, p['moe_w_router'][j], p['moe_b_router'][j],
                               p['moe_w_gate'][j], p['moe_w_up'][j], p['moe_w_down'][j])
        new['k'].append(k_win)
        new['v'].append(v_win)
        new['re'].append(s_re.astype(h0_re.dtype))
        new['im'].append(s_im.astype(h0_im.dtype))
        if past is not None:
            new['gv'].append(v_rows)
    return rms_norm(x, p['final_norm_g']), new


def setup_inputs(seed: int = 0) -> dict:
    key = jax.random.key(seed)
    ks = iter(jax.random.split(key, 40))
    f32 = jnp.float32
    win_buf = min(WINDOW, PAST_LEN)

    def nrm(shape, scale):
        return jax.random.normal(next(ks), shape, f32) * scale

    def gain(shape):
        return 1.0 + nrm(shape, 0.02)

    n_idx = jnp.arange(SSM_STATE, dtype=f32)
    return {
        'x_prompt': nrm((BATCH, SEQ, D_MODEL), 1.0),
        'x_sample': nrm((DEC_BATCH, DEC_SEQ, D_MODEL), 1.0),
        'cache_k_win': nrm((DEPTH, DEC_BATCH, win_buf, N_KV_HEADS, HEAD_DIM), 1.0),
        'cache_v_win': nrm((DEPTH, DEC_BATCH, win_buf, N_KV_HEADS, HEAD_DIM), 1.0),
        'state_ssm_re': nrm((DEPTH, DEC_BATCH, SSM_GROUPS, SSM_STATE), 0.1),
        'state_ssm_im': nrm((DEPTH, DEC_BATCH, SSM_GROUPS, SSM_STATE), 0.1),
        'norm_mix_g': gain((DEPTH, D_MODEL)),
        'w_in': nrm((DEPTH, D_MODEL, IN_COLS), D_MODEL ** -0.5),
        'attn_sinks': nrm((DEPTH, N_HEADS), 1.0),
        'ssm_a_re': -0.5 + nrm((DEPTH, SSM_GROUPS, SSM_STATE), 0.01),
        'ssm_a_im': math.pi * n_idx + nrm((DEPTH, SSM_GROUPS, SSM_STATE), 0.01),
        'ssm_log_dt': jax.random.uniform(next(ks), (DEPTH, SSM_GROUPS), f32,
                                         math.log(1e-3), math.log(1e-1)),
        'ssm_b_re': nrm((DEPTH, SSM_GROUPS, SSM_STATE, SSM_GROUP_CH), (2 * SSM_GROUP_CH) ** -0.5),
        'ssm_b_im': nrm((DEPTH, SSM_GROUPS, SSM_STATE, SSM_GROUP_CH), (2 * SSM_GROUP_CH) ** -0.5),
        'ssm_c_re': nrm((DEPTH, SSM_GROUPS, SSM_GROUP_CH, SSM_STATE), SSM_STATE ** -0.5),
        'ssm_c_im': nrm((DEPTH, SSM_GROUPS, SSM_GROUP_CH, SSM_STATE), SSM_STATE ** -0.5),
        'ssm_d': nrm((DEPTH, SSM_W), 1.0),
        'ssm_w_glu': nrm((DEPTH, SSM_W, SSM_W), SSM_W ** -0.5),
        'ssm_b_glu': nrm((DEPTH, SSM_W), 0.01),
        'gmlp_v_norm_g': gain((DEPTH, GM_W)),
        'gmlp_w_s': nrm((DEPTH, GM_HEADS, CHUNK, CHUNK), CHUNK ** -0.5),
        'gmlp_b_s': gain((DEPTH, GM_HEADS, CHUNK)),
        'mix_out_norm_g': gain((DEPTH, MIX_W)),
        'w_out': nrm((DEPTH, MIX_W, D_MODEL), MIX_W ** -0.5),
        'norm_ffn_g': gain((DEPTH, D_MODEL)),
        'ffn_w_gate': nrm((N_DENSE, D_MODEL, D_FF), D_MODEL ** -0.5),
        'ffn_w_up': nrm((N_DENSE, D_MODEL, D_FF), D_MODEL ** -0.5),
        'ffn_w_down': nrm((N_DENSE, D_FF, D_MODEL), D_FF ** -0.5),
        'moe_w_router': nrm((N_MOE, D_MODEL, N_EXPERTS), D_MODEL ** -0.5),
        'moe_b_router': nrm((N_MOE, N_EXPERTS), 0.01),
        'moe_w_gate': nrm((N_MOE, N_EXPERTS, D_MODEL, E_FF), D_MODEL ** -0.5),
        'moe_w_up': nrm((N_MOE, N_EXPERTS, D_MODEL, E_FF), D_MODEL ** -0.5),
        'moe_w_down': nrm((N_MOE, N_EXPERTS, E_FF, D_MODEL), E_FF ** -0.5),
        'final_norm_g': gain((D_MODEL,)),
    }


def reference(x_prompt, x_sample, cache_k_win, cache_v_win, state_ssm_re, state_ssm_im,
              norm_mix_g, w_in, attn_sinks, ssm_a_re, ssm_a_im, ssm_log_dt, ssm_b_re, ssm_b_im,
              ssm_c_re, ssm_c_im, ssm_d, ssm_w_glu, ssm_b_glu, gmlp_v_norm_g, gmlp_w_s, gmlp_b_s,
              mix_out_norm_g, w_out, norm_ffn_g, ffn_w_gate, ffn_w_up, ffn_w_down,
              moe_w_router, moe_b_router, moe_w_gate, moe_w_up, moe_w_down, final_norm_g):
    params = {
        'norm_mix_g': norm_mix_g, 'w_in': w_in, 'attn_sinks': attn_sinks,
        'ssm_a_re': ssm_a_re, 'ssm_a_im': ssm_a_im, 'ssm_log_dt': ssm_log_dt,
        'ssm_b_re': ssm_b_re, 'ssm_b_im': ssm_b_im, 'ssm_c_re': ssm_c_re, 'ssm_c_im': ssm_c_im,
        'ssm_d': ssm_d, 'ssm_w_glu': ssm_w_glu, 'ssm_b_glu': ssm_b_glu,
        'gmlp_v_norm_g': gmlp_v_norm_g, 'gmlp_w_s': gmlp_w_s, 'gmlp_b_s': gmlp_b_s,
        'mix_out_norm_g': mix_out_norm_g, 'w_out': w_out, 'norm_ffn_g': norm_ffn_g,
        'ffn_w_gate': ffn_w_gate, 'ffn_w_up': ffn_w_up, 'ffn_w_down': ffn_w_down,
        'moe_w_router': moe_w_router, 'moe_b_router': moe_b_router,
        'moe_w_gate': moe_w_gate, 'moe_w_up': moe_w_up, 'moe_w_down': moe_w_down,
        'final_norm_g': final_norm_g,
    }
    y_prompt, st_p = trunk(x_prompt, params, None)
    y_sample, st_s = trunk(x_sample, params, (cache_k_win, cache_v_win, state_ssm_re, state_ssm_im))
    return (y_prompt, y_sample,
            jnp.stack(st_p['k']), jnp.stack(st_p['v']),
            jnp.stack(st_p['re']), jnp.stack(st_p['im']),
            jnp.stack(st_s['k']), jnp.stack(st_s['v']),
            jnp.stack(st_s['re']), jnp.stack(st_s['im']),
            jnp.stack(st_s['gv']))
```

```python
import functools
import math

import jax
import jax.numpy as jnp
from jax import lax
from jax.experimental import pallas as pl
from jax.experimental.pallas import tpu as pltpu

F32 = jnp.float32
BF16 = jnp.bfloat16

EPS = 1e-6
HEAD_DIM = 64
N_HEADS = 8
N_KV_HEADS = 2
GQA_GROUP = N_HEADS // N_KV_HEADS
ATTN_W = N_HEADS * HEAD_DIM
KV_W = N_KV_HEADS * HEAD_DIM
WINDOW = 128
ATTN_SCALE = 1.0 / math.sqrt(HEAD_DIM)
SSM_W = 256
SSM_GROUP_CH = 16
SSM_GROUPS = 16
SSM_STATE = 64
SSM_FLAT = SSM_GROUPS * SSM_STATE
GM_W = 256
GM_HEADS = 4
GM_HEAD_DIM = 64
CHUNK = 128
Q_END = ATTN_W
K_END = Q_END + KV_W
V_END = K_END + KV_W
S_END = V_END + SSM_W
IN_COLS = S_END + 2 * GM_W
N_EXPERTS = 8
ROUTER_LANES = 128
NEG_BIG = -1e30

VMEM_LIMIT_BYTES = 56 * 1024 * 1024
ROW_TILE = 512
SCAN_LANES = 512


def _params(n_axes):
    return pltpu.CompilerParams(
        dimension_semantics=("arbitrary",) * n_axes,
        vmem_limit_bytes=VMEM_LIMIT_BYTES)


def _rms(x, g):
    return x * lax.rsqrt(jnp.mean(x * x, axis=-1, keepdims=True) + EPS) * g


def _bdot(a, b):
    return jnp.dot(a.astype(BF16), b.astype(BF16), preferred_element_type=F32)


def _full(shape):
    return pl.BlockSpec(shape, lambda *_: (0,) * len(shape))


def _group_mean(a, n_groups, width):
    lane = lax.broadcasted_iota(jnp.int32, (1, n_groups * width), 1)
    out = jnp.zeros_like(a)
    for h in range(n_groups):
        m = (lane >= h * width) & (lane < (h + 1) * width)
        s = jnp.sum(jnp.where(m, a, 0.0), axis=-1, keepdims=True) * (1.0 / width)
        out = jnp.where(m, s, out)
    return out


def _inproj_body(x_ref, g_ref, w_ref, gv_ref, q_ref, k_ref, v_ref, u_ref, gu_ref, gvn_ref):
    h = _rms(x_ref[...], g_ref[...])
    z = jnp.dot(h.astype(BF16), w_ref[...], preferred_element_type=F32)
    q_ref[...] = (z[:, :Q_END] * ATTN_SCALE).astype(BF16)
    k_ref[...] = z[:, Q_END:K_END]
    v_ref[...] = z[:, K_END:V_END]
    u_ref[...] = z[:, V_END:S_END]
    g = jax.nn.gelu(z[:, S_END:])
    gu_ref[...] = g[:, :GM_W]
    vv = g[:, GM_W:]
    mu = _group_mean(vv, GM_HEADS, GM_HEAD_DIM)
    var = _group_mean(jnp.square(vv - mu), GM_HEADS, GM_HEAD_DIM)
    gvn_ref[...] = (vv - mu) * lax.rsqrt(var + EPS) * gv_ref[...]


def _inproj(x, g, w_bf16, gv, *, seq_major_u):
    rows, d = x.shape
    tm = min(ROW_TILE, rows)
    nt = rows // tm
    row = lambda w: pl.BlockSpec((tm, w), lambda i: (i, 0))
    if seq_major_u is None:
        u_shape, u_spec = (rows, SSM_W), row(SSM_W)
    else:
        batch, seq = seq_major_u
        per = seq // tm
        u_shape = (seq, batch * SSM_W)
        u_spec = pl.BlockSpec((tm, SSM_W), lambda i: (i % per, i // per))
    return pl.pallas_call(
        _inproj_body,
        grid=(nt,),
        in_specs=[row(d), _full((1, d)), _full((d, IN_COLS)), _full((1, GM_W))],
        out_specs=[row(ATTN_W), row(KV_W), row(KV_W), u_spec, row(GM_W), row(GM_W)],
        out_shape=[jax.ShapeDtypeStruct((rows, ATTN_W), BF16),
                   jax.ShapeDtypeStruct((rows, KV_W), F32),
                   jax.ShapeDtypeStruct((rows, KV_W), F32),
                   jax.ShapeDtypeStruct(u_shape, F32),
                   jax.ShapeDtypeStruct((rows, GM_W), F32),
                   jax.ShapeDtypeStruct((rows, GM_W), F32)],
        compiler_params=_params(1),
        name="inproj",
    )(x, g, w_bf16, gv)


def _sink_softmax(s, mask, sink):
    s = jnp.where(mask, s, -jnp.inf)
    m = jnp.maximum(jnp.max(s, axis=-1, keepdims=True), sink)
    e = jnp.exp(s - m)
    return e / (jnp.sum(e, axis=-1, keepdims=True) + jnp.exp(sink - m))


def _attn_prompt_body(sink_ref, q_ref, kc_ref, kp_ref, vc_ref, vp_ref, o_ref):
    n = pl.program_id(1)
    qi = lax.broadcasted_iota(jnp.int32, (WINDOW, 2 * WINDOW), 0)
    si = lax.broadcasted_iota(jnp.int32, (WINDOW, 2 * WINDOW), 1)
    dist = WINDOW + qi - si
    mask = (dist >= 0) & (dist < WINDOW) & ((n > 0) | (si >= WINDOW))
    kk = jnp.concatenate([kp_ref[...], kc_ref[...]], axis=0).astype(BF16)
    vv = jnp.concatenate([vp_ref[...], vc_ref[...]], axis=0).astype(BF16)
    for kh in range(N_KV_HEADS):
        kh_sl = slice(kh * HEAD_DIM, (kh + 1) * HEAD_DIM)
        k_h = kk[:, kh_sl]
        v_h = vv[:, kh_sl]
        for g in range(GQA_GROUP):
            h = kh * GQA_GROUP + g
            h_sl = slice(h * HEAD_DIM, (h + 1) * HEAD_DIM)
            s = lax.dot_general(q_ref[:, h_sl], k_h, (((1,), (1,)), ((), ())),
                                preferred_element_type=F32)
            p = _sink_softmax(s, mask, sink_ref[h])
            o_ref[:, h_sl] = jnp.dot(p.astype(BF16), v_h, preferred_element_type=F32)


def _attn_prompt(q, k, v, sinks, *, batch, seq):
    nb = seq // WINDOW
    cur = lambda w: pl.BlockSpec((WINDOW, w), lambda b, n: (b * nb + n, 0))
    prev = lambda w: pl.BlockSpec((WINDOW, w), lambda b, n: (b * nb + jnp.maximum(n - 1, 0), 0))
    return pl.pallas_call(
        _attn_prompt_body,
        grid=(batch, nb),
        in_specs=[pl.BlockSpec(memory_space=pltpu.SMEM),
                  cur(ATTN_W), cur(KV_W), prev(KV_W), cur(KV_W), prev(KV_W)],
        out_specs=cur(ATTN_W),
        out_shape=jax.ShapeDtypeStruct((batch * seq, ATTN_W), F32),
        compiler_params=_params(2),
        name="attn_prompt",
    )(sinks, q, k, k, v, v)


def _attn_sample_body(sink_ref, q_ref, k_ref, v_ref, kb_ref, vb_ref, o_ref, ko_ref, vo_ref,
                      *, bb, t):
    w = kb_ref.shape[1]
    k_new = k_ref[...].reshape(bb, t, KV_W)
    v_new = v_ref[...].reshape(bb, t, KV_W)
    ko_ref[...] = jnp.concatenate([kb_ref[:, t:, :], k_new], axis=1)
    vo_ref[...] = jnp.concatenate([vb_ref[:, t:, :], v_new], axis=1)
    pad = jnp.zeros((bb, w - t, KV_W), F32)
    kk = jnp.concatenate([kb_ref[...], k_new, pad], axis=1)
    vv = jnp.concatenate([vb_ref[...], v_new, pad], axis=1)
    q = q_ref[...].astype(F32).reshape(bb, t, ATTN_W)
    rows = GQA_GROUP * t
    ti = lax.broadcasted_iota(jnp.int32, (rows, 2 * w), 0) % t
    si = lax.broadcasted_iota(jnp.int32, (rows, 2 * w), 1)
    dist = w + ti - si
    mask = ((dist >= 0) & (dist < WINDOW))[None]
    gi = lax.broadcasted_iota(jnp.int32, (rows, 1), 0) // t
    for kh in range(N_KV_HEADS):
        kh_sl = slice(kh * HEAD_DIM, (kh + 1) * HEAD_DIM)
        k_h = kk[:, :, kh_sl].astype(BF16)
        v_h = vv[:, :, kh_sl].astype(BF16)
        q_g = jnp.concatenate(
            [q[:, :, (kh * GQA_GROUP + g) * HEAD_DIM:(kh * GQA_GROUP + g + 1) * HEAD_DIM]
             for g in range(GQA_GROUP)], axis=1)
        sink = jnp.zeros((rows, 1), F32)
        for g in range(GQA_GROUP):
            sink = jnp.where(gi == g, sink_ref[kh * GQA_GROUP + g], sink)
        s = jnp.einsum('bqd,bkd->bqk', q_g.astype(BF16), k_h, preferred_element_type=F32)
        p = _sink_softmax(s, mask, sink[None])
        o = jnp.einsum('bqk,bkd->bqd', p.astype(BF16), v_h, preferred_element_type=F32)
        for g in range(GQA_GROUP):
            h = kh * GQA_GROUP + g
            o_ref[:, h * HEAD_DIM:(h + 1) * HEAD_DIM] = (
                o[:, g * t:(g + 1) * t].reshape(bb * t, HEAD_DIM))


def _attn_sample(q, k, v, k_buf, v_buf, sinks, *, batch, t):
    w = k_buf.shape[1]
    bb = 16
    row = lambda c: pl.BlockSpec((bb * t, c), lambda i: (i, 0))
    buf = pl.BlockSpec((bb, w, KV_W), lambda i: (i, 0, 0))
    return pl.pallas_call(
        functools.partial(_attn_sample_body, bb=bb, t=t),
        grid=(batch // bb,),
        in_specs=[pl.BlockSpec(memory_space=pltpu.SMEM),
                  row(ATTN_W), row(KV_W), row(KV_W), buf, buf],
        out_specs=[row(ATTN_W), buf, buf],
        out_shape=[jax.ShapeDtypeStruct((batch * t, ATTN_W), F32),
                   jax.ShapeDtypeStruct((batch, w, KV_W), F32),
                   jax.ShapeDtypeStruct((batch, w, KV_W), F32)],
        compiler_params=_params(1),
        name="attn_sample",
    )(sinks, q, k, v, k_buf, v_buf)


def _ssm_body(u_ref, h0_ref, are_ref, aim_ref, ldt_ref, bre_ref, bim_ref, cre_ref, cim_ref,
              d_ref, wglu_ref, bglu_ref, o_ref, s_ref, abar_scr, bbar_scr, bu_scr, h_scr,
              *, r, tc):
    c = pl.program_id(0)

    @pl.when(c == 0)
    def _():
        ar = are_ref[...]
        ai = aim_ref[...]
        dt = jnp.exp(ldt_ref[...])
        decay = jnp.exp(dt * ar)
        abr = decay * jnp.cos(dt * ai)
        abi = decay * jnp.sin(dt * ai)
        den = ar * ar + ai * ai
        nr = abr - 1.0
        fr = (nr * ar + abi * ai) / den
        fi = (abi * ar - nr * ai) / den
        abar_scr[0:1, :] = abr
        abar_scr[1:2, :] = abi
        br = bre_ref[...]
        bi = bim_ref[...]
        bbar_scr[:, :SSM_FLAT] = (fr * br - fi * bi).astype(BF16)
        bbar_scr[:, SSM_FLAT:] = (fr * bi + fi * br).astype(BF16)
        h_scr[...] = h0_ref[...]

    u = u_ref[...]
    bu_scr[...] = jnp.dot(u.astype(BF16), bbar_scr[...], preferred_element_type=F32)

    for rg in range(r // 8):
        for lc in range(SSM_FLAT // SCAN_LANES):
            re_sl = slice(lc * SCAN_LANES, (lc + 1) * SCAN_LANES)
            im_sl = slice(SSM_FLAT + lc * SCAN_LANES, SSM_FLAT + (lc + 1) * SCAN_LANES)
            rg_sl = slice(rg * 8, (rg + 1) * 8)
            ar = jnp.broadcast_to(abar_scr[0:1, re_sl], (8, SCAN_LANES))
            ai = jnp.broadcast_to(abar_scr[1:2, re_sl], (8, SCAN_LANES))

            def step(t, carry, re_sl=re_sl, im_sl=im_sl, rg=rg, ar=ar, ai=ai):
                hr, hi = carry
                rows = pl.ds(pl.multiple_of(t * r + rg * 8, 8), 8)
                nr = ar * hr - ai * hi + bu_scr[rows, re_sl]
                ni = ar * hi + ai * hr + bu_scr[rows, im_sl]
                bu_scr[rows, re_sl] = nr
                bu_scr[rows, im_sl] = ni
                return nr, ni

            hr, hi = lax.fori_loop(0, tc, step, (h_scr[rg_sl, re_sl], h_scr[rg_sl, im_sl]),
                                   unroll=4)
            h_scr[rg_sl, re_sl] = hr
            h_scr[rg_sl, im_sl] = hi

    y = (jnp.dot(bu_scr[:, :SSM_FLAT].astype(BF16), cre_ref[...], preferred_element_type=F32)
         - jnp.dot(bu_scr[:, SSM_FLAT:].astype(BF16), cim_ref[...], preferred_element_type=F32))
    y = jax.nn.gelu(y + d_ref[...] * u)
    gl = jnp.dot(y.astype(BF16), wglu_ref[...], preferred_element_type=F32) + bglu_ref[...]
    o_ref[...] = y * jax.nn.sigmoid(gl)

    @pl.when(c == pl.num_programs(0) - 1)
    def _():
        s_ref[...] = h_scr[...]


def _ssm(u, h0, sp, *, r, tc):
    rows = u.shape[0]
    blk = tc * r
    return pl.pallas_call(
        functools.partial(_ssm_body, r=r, tc=tc),
        grid=(rows // blk,),
        in_specs=[pl.BlockSpec((blk, SSM_W), lambda c: (c, 0)),
                  _full((r, 2 * SSM_FLAT)),
                  _full((1, SSM_FLAT)), _full((1, SSM_FLAT)), _full((1, SSM_FLAT)),
                  _full((SSM_W, SSM_FLAT)), _full((SSM_W, SSM_FLAT)),
                  _full((SSM_FLAT, SSM_W)), _full((SSM_FLAT, SSM_W)),
                  _full((1, SSM_W)), _full((SSM_W, SSM_W)), _full((1, SSM_W))],
        out_specs=[pl.BlockSpec((blk, SSM_W), lambda c: (c, 0)),
                   _full((r, 2 * SSM_FLAT))],
        out_shape=[jax.ShapeDtypeStruct((rows, SSM_W), F32),
                   jax.ShapeDtypeStruct((r, 2 * SSM_FLAT), F32)],
        scratch_shapes=[pltpu.VMEM((2, SSM_FLAT), F32),
                        pltpu.VMEM((SSM_W, 2 * SSM_FLAT), BF16),
                        pltpu.VMEM((blk, 2 * SSM_FLAT), F32),
                        pltpu.VMEM((r, 2 * SSM_FLAT), F32)],
        compiler_params=_params(1),
        name="ssm",
    )(u, h0, sp['a_re'], sp['a_im'], sp['log_dt'], sp['b_re'], sp['b_im'],
      sp['c_re'], sp['c_im'], sp['d'], sp['w_glu'], sp['b_glu'])


def _ssm_layer_params(p, i):
    eye = jnp.eye(SSM_GROUPS, dtype=F32)

    def b_blockdiag(b):
        return jnp.einsum('gpc,gh->gchp', b, eye).reshape(SSM_W, SSM_FLAT)

    def c_blockdiag(c):
        return jnp.einsum('gcp,gh->gphc', c, eye).reshape(SSM_FLAT, SSM_W)

    return {
        'a_re': p['ssm_a_re'][i].reshape(1, SSM_FLAT),
        'a_im': p['ssm_a_im'][i].reshape(1, SSM_FLAT),
        'log_dt': jnp.repeat(p['ssm_log_dt'][i], SSM_STATE).reshape(1, SSM_FLAT),
        'b_re': b_blockdiag(p['ssm_b_re'][i]),
        'b_im': b_blockdiag(p['ssm_b_im'][i]),
        'c_re': c_blockdiag(p['ssm_c_re'][i]).astype(BF16),
        'c_im': c_blockdiag(p['ssm_c_im'][i]).astype(BF16),
        'd': p['ssm_d'][i].reshape(1, SSM_W),
        'w_glu': p['ssm_w_glu'][i].astype(BF16),
        'b_glu': p['ssm_b_glu'][i].reshape(1, SSM_W),
    }


def _outproj_body(x_ref, oa_ref, os_ref, gu_ref, gvn_ref, ws_ref, bs_ref, go_ref, w_ref, o_ref,
                  *, chunk):
    tm = x_ref.shape[0]
    ri = lax.broadcasted_iota(jnp.int32, (CHUNK, CHUNK), 0)
    ci = lax.broadcasted_iota(jnp.int32, (CHUNK, CHUNK), 1)
    causal = (ri // chunk == ci // chunk) & (ci <= ri)
    lane = lax.broadcasted_iota(jnp.int32, (1, GM_W), 1)
    zs = []
    for cblk in range(tm // CHUNK):
        vn = gvn_ref[cblk * CHUNK:(cblk + 1) * CHUNK, :].astype(BF16)
        z = bs_ref[...]
        for h in range(GM_HEADS):
            w_h = jnp.where(causal, ws_ref[h], 0.0).astype(BF16)
            z_h = jnp.dot(w_h, vn, preferred_element_type=F32)
            head = (lane >= h * GM_HEAD_DIM) & (lane < (h + 1) * GM_HEAD_DIM)
            z = z + jnp.where(head, z_h, 0.0)
        zs.append(z)
    o_gm = gu_ref[...] * jnp.concatenate(zs, axis=0)
    go = go_ref[...]
    o = jnp.concatenate([
        _rms(oa_ref[...], go[:, :ATTN_W]),
        _rms(os_ref[...], go[:, ATTN_W:ATTN_W + SSM_W]),
        _rms(o_gm, go[:, ATTN_W + SSM_W:])], axis=-1)
    o_ref[...] = x_ref[...] + jnp.dot(o.astype(BF16), w_ref[...], preferred_element_type=F32)


def _outproj(x, o_attn, o_ssm, gu, gvn, ws, bs, g_out, w_bf16, *, chunk, seq_major_ssm):
    rows, d = x.shape
    tm = min(ROW_TILE, rows)
    row = lambda w: pl.BlockSpec((tm, w), lambda i: (i, 0))
    if seq_major_ssm is None:
        ssm_spec = row(SSM_W)
    else:
        _, seq = seq_major_ssm
        per = seq // tm
        ssm_spec = pl.BlockSpec((tm, SSM_W), lambda i: (i % per, i // per))
    return pl.pallas_call(
        functools.partial(_outproj_body, chunk=chunk),
        grid=(rows // tm,),
        in_specs=[row(d), row(ATTN_W), ssm_spec, row(GM_W), row(GM_W),
                  _full((GM_HEADS, CHUNK, CHUNK)), _full((CHUNK, GM_W)),
                  _full((1, d)), _full((d, d))],
        out_specs=row(d),
        out_shape=jax.ShapeDtypeStruct((rows, d), F32),
        compiler_params=_params(1),
        name="outproj",
    )(x, o_attn, o_ssm, gu, gvn, ws, bs, g_out, w_bf16)


def _gmlp_layer_params(p, i, chunk):
    rep = CHUNK // chunk
    ws = jnp.tile(p['gmlp_w_s'][i][:, :chunk, :chunk], (1, rep, rep))
    bs = jnp.tile(p['gmlp_b_s'][i][:, :chunk], (1, rep))
    bs = jnp.repeat(bs.T, GM_HEAD_DIM, axis=1)
    return ws, bs


def _swiglu_cols(h, wg_ref, wu_ref, wd_ref, n_chunks):
    ff = wg_ref.shape[-1]
    fc = ff // n_chunks
    acc = None
    for j in range(n_chunks):
        sl = slice(j * fc, (j + 1) * fc)
        a = jnp.dot(h, wg_ref[:, sl], preferred_element_type=F32)
        b = jnp.dot(h, wu_ref[:, sl], preferred_element_type=F32)
        m = (jax.nn.silu(a) * b).astype(BF16)
        y = jnp.dot(m, wd_ref[sl, :], preferred_element_type=F32)
        acc = y if acc is None else acc + y
    return acc


def _ffn_body(x_ref, g_ref, wg_ref, wu_ref, wd_ref, o_ref):
    x = x_ref[...]
    h = _rms(x, g_ref[...]).astype(BF16)
    o_ref[...] = x + _swiglu_cols(h, wg_ref, wu_ref, wd_ref, 2)


def _ffn(x, g, wg, wu, wd):
    rows, d = x.shape
    ff = wg.shape[1]
    tm = min(ROW_TILE, rows)
    row = pl.BlockSpec((tm, d), lambda i: (i, 0))
    once = lambda shape: pl.BlockSpec(shape, lambda i: (0, 0), pipeline_mode=pl.Buffered(1))
    return pl.pallas_call(
        _ffn_body,
        grid=(rows // tm,),
        in_specs=[row, _full((1, d)), once((d, ff)), once((d, ff)), once((ff, d))],
        out_specs=row,
        out_shape=jax.ShapeDtypeStruct((rows, d), F32),
        compiler_params=_params(1),
        name="ffn",
    )(x, g, wg, wu, wd)


def _split_bf16(x):
    hi = x.astype(BF16)
    lo = (x - hi.astype(F32)).astype(BF16)
    return hi, lo


def _router_comb(hf, wr_ref, br_ref):
    h_hi, h_lo = _split_bf16(hf)
    w_hi, w_lo = _split_bf16(wr_ref[...])
    dot = lambda a, b: jnp.dot(a, b, preferred_element_type=F32)
    logits = dot(h_hi, w_hi) + (dot(h_hi, w_lo) + dot(h_lo, w_hi)) + br_ref[...]
    lane = lax.broadcasted_iota(jnp.int32, logits.shape, 1).astype(F32)
    far = float(ROUTER_LANES)
    m1 = jnp.max(logits, axis=-1, keepdims=True)
    i1 = jnp.min(jnp.where(logits == m1, lane, far), axis=-1, keepdims=True)
    rest = jnp.where(lane == i1, -jnp.inf, logits)
    m2 = jnp.max(rest, axis=-1, keepdims=True)
    i2 = jnp.min(jnp.where(rest == m2, lane, far), axis=-1, keepdims=True)
    e2 = jnp.exp(m2 - m1)
    den = 1.0 + e2
    return jnp.where(lane == i1, 1.0 / den, 0.0) + jnp.where(lane == i2, e2 / den, 0.0)


def _moe_body(x_ref, g_ref, wr_ref, br_ref, wg_ref, wu_ref, wd_ref, o_ref, h_scr, comb_scr):
    e = pl.program_id(1)

    @pl.when(e == 0)
    def _():
        x = x_ref[...]
        hf = _rms(x, g_ref[...])
        h_scr[...] = hf.astype(BF16)
        comb_scr[...] = _router_comb(hf, wr_ref, br_ref)
        o_ref[...] = x

    y = _swiglu_cols(h_scr[...], wg_ref.at[0], wu_ref.at[0], wd_ref.at[0], 1)
    lane = lax.broadcasted_iota(jnp.int32, (1, ROUTER_LANES), 1)
    ce = jnp.sum(jnp.where(lane == e, comb_scr[...], 0.0), axis=-1, keepdims=True)
    o_ref[...] += ce * y


def _moe(x, g, wr_pad, br_pad, wg, wu, wd):
    rows, d = x.shape
    n_e, _, ff = wg.shape
    tm = min(ROW_TILE, rows)
    row = pl.BlockSpec((tm, d), lambda i, e: (i, 0))
    return pl.pallas_call(
        _moe_body,
        grid=(rows // tm, n_e),
        in_specs=[row, _full((1, d)), _full((d, ROUTER_LANES)), _full((1, ROUTER_LANES)),
                  pl.BlockSpec((1, d, ff), lambda i, e: (e, 0, 0)),
                  pl.BlockSpec((1, d, ff), lambda i, e: (e, 0, 0)),
                  pl.BlockSpec((1, ff, d), lambda i, e: (e, 0, 0))],
        out_specs=row,
        out_shape=jax.ShapeDtypeStruct((rows, d), F32),
        scratch_shapes=[pltpu.VMEM((tm, d), BF16), pltpu.VMEM((tm, ROUTER_LANES), F32)],
        compiler_params=_params(2),
        name="moe",
    )(x, g, wr_pad, br_pad, wg, wu, wd)


def _final_norm_body(x_ref, g_ref, o_ref):
    o_ref[...] = _rms(x_ref[...], g_ref[...])


def _final_norm(x, g):
    rows, d = x.shape
    tm = min(2 * ROW_TILE, rows)
    row = pl.BlockSpec((tm, d), lambda i: (i, 0))
    return pl.pallas_call(
        _final_norm_body, grid=(rows // tm,),
        in_specs=[row, _full((1, d))], out_specs=row,
        out_shape=jax.ShapeDtypeStruct((rows, d), F32),
        compiler_params=_params(1), name="final_norm",
    )(x, g)


def _trunk(x, p, past):
    b, l, d = x.shape
    depth = p['w_in'].shape[0]
    rows = b * l
    xr = x.reshape(rows, d)
    prompt = past is None
    new = {'k': [], 'v': [], 're': [], 'im': [], 'gv': []}
    for i in range(depth):
        g_mix = p['norm_mix_g'][i].reshape(1, d)
        gv = p['gmlp_v_norm_g'][i].reshape(1, GM_W)
        q, k, v, u, gu, gvn = _inproj(xr, g_mix, p['w_in'][i].astype(BF16), gv,
                                      seq_major_u=(b, l) if prompt else None)
        sinks = p['attn_sinks'][i]
        sp = _ssm_layer_params(p, i)
        if prompt:
            o_attn = _attn_prompt(q, k, v, sinks, batch=b, seq=l)
            nw = min(WINDOW, l)
            k_win = k.reshape(b, l, N_KV_HEADS, HEAD_DIM)[:, l - nw:]
            v_win = v.reshape(b, l, N_KV_HEADS, HEAD_DIM)[:, l - nw:]
            h0 = jnp.zeros((b, 2 * SSM_FLAT), F32)
            o_ssm, s_fin = _ssm(u.reshape(l * b, SSM_W), h0, sp, r=b, tc=CHUNK)
            o_ssm = o_ssm.reshape(l, b * SSM_W)
        else:
            w = past[0].shape[2]
            o_attn, k_win, v_win = _attn_sample(
                q, k, v, past[0][i].reshape(b, w, KV_W), past[1][i].reshape(b, w, KV_W),
                sinks, batch=b, t=l)
            k_win = k_win.reshape(b, w, N_KV_HEADS, HEAD_DIM)
            v_win = v_win.reshape(b, w, N_KV_HEADS, HEAD_DIM)
            h0 = jnp.concatenate([past[2][i].reshape(b, SSM_FLAT),
                                  past[3][i].reshape(b, SSM_FLAT)], axis=1)
            u_sm = u.reshape(b, l, SSM_W).transpose(1, 0, 2).reshape(rows, SSM_W)
            o_ssm, s_fin = _ssm(u_sm, h0, sp, r=b, tc=l)
            o_ssm = o_ssm.reshape(l, b, SSM_W).transpose(1, 0, 2).reshape(rows, SSM_W)
            new['gv'].append(gvn.reshape(b, l, GM_HEADS, GM_HEAD_DIM))
        ws, bs = _gmlp_layer_params(p, i, min(l, CHUNK))
        xr = _outproj(xr, o_attn, o_ssm, gu, gvn, ws, bs,
                      p['mix_out_norm_g'][i].reshape(1, d), p['w_out'][i].astype(BF16),
                      chunk=min(l, CHUNK), seq_major_ssm=(b, l) if prompt else None)
        g_ffn = p['norm_ffn_g'][i].reshape(1, d)
        j = i // 2
        if i % 2 == 0:
            xr = _ffn(xr, g_ffn, p['ffn_w_gate'][j].astype(BF16), p['ffn_w_up'][j].astype(BF16),
                      p['ffn_w_down'][j].astype(BF16))
        else:
            wr = jnp.pad(p['moe_w_router'][j], ((0, 0), (0, ROUTER_LANES - N_EXPERTS)))
            br = jnp.pad(p['moe_b_router'][j], (0, ROUTER_LANES - N_EXPERTS),
                         constant_values=NEG_BIG).reshape(1, ROUTER_LANES)
            xr = _moe(xr, g_ffn, wr, br, p['moe_w_gate'][j].astype(BF16),
                      p['moe_w_up'][j].astype(BF16), p['moe_w_down'][j].astype(BF16))
        new['k'].append(k_win)
        new['v'].append(v_win)
        new['re'].append(s_fin[:, :SSM_FLAT].reshape(b, SSM_GROUPS, SSM_STATE))
        new['im'].append(s_fin[:, SSM_FLAT:].reshape(b, SSM_GROUPS, SSM_STATE))
    y = _final_norm(xr, p['final_norm_g'].reshape(1, d)).reshape(b, l, d)
    return y, new


def kernel(x_prompt, x_sample, cache_k_win, cache_v_win, state_ssm_re, state_ssm_im,
           norm_mix_g, w_in, attn_sinks, ssm_a_re, ssm_a_im, ssm_log_dt, ssm_b_re, ssm_b_im,
           ssm_c_re, ssm_c_im, ssm_d, ssm_w_glu, ssm_b_glu, gmlp_v_norm_g, gmlp_w_s, gmlp_b_s,
           mix_out_norm_g, w_out, norm_ffn_g, ffn_w_gate, ffn_w_up, ffn_w_down,
           moe_w_router, moe_b_router, moe_w_gate, moe_w_up, moe_w_down, final_norm_g):
    params = {
        'norm_mix_g': norm_mix_g, 'w_in': w_in, 'attn_sinks': attn_sinks,
        'ssm_a_re': ssm_a_re, 'ssm_a_im': ssm_a_im, 'ssm_log_dt': ssm_log_dt,
        'ssm_b_re': ssm_b_re, 'ssm_b_im': ssm_b_im, 'ssm_c_re': ssm_c_re, 'ssm_c_im': ssm_c_im,
        'ssm_d': ssm_d, 'ssm_w_glu': ssm_w_glu, 'ssm_b_glu': ssm_b_glu,
        'gmlp_v_norm_g': gmlp_v_norm_g, 'gmlp_w_s': gmlp_w_s, 'gmlp_b_s': gmlp_b_s,
        'mix_out_norm_g': mix_out_norm_g, 'w_out': w_out, 'norm_ffn_g': norm_ffn_g,
        'ffn_w_gate': ffn_w_gate, 'ffn_w_up': ffn_w_up, 'ffn_w_down': ffn_w_down,
        'moe_w_router': moe_w_router, 'moe_b_router': moe_b_router,
        'moe_w_gate': moe_w_gate, 'moe_w_up': moe_w_up, 'moe_w_down': moe_w_down,
        'final_norm_g': final_norm_g,
    }
    y_p, st_p = _trunk(x_prompt, params, None)
    y_s, st_s = _trunk(x_sample, params,
                       (cache_k_win, cache_v_win, state_ssm_re, state_ssm_im))
    return (y_p, y_s,
            jnp.stack(st_p['k']), jnp.stack(st_p['v']),
            jnp.stack(st_p['re']), jnp.stack(st_p['im']),
            jnp.stack(st_s['k']), jnp.stack(st_s['v']),
            jnp.stack(st_s['re']), jnp.stack(st_s['im']),
            jnp.stack(st_s['gv']))
```

```python
import functools
import math

import jax
import jax.numpy as jnp
from jax import lax
from jax.experimental import pallas as pl
from jax.experimental.pallas import tpu as pltpu

F32 = jnp.float32
BF16 = jnp.bfloat16

EPS = 1e-6
HEAD_DIM = 64
N_HEADS = 8
N_KV_HEADS = 2
GQA_GROUP = N_HEADS // N_KV_HEADS
ATTN_W = N_HEADS * HEAD_DIM
KV_W = N_KV_HEADS * HEAD_DIM
WINDOW = 128
ATTN_SCALE = 1.0 / math.sqrt(HEAD_DIM)
SSM_W = 256
SSM_GROUP_CH = 16
SSM_GROUPS = 16
SSM_STATE = 64
SSM_FLAT = SSM_GROUPS * SSM_STATE
GM_W = 256
GM_HEADS = 4
GM_HEAD_DIM = 64
CHUNK = 128
Q_END = ATTN_W
K_END = Q_END + KV_W
V_END = K_END + KV_W
S_END = V_END + SSM_W
IN_COLS = S_END + 2 * GM_W
N_EXPERTS = 8
ROUTER_LANES = 128
NEG_BIG = -1e30

VMEM_LIMIT_BYTES = 56 * 1024 * 1024
ROW_TILE = 512
MOE_ROW_TILE = 1024
MOE_CHUNK = 128
MOE_SCATTER_ROWS = 256
SCAN_LANES = 512


def _params(n_axes):
    return pltpu.CompilerParams(
        dimension_semantics=("arbitrary",) * n_axes,
        vmem_limit_bytes=VMEM_LIMIT_BYTES)


def _rms(x, g):
    return x * lax.rsqrt(jnp.mean(x * x, axis=-1, keepdims=True) + EPS) * g


def _bdot(a, b):
    return jnp.dot(a.astype(BF16), b.astype(BF16), preferred_element_type=F32)


def _full(shape):
    return pl.BlockSpec(shape, lambda *_: (0,) * len(shape))


def _group_mean(a, n_groups, width):
    lane = lax.broadcasted_iota(jnp.int32, (1, n_groups * width), 1)
    out = jnp.zeros_like(a)
    for h in range(n_groups):
        m = (lane >= h * width) & (lane < (h + 1) * width)
        s = jnp.sum(jnp.where(m, a, 0.0), axis=-1, keepdims=True) * (1.0 / width)
        out = jnp.where(m, s, out)
    return out


def _inproj_body(x_ref, g_ref, w_ref, gv_ref, q_ref, k_ref, v_ref, u_ref, gu_ref, gvn_ref):
    h = _rms(x_ref[...], g_ref[...])
    z = jnp.dot(h.astype(BF16), w_ref[...], preferred_element_type=F32)
    q_ref[...] = (z[:, :Q_END] * ATTN_SCALE).astype(BF16)
    k_ref[...] = z[:, Q_END:K_END]
    v_ref[...] = z[:, K_END:V_END]
    u_ref[...] = z[:, V_END:S_END]
    g = jax.nn.gelu(z[:, S_END:])
    gu_ref[...] = g[:, :GM_W]
    vv = g[:, GM_W:]
    mu = _group_mean(vv, GM_HEADS, GM_HEAD_DIM)
    var = _group_mean(jnp.square(vv - mu), GM_HEADS, GM_HEAD_DIM)
    gvn_ref[...] = (vv - mu) * lax.rsqrt(var + EPS) * gv_ref[...]


def _layer(shape, j):
    return pl.BlockSpec((None,) + tuple(shape), lambda *_: (j,) + (0,) * len(shape))


def _inproj(x, g, w_bf16, gv, j, *, seq_major_u):
    rows, d = x.shape
    tm = min(ROW_TILE, rows)
    nt = rows // tm
    row = lambda w: pl.BlockSpec((tm, w), lambda i: (i, 0))
    if seq_major_u is None:
        u_shape, u_spec = (rows, SSM_W), row(SSM_W)
    else:
        batch, seq = seq_major_u
        per = seq // tm
        u_shape = (seq, batch * SSM_W)
        u_spec = pl.BlockSpec((tm, SSM_W), lambda i: (i % per, i // per))
    return pl.pallas_call(
        _inproj_body,
        grid=(nt,),
        in_specs=[row(d), _full((1, d)), _layer((d, IN_COLS), j), _full((1, GM_W))],
        out_specs=[row(ATTN_W), row(KV_W), row(KV_W), u_spec, row(GM_W), row(GM_W)],
        out_shape=[jax.ShapeDtypeStruct((rows, ATTN_W), BF16),
                   jax.ShapeDtypeStruct((rows, KV_W), F32),
                   jax.ShapeDtypeStruct((rows, KV_W), F32),
                   jax.ShapeDtypeStruct(u_shape, F32),
                   jax.ShapeDtypeStruct((rows, GM_W), F32),
                   jax.ShapeDtypeStruct((rows, GM_W), F32)],
        compiler_params=_params(1),
        name="inproj",
    )(x, g, w_bf16, gv)


def _sink_softmax(s, mask, sink):
    s = jnp.where(mask, s, -jnp.inf)
    m = jnp.maximum(jnp.max(s, axis=-1, keepdims=True), sink)
    e = jnp.exp(s - m)
    return e / (jnp.sum(e, axis=-1, keepdims=True) + jnp.exp(sink - m))


def _attn_prompt_body(sink_ref, q_ref, kc_ref, kp_ref, vc_ref, vp_ref, o_ref):
    n = pl.program_id(1)
    qi = lax.broadcasted_iota(jnp.int32, (WINDOW, 2 * WINDOW), 0)
    si = lax.broadcasted_iota(jnp.int32, (WINDOW, 2 * WINDOW), 1)
    dist = WINDOW + qi - si
    mask = (dist >= 0) & (dist < WINDOW) & ((n > 0) | (si >= WINDOW))
    kk = jnp.concatenate([kp_ref[...], kc_ref[...]], axis=0).astype(BF16)
    vv = jnp.concatenate([vp_ref[...], vc_ref[...]], axis=0).astype(BF16)
    for kh in range(N_KV_HEADS):
        kh_sl = slice(kh * HEAD_DIM, (kh + 1) * HEAD_DIM)
        k_h = kk[:, kh_sl]
        v_h = vv[:, kh_sl]
        for g in range(GQA_GROUP):
            h = kh * GQA_GROUP + g
            h_sl = slice(h * HEAD_DIM, (h + 1) * HEAD_DIM)
            s = lax.dot_general(q_ref[:, h_sl], k_h, (((1,), (1,)), ((), ())),
                                preferred_element_type=F32)
            p = _sink_softmax(s, mask, sink_ref[h])
            o_ref[:, h_sl] = jnp.dot(p.astype(BF16), v_h, preferred_element_type=F32)


def _attn_prompt(q, k, v, sinks, *, batch, seq):
    nb = seq // WINDOW
    cur = lambda w: pl.BlockSpec((WINDOW, w), lambda b, n: (b * nb + n, 0))
    prev = lambda w: pl.BlockSpec((WINDOW, w), lambda b, n: (b * nb + jnp.maximum(n - 1, 0), 0))
    return pl.pallas_call(
        _attn_prompt_body,
        grid=(batch, nb),
        in_specs=[pl.BlockSpec(memory_space=pltpu.SMEM),
                  cur(ATTN_W), cur(KV_W), prev(KV_W), cur(KV_W), prev(KV_W)],
        out_specs=cur(ATTN_W),
        out_shape=jax.ShapeDtypeStruct((batch * seq, ATTN_W), F32),
        compiler_params=_params(2),
        name="attn_prompt",
    )(sinks, q, k, k, v, v)


def _attn_sample_body(sink_ref, q_ref, k_ref, v_ref, kb_ref, vb_ref, o_ref, ko_ref, vo_ref,
                      *, bb, t):
    w = kb_ref.shape[1]
    k_new = k_ref[...].reshape(bb, t, KV_W)
    v_new = v_ref[...].reshape(bb, t, KV_W)
    ko_ref[...] = jnp.concatenate([kb_ref[:, t:, :], k_new], axis=1)
    vo_ref[...] = jnp.concatenate([vb_ref[:, t:, :], v_new], axis=1)
    pad = jnp.zeros((bb, w - t, KV_W), F32)
    kk = jnp.concatenate([kb_ref[...], k_new, pad], axis=1)
    vv = jnp.concatenate([vb_ref[...], v_new, pad], axis=1)
    q = q_ref[...].astype(F32).reshape(bb, t, ATTN_W)
    rows = GQA_GROUP * t
    ti = lax.broadcasted_iota(jnp.int32, (rows, 2 * w), 0) % t
    si = lax.broadcasted_iota(jnp.int32, (rows, 2 * w), 1)
    dist = w + ti - si
    mask = ((dist >= 0) & (dist < WINDOW))[None]
    gi = lax.broadcasted_iota(jnp.int32, (rows, 1), 0) // t
    for kh in range(N_KV_HEADS):
        kh_sl = slice(kh * HEAD_DIM, (kh + 1) * HEAD_DIM)
        k_h = kk[:, :, kh_sl].astype(BF16)
        v_h = vv[:, :, kh_sl].astype(BF16)
        q_g = jnp.concatenate(
            [q[:, :, (kh * GQA_GROUP + g) * HEAD_DIM:(kh * GQA_GROUP + g + 1) * HEAD_DIM]
             for g in range(GQA_GROUP)], axis=1)
        sink = jnp.zeros((rows, 1), F32)
        for g in range(GQA_GROUP):
            sink = jnp.where(gi == g, sink_ref[kh * GQA_GROUP + g], sink)
        s = jnp.einsum('bqd,bkd->bqk', q_g.astype(BF16), k_h, preferred_element_type=F32)
        p = _sink_softmax(s, mask, sink[None])
        o = jnp.einsum('bqk,bkd->bqd', p.astype(BF16), v_h, preferred_element_type=F32)
        for g in range(GQA_GROUP):
            h = kh * GQA_GROUP + g
            o_ref[:, h * HEAD_DIM:(h + 1) * HEAD_DIM] = (
                o[:, g * t:(g + 1) * t].reshape(bb * t, HEAD_DIM))


def _attn_sample(q, k, v, k_buf, v_buf, sinks, *, batch, t):
    w = k_buf.shape[1]
    bb = 16
    row = lambda c: pl.BlockSpec((bb * t, c), lambda i: (i, 0))
    buf = pl.BlockSpec((bb, w, KV_W), lambda i: (i, 0, 0))
    return pl.pallas_call(
        functools.partial(_attn_sample_body, bb=bb, t=t),
        grid=(batch // bb,),
        in_specs=[pl.BlockSpec(memory_space=pltpu.SMEM),
                  row(ATTN_W), row(KV_W), row(KV_W), buf, buf],
        out_specs=[row(ATTN_W), buf, buf],
        out_shape=[jax.ShapeDtypeStruct((batch * t, ATTN_W), F32),
                   jax.ShapeDtypeStruct((batch, w, KV_W), F32),
                   jax.ShapeDtypeStruct((batch, w, KV_W), F32)],
        compiler_params=_params(1),
        name="attn_sample",
    )(sinks, q, k, v, k_buf, v_buf)


def _ssm_body(u_ref, h0_ref, are_ref, aim_ref, ldt_ref, bre_ref, bim_ref, cre_ref, cim_ref,
              d_ref, wglu_ref, bglu_ref, o_ref, s_ref, abar_scr, bbar_scr, bu_scr, h_scr,
              *, r, tc):
    c = pl.program_id(0)

    @pl.when(c == 0)
    def _():
        ar = are_ref[...]
        ai = aim_ref[...]
        dt = jnp.exp(ldt_ref[...])
        decay = jnp.exp(dt * ar)
        abr = decay * jnp.cos(dt * ai)
        abi = decay * jnp.sin(dt * ai)
        den = ar * ar + ai * ai
        nr = abr - 1.0
        fr = (nr * ar + abi * ai) / den
        fi = (abi * ar - nr * ai) / den
        abar_scr[0:1, :] = abr
        abar_scr[1:2, :] = abi
        br = bre_ref[...]
        bi = bim_ref[...]
        bbar_scr[:, :SSM_FLAT] = (fr * br - fi * bi).astype(BF16)
        bbar_scr[:, SSM_FLAT:] = (fr * bi + fi * br).astype(BF16)
        h_scr[...] = h0_ref[...]

    u = u_ref[...]
    bu_scr[...] = jnp.dot(u.astype(BF16), bbar_scr[...], preferred_element_type=F32)

    for rg in range(r // 8):
        for lc in range(SSM_FLAT // SCAN_LANES):
            re_sl = slice(lc * SCAN_LANES, (lc + 1) * SCAN_LANES)
            im_sl = slice(SSM_FLAT + lc * SCAN_LANES, SSM_FLAT + (lc + 1) * SCAN_LANES)
            rg_sl = slice(rg * 8, (rg + 1) * 8)
            ar = jnp.broadcast_to(abar_scr[0:1, re_sl], (8, SCAN_LANES))
            ai = jnp.broadcast_to(abar_scr[1:2, re_sl], (8, SCAN_LANES))

            def step(t, carry, re_sl=re_sl, im_sl=im_sl, rg=rg, ar=ar, ai=ai):
                hr, hi = carry
                rows = pl.ds(pl.multiple_of(t * r + rg * 8, 8), 8)
                nr = ar * hr - ai * hi + bu_scr[rows, re_sl]
                ni = ar * hi + ai * hr + bu_scr[rows, im_sl]
                bu_scr[rows, re_sl] = nr
                bu_scr[rows, im_sl] = ni
                return nr, ni

            hr, hi = lax.fori_loop(0, tc, step, (h_scr[rg_sl, re_sl], h_scr[rg_sl, im_sl]),
                                   unroll=4)
            h_scr[rg_sl, re_sl] = hr
            h_scr[rg_sl, im_sl] = hi

    y = (jnp.dot(bu_scr[:, :SSM_FLAT].astype(BF16), cre_ref[...], preferred_element_type=F32)
         - jnp.dot(bu_scr[:, SSM_FLAT:].astype(BF16), cim_ref[...], preferred_element_type=F32))
    y = jax.nn.gelu(y + d_ref[...] * u)
    gl = jnp.dot(y.astype(BF16), wglu_ref[...], preferred_element_type=F32) + bglu_ref[...]
    o_ref[...] = y * jax.nn.sigmoid(gl)

    @pl.when(c == pl.num_programs(0) - 1)
    def _():
        s_ref[...] = h_scr[...]


def _ssm(u, h0, sp, *, r, tc):
    rows = u.shape[0]
    blk = tc * r
    return pl.pallas_call(
        functools.partial(_ssm_body, r=r, tc=tc),
        grid=(rows // blk,),
        in_specs=[pl.BlockSpec((blk, SSM_W), lambda c: (c, 0)),
                  _full((r, 2 * SSM_FLAT)),
                  _full((1, SSM_FLAT)), _full((1, SSM_FLAT)), _full((1, SSM_FLAT)),
                  _full((SSM_W, SSM_FLAT)), _full((SSM_W, SSM_FLAT)),
                  _full((SSM_FLAT, SSM_W)), _full((SSM_FLAT, SSM_W)),
                  _full((1, SSM_W)), _full((SSM_W, SSM_W)), _full((1, SSM_W))],
        out_specs=[pl.BlockSpec((blk, SSM_W), lambda c: (c, 0)),
                   _full((r, 2 * SSM_FLAT))],
        out_shape=[jax.ShapeDtypeStruct((rows, SSM_W), F32),
                   jax.ShapeDtypeStruct((r, 2 * SSM_FLAT), F32)],
        scratch_shapes=[pltpu.VMEM((2, SSM_FLAT), F32),
                        pltpu.VMEM((SSM_W, 2 * SSM_FLAT), BF16),
                        pltpu.VMEM((blk, 2 * SSM_FLAT), F32),
                        pltpu.VMEM((r, 2 * SSM_FLAT), F32)],
        compiler_params=_params(1),
        name="ssm",
    )(u, h0, sp['a_re'], sp['a_im'], sp['log_dt'], sp['b_re'], sp['b_im'],
      sp['c_re'], sp['c_im'], sp['d'], sp['w_glu'], sp['b_glu'])


def _ssm_layer_params(p, i):
    eye = jnp.eye(SSM_GROUPS, dtype=F32)

    def b_blockdiag(b):
        return jnp.einsum('gpc,gh->gchp', b, eye).reshape(SSM_W, SSM_FLAT)

    def c_blockdiag(c):
        return jnp.einsum('gcp,gh->gphc', c, eye).reshape(SSM_FLAT, SSM_W)

    return {
        'a_re': p['ssm_a_re'][i].reshape(1, SSM_FLAT),
        'a_im': p['ssm_a_im'][i].reshape(1, SSM_FLAT),
        'log_dt': jnp.repeat(p['ssm_log_dt'][i], SSM_STATE).reshape(1, SSM_FLAT),
        'b_re': b_blockdiag(p['ssm_b_re'][i]),
        'b_im': b_blockdiag(p['ssm_b_im'][i]),
        'c_re': c_blockdiag(p['ssm_c_re'][i]).astype(BF16),
        'c_im': c_blockdiag(p['ssm_c_im'][i]).astype(BF16),
        'd': p['ssm_d'][i].reshape(1, SSM_W),
        'w_glu': p['ssm_w_glu'][i].astype(BF16),
        'b_glu': p['ssm_b_glu'][i].reshape(1, SSM_W),
    }


def _outproj_body(x_ref, oa_ref, os_ref, gu_ref, gvn_ref, ws_ref, bs_ref, go_ref, w_ref, o_ref,
                  *, chunk):
    tm = x_ref.shape[0]
    ri = lax.broadcasted_iota(jnp.int32, (CHUNK, CHUNK), 0)
    ci = lax.broadcasted_iota(jnp.int32, (CHUNK, CHUNK), 1)
    causal = (ri // chunk == ci // chunk) & (ci <= ri)
    lane = lax.broadcasted_iota(jnp.int32, (1, GM_W), 1)
    zs = []
    for cblk in range(tm // CHUNK):
        vn = gvn_ref[cblk * CHUNK:(cblk + 1) * CHUNK, :].astype(BF16)
        z = bs_ref[...]
        for h in range(GM_HEADS):
            w_h = jnp.where(causal, ws_ref[h], 0.0).astype(BF16)
            z_h = jnp.dot(w_h, vn, preferred_element_type=F32)
            head = (lane >= h * GM_HEAD_DIM) & (lane < (h + 1) * GM_HEAD_DIM)
            z = z + jnp.where(head, z_h, 0.0)
        zs.append(z)
    o_gm = gu_ref[...] * jnp.concatenate(zs, axis=0)
    go = go_ref[...]
    o = jnp.concatenate([
        _rms(oa_ref[...], go[:, :ATTN_W]),
        _rms(os_ref[...], go[:, ATTN_W:ATTN_W + SSM_W]),
        _rms(o_gm, go[:, ATTN_W + SSM_W:])], axis=-1)
    o_ref[...] = x_ref[...] + jnp.dot(o.astype(BF16), w_ref[...], preferred_element_type=F32)


def _outproj(x, o_attn, o_ssm, gu, gvn, ws, bs, g_out, w_bf16, j, *, chunk, seq_major_ssm):
    rows, d = x.shape
    tm = min(ROW_TILE, rows)
    row = lambda w: pl.BlockSpec((tm, w), lambda i: (i, 0))
    if seq_major_ssm is None:
        ssm_spec = row(SSM_W)
    else:
        _, seq = seq_major_ssm
        per = seq // tm
        ssm_spec = pl.BlockSpec((tm, SSM_W), lambda i: (i % per, i // per))
    return pl.pallas_call(
        functools.partial(_outproj_body, chunk=chunk),
        grid=(rows // tm,),
        in_specs=[row(d), row(ATTN_W), ssm_spec, row(GM_W), row(GM_W),
                  _full((GM_HEADS, CHUNK, CHUNK)), _full((CHUNK, GM_W)),
                  _full((1, d)), _layer((d, d), j)],
        out_specs=row(d),
        out_shape=jax.ShapeDtypeStruct((rows, d), F32),
        compiler_params=_params(1),
        name="outproj",
    )(x, o_attn, o_ssm, gu, gvn, ws, bs, g_out, w_bf16)


def _gmlp_layer_params(p, i, chunk):
    rep = CHUNK // chunk
    ws = jnp.tile(p['gmlp_w_s'][i][:, :chunk, :chunk], (1, rep, rep))
    bs = jnp.tile(p['gmlp_b_s'][i][:, :chunk], (1, rep))
    bs = jnp.repeat(bs.T, GM_HEAD_DIM, axis=1)
    return ws, bs


def _swiglu_cols(h, wg_ref, wu_ref, wd_ref, n_chunks):
    ff = wg_ref.shape[-1]
    fc = ff // n_chunks
    acc = None
    for j in range(n_chunks):
        sl = slice(j * fc, (j + 1) * fc)
        a = jnp.dot(h, wg_ref[:, sl], preferred_element_type=F32)
        b = jnp.dot(h, wu_ref[:, sl], preferred_element_type=F32)
        m = (jax.nn.silu(a) * b).astype(BF16)
        y = jnp.dot(m, wd_ref[sl, :], preferred_element_type=F32)
        acc = y if acc is None else acc + y
    return acc


def _ffn_body(x_ref, g_ref, wg_ref, wu_ref, wd_ref, o_ref):
    x = x_ref[...]
    h = _rms(x, g_ref[...]).astype(BF16)
    o_ref[...] = x + _swiglu_cols(h, wg_ref, wu_ref, wd_ref, 2)


def _ffn(x, g, wg, wu, wd, j):
    rows, d = x.shape
    ff = wg.shape[-1]
    tm = min(ROW_TILE, rows)
    row = pl.BlockSpec((tm, d), lambda i: (i, 0))
    once = lambda shape: pl.BlockSpec((None,) + shape, lambda i: (j, 0, 0),
                                      pipeline_mode=pl.Buffered(1))
    return pl.pallas_call(
        _ffn_body,
        grid=(rows // tm,),
        in_specs=[row, _full((1, d)), once((d, ff)), once((d, ff)), once((ff, d))],
        out_specs=row,
        out_shape=jax.ShapeDtypeStruct((rows, d), F32),
        compiler_params=_params(1),
        name="ffn",
    )(x, g, wg, wu, wd)


def _split_bf16(x):
    hi = x.astype(BF16)
    lo = (x - hi.astype(F32)).astype(BF16)
    return hi, lo


def _router_comb(hf, wr_ref, br_ref):
    h_hi, h_lo = _split_bf16(hf)
    w_hi, w_lo = _split_bf16(wr_ref[...])
    dot = lambda a, b: jnp.dot(a, b, preferred_element_type=F32)
    logits = dot(h_hi, w_hi) + (dot(h_hi, w_lo) + dot(h_lo, w_hi)) + br_ref[...]
    lane = lax.broadcasted_iota(jnp.int32, logits.shape, 1).astype(F32)
    far = float(ROUTER_LANES)
    m1 = jnp.max(logits, axis=-1, keepdims=True)
    i1 = jnp.min(jnp.where(logits == m1, lane, far), axis=-1, keepdims=True)
    rest = jnp.where(lane == i1, -jnp.inf, logits)
    m2 = jnp.max(rest, axis=-1, keepdims=True)
    i2 = jnp.min(jnp.where(rest == m2, lane, far), axis=-1, keepdims=True)
    e2 = jnp.exp(m2 - m1)
    den = 1.0 + e2
    comb = jnp.where(lane == i1, 1.0 / den, 0.0) + jnp.where(lane == i2, e2 / den, 0.0)
    sel = jnp.where((lane == i1) | (lane == i2), 1.0, 0.0)
    return comb, sel


def _moe_body(x_ref, g_ref, wr_ref, br_ref, wg_ref, wu_ref, wd_ref, o_ref,
              h_scr, gate_scr, rank_scr, rank_t_scr, cnt_smem):
    e = pl.program_id(1)
    tm = x_ref.shape[0]

    @pl.when(e == 0)
    def _():
        x = x_ref[...]
        hf = _rms(x, g_ref[...])
        h_scr[...] = hf.astype(BF16)
        comb, sel = _router_comb(hf, wr_ref, br_ref)
        gate_scr[...] = comb
        ri = lax.broadcasted_iota(jnp.int32, (tm, tm), 0)
        ci = lax.broadcasted_iota(jnp.int32, (tm, tm), 1)
        lower = jnp.where(ci < ri, 1.0, 0.0).astype(BF16)
        rank = jnp.dot(lower, sel.astype(BF16), preferred_element_type=F32)
        rank = jnp.where(sel > 0.0, rank, -1.0)
        rank_scr[...] = rank
        rank_t_scr[...] = rank.T
        cnt = jnp.sum(sel, axis=0, keepdims=True)
        for k in range(N_EXPERTS):
            cnt_smem[k] = cnt[0, k].astype(jnp.int32)
        o_ref[...] = x

    lane = lax.broadcasted_iota(jnp.int32, (1, ROUTER_LANES), 1)
    pick = lane == e
    rank_col = jnp.sum(jnp.where(pick, rank_scr[...], 0.0), axis=-1, keepdims=True)
    gate_col = jnp.sum(jnp.where(pick, gate_scr[...], 0.0), axis=-1, keepdims=True)
    rank_row = rank_t_scr[pl.ds(e, 1), :]
    n_chunks = (cnt_smem[e] + (MOE_CHUNK - 1)) // MOE_CHUNK
    wg, wu, wd = wg_ref, wu_ref, wd_ref

    def chunk(c, carry):
        base = (c * MOE_CHUNK).astype(F32)
        slot_col = lax.broadcasted_iota(jnp.int32, (MOE_CHUNK, 1), 0).astype(F32) + base
        slot_row = lax.broadcasted_iota(jnp.int32, (1, MOE_CHUNK), 1).astype(F32) + base
        gather = jnp.where(rank_row == slot_col, 1.0, 0.0).astype(BF16)
        xe = jnp.dot(gather, h_scr[...], preferred_element_type=F32).astype(BF16)
        ye = _swiglu_cols(xe, wg, wu, wd, 1).astype(BF16)
        sr = min(MOE_SCATTER_ROWS, tm)
        for rb in range(tm // sr):
            sl = slice(rb * sr, (rb + 1) * sr)
            scatter = jnp.where(rank_col[sl] == slot_row, 1.0, 0.0).astype(BF16)
            o_ref[sl, :] += gate_col[sl] * jnp.dot(scatter, ye, preferred_element_type=F32)
        return carry

    lax.fori_loop(0, n_chunks, chunk, 0)


def _moe(x, g, wr_pad, br_pad, wg, wu, wd, j):
    rows, d = x.shape
    _, n_e, _, ff = wg.shape
    tm = min(MOE_ROW_TILE, rows)
    row = pl.BlockSpec((tm, d), lambda i, e: (i, 0))
    return pl.pallas_call(
        _moe_body,
        grid=(rows // tm, n_e),
        in_specs=[row, _full((1, d)), _full((d, ROUTER_LANES)), _full((1, ROUTER_LANES)),
                  pl.BlockSpec((None, None, d, ff), lambda i, e: (j, e, 0, 0)),
                  pl.BlockSpec((None, None, d, ff), lambda i, e: (j, e, 0, 0)),
                  pl.BlockSpec((None, None, ff, d), lambda i, e: (j, e, 0, 0))],
        out_specs=row,
        out_shape=jax.ShapeDtypeStruct((rows, d), F32),
        scratch_shapes=[pltpu.VMEM((tm, d), BF16),
                        pltpu.VMEM((tm, ROUTER_LANES), F32),
                        pltpu.VMEM((tm, ROUTER_LANES), F32),
                        pltpu.VMEM((ROUTER_LANES, tm), F32),
                        pltpu.SMEM((N_EXPERTS,), jnp.int32)],
        compiler_params=_params(2),
        name="moe",
    )(x, g, wr_pad, br_pad, wg, wu, wd)


def _final_norm_body(x_ref, g_ref, o_ref):
    o_ref[...] = _rms(x_ref[...], g_ref[...])


def _final_norm(x, g):
    rows, d = x.shape
    tm = min(2 * ROW_TILE, rows)
    row = pl.BlockSpec((tm, d), lambda i: (i, 0))
    return pl.pallas_call(
        _final_norm_body, grid=(rows // tm,),
        in_specs=[row, _full((1, d))], out_specs=row,
        out_shape=jax.ShapeDtypeStruct((rows, d), F32),
        compiler_params=_params(1), name="final_norm",
    )(x, g)


def _trunk(x, p, past):
    b, l, d = x.shape
    depth = p['w_in'].shape[0]
    rows = b * l
    xr = x.reshape(rows, d)
    prompt = past is None
    new = {'k': [], 'v': [], 're': [], 'im': [], 'gv': []}
    for i in range(depth):
        g_mix = p['norm_mix_g'][i].reshape(1, d)
        gv = p['gmlp_v_norm_g'][i].reshape(1, GM_W)
        q, k, v, u, gu, gvn = _inproj(xr, g_mix, p['w_in_bf16'], gv, i,
                                      seq_major_u=(b, l) if prompt else None)
        sinks = p['attn_sinks'][i]
        sp = _ssm_layer_params(p, i)
        if prompt:
            o_attn = _attn_prompt(q, k, v, sinks, batch=b, seq=l)
            nw = min(WINDOW, l)
            k_win = k.reshape(b, l, N_KV_HEADS, HEAD_DIM)[:, l - nw:]
            v_win = v.reshape(b, l, N_KV_HEADS, HEAD_DIM)[:, l - nw:]
            h0 = jnp.zeros((b, 2 * SSM_FLAT), F32)
            o_ssm, s_fin = _ssm(u.reshape(l * b, SSM_W), h0, sp, r=b, tc=CHUNK)
            o_ssm = o_ssm.reshape(l, b * SSM_W)
        else:
            w = past[0].shape[2]
            o_attn, k_win, v_win = _attn_sample(
                q, k, v, past[0][i].reshape(b, w, KV_W), past[1][i].reshape(b, w, KV_W),
                sinks, batch=b, t=l)
            k_win = k_win.reshape(b, w, N_KV_HEADS, HEAD_DIM)
            v_win = v_win.reshape(b, w, N_KV_HEADS, HEAD_DIM)
            h0 = jnp.concatenate([past[2][i].reshape(b, SSM_FLAT),
                                  past[3][i].reshape(b, SSM_FLAT)], axis=1)
            u_sm = u.reshape(b, l, SSM_W).transpose(1, 0, 2).reshape(rows, SSM_W)
            o_ssm, s_fin = _ssm(u_sm, h0, sp, r=b, tc=l)
            o_ssm = o_ssm.reshape(l, b, SSM_W).transpose(1, 0, 2).reshape(rows, SSM_W)
            new['gv'].append(gvn.reshape(b, l, GM_HEADS, GM_HEAD_DIM))
        ws, bs = _gmlp_layer_params(p, i, min(l, CHUNK))
        xr = _outproj(xr, o_attn, o_ssm, gu, gvn, ws, bs,
                      p['mix_out_norm_g'][i].reshape(1, d), p['w_out_bf16'], i,
                      chunk=min(l, CHUNK), seq_major_ssm=(b, l) if prompt else None)
        g_ffn = p['norm_ffn_g'][i].reshape(1, d)
        j = i // 2
        if i % 2 == 0:
            xr = _ffn(xr, g_ffn, p['ffn_w_gate_bf16'], p['ffn_w_up_bf16'],
                      p['ffn_w_down_bf16'], j)
        else:
            wr = jnp.pad(p['moe_w_router'][j], ((0, 0), (0, ROUTER_LANES - N_EXPERTS)))
            br = jnp.pad(p['moe_b_router'][j], (0, ROUTER_LANES - N_EXPERTS),
                         constant_values=NEG_BIG).reshape(1, ROUTER_LANES)
            xr = _moe(xr, g_ffn, wr, br, p['moe_w_gate_bf16'], p['moe_w_up_bf16'],
                      p['moe_w_down_bf16'], j)
        new['k'].append(k_win)
        new['v'].append(v_win)
        new['re'].append(s_fin[:, :SSM_FLAT].reshape(b, SSM_GROUPS, SSM_STATE))
        new['im'].append(s_fin[:, SSM_FLAT:].reshape(b, SSM_GROUPS, SSM_STATE))
    y = _final_norm(xr, p['final_norm_g'].reshape(1, d)).reshape(b, l, d)
    return y, new


def kernel(x_prompt, x_sample, cache_k_win, cache_v_win, state_ssm_re, state_ssm_im,
           norm_mix_g, w_in, attn_sinks, ssm_a_re, ssm_a_im, ssm_log_dt, ssm_b_re, ssm_b_im,
           ssm_c_re, ssm_c_im, ssm_d, ssm_w_glu, ssm_b_glu, gmlp_v_norm_g, gmlp_w_s, gmlp_b_s,
           mix_out_norm_g, w_out, norm_ffn_g, ffn_w_gate, ffn_w_up, ffn_w_down,
           moe_w_router, moe_b_router, moe_w_gate, moe_w_up, moe_w_down, final_norm_g):
    params = {
        'norm_mix_g': norm_mix_g, 'w_in': w_in, 'attn_sinks': attn_sinks,
        'ssm_a_re': ssm_a_re, 'ssm_a_im': ssm_a_im, 'ssm_log_dt': ssm_log_dt,
        'ssm_b_re': ssm_b_re, 'ssm_b_im': ssm_b_im, 'ssm_c_re': ssm_c_re, 'ssm_c_im': ssm_c_im,
        'ssm_d': ssm_d, 'ssm_w_glu': ssm_w_glu, 'ssm_b_glu': ssm_b_glu,
        'gmlp_v_norm_g': gmlp_v_norm_g, 'gmlp_w_s': gmlp_w_s, 'gmlp_b_s': gmlp_b_s,
        'mix_out_norm_g': mix_out_norm_g, 'w_out': w_out, 'norm_ffn_g': norm_ffn_g,
        'ffn_w_gate': ffn_w_gate, 'ffn_w_up': ffn_w_up, 'ffn_w_down': ffn_w_down,
        'moe_w_router': moe_w_router, 'moe_b_router': moe_b_router,
        'moe_w_gate': moe_w_gate, 'moe_w_up': moe_w_up, 'moe_w_down': moe_w_down,
        'final_norm_g': final_norm_g,
    }
    for name in ('w_in', 'w_out', 'ffn_w_gate', 'ffn_w_up', 'ffn_w_down',
                 'moe_w_gate', 'moe_w_up', 'moe_w_down'):
        params[name + '_bf16'] = params[name].astype(BF16)
    y_p, st_p = _trunk(x_prompt, params, None)
    y_s, st_s = _trunk(x_sample, params,
                       (cache_k_win, cache_v_win, state_ssm_re, state_ssm_im))
    return (y_p, y_s,
            jnp.stack(st_p['k']), jnp.stack(st_p['v']),
            jnp.stack(st_p['re']), jnp.stack(st_p['im']),
            jnp.stack(st_s['k']), jnp.stack(st_s['v']),
            jnp.stack(st_s['re']), jnp.stack(st_s['im']),
            jnp.stack(st_s['gv']))
```

```python
import functools
import math

import jax
import jax.numpy as jnp
from jax import lax
from jax.experimental import pallas as pl
from jax.experimental.pallas import tpu as pltpu

F32 = jnp.float32
BF16 = jnp.bfloat16

EPS = 1e-6
HEAD_DIM = 64
N_HEADS = 8
N_KV_HEADS = 2
GQA_GROUP = N_HEADS // N_KV_HEADS
ATTN_W = N_HEADS * HEAD_DIM
KV_W = N_KV_HEADS * HEAD_DIM
WINDOW = 128
ATTN_SCALE = 1.0 / math.sqrt(HEAD_DIM)
SSM_W = 256
SSM_GROUP_CH = 16
SSM_GROUPS = 16
SSM_STATE = 64
SSM_FLAT = SSM_GROUPS * SSM_STATE
GM_W = 256
GM_HEADS = 4
GM_HEAD_DIM = 64
CHUNK = 128
Q_END = ATTN_W
K_END = Q_END + KV_W
V_END = K_END + KV_W
S_END = V_END + SSM_W
IN_COLS = S_END + 2 * GM_W
N_EXPERTS = 8
ROUTER_LANES = 128
NEG_BIG = -1e30

VMEM_LIMIT_BYTES = 56 * 1024 * 1024
ROW_TILE = 512
ATTN_Q_BLOCKS = 4
MOE_ROW_TILE = 1024
MOE_CHUNK = 128
MOE_SCATTER_ROWS = 256
SCAN_LANES = 512


def _params(n_axes):
    return pltpu.CompilerParams(
        dimension_semantics=("arbitrary",) * n_axes,
        vmem_limit_bytes=VMEM_LIMIT_BYTES)


def _rms(x, g):
    return x * lax.rsqrt(jnp.mean(x * x, axis=-1, keepdims=True) + EPS) * g


def _bdot(a, b):
    return jnp.dot(a.astype(BF16), b.astype(BF16), preferred_element_type=F32)


def _full(shape):
    return pl.BlockSpec(shape, lambda *_: (0,) * len(shape))


def _group_mean(a, n_groups, width):
    lane = lax.broadcasted_iota(jnp.int32, (1, n_groups * width), 1)
    out = jnp.zeros_like(a)
    for h in range(n_groups):
        m = (lane >= h * width) & (lane < (h + 1) * width)
        s = jnp.sum(jnp.where(m, a, 0.0), axis=-1, keepdims=True) * (1.0 / width)
        out = jnp.where(m, s, out)
    return out


def _inproj_body(x_ref, g_ref, w_ref, gv_ref, q_ref, k_ref, v_ref, u_ref, gu_ref, gvn_ref):
    h = _rms(x_ref[...], g_ref[...])
    z = jnp.dot(h.astype(BF16), w_ref[...], preferred_element_type=F32)
    q_ref[...] = (z[:, :Q_END] * ATTN_SCALE).astype(BF16)
    k_ref[...] = z[:, Q_END:K_END]
    v_ref[...] = z[:, K_END:V_END]
    u_ref[...] = z[:, V_END:S_END]
    g = jax.nn.gelu(z[:, S_END:])
    gu_ref[...] = g[:, :GM_W]
    vv = g[:, GM_W:]
    mu = _group_mean(vv, GM_HEADS, GM_HEAD_DIM)
    var = _group_mean(jnp.square(vv - mu), GM_HEADS, GM_HEAD_DIM)
    gvn_ref[...] = (vv - mu) * lax.rsqrt(var + EPS) * gv_ref[...]


def _layer(shape, j):
    return pl.BlockSpec((None,) + tuple(shape), lambda *_: (j,) + (0,) * len(shape))


def _inproj(x, g, w_bf16, gv, j):
    rows, d = x.shape
    tm = min(ROW_TILE, rows)
    nt = rows // tm
    row = lambda w: pl.BlockSpec((tm, w), lambda i: (i, 0))
    return pl.pallas_call(
        _inproj_body,
        grid=(nt,),
        in_specs=[row(d), _full((1, d)), _layer((d, IN_COLS), j), _full((1, GM_W))],
        out_specs=[row(ATTN_W), row(KV_W), row(KV_W), row(SSM_W), row(GM_W), row(GM_W)],
        out_shape=[jax.ShapeDtypeStruct((rows, ATTN_W), BF16),
                   jax.ShapeDtypeStruct((rows, KV_W), F32),
                   jax.ShapeDtypeStruct((rows, KV_W), F32),
                   jax.ShapeDtypeStruct((rows, SSM_W), F32),
                   jax.ShapeDtypeStruct((rows, GM_W), F32),
                   jax.ShapeDtypeStruct((rows, GM_W), F32)],
        compiler_params=_params(1),
        name="inproj",
    )(x, g, w_bf16, gv)


def _sink_softmax(s, mask, sink):
    s = jnp.where(mask, s, -jnp.inf)
    m = jnp.maximum(jnp.max(s, axis=-1, keepdims=True), sink)
    e = jnp.exp(s - m)
    return e / (jnp.sum(e, axis=-1, keepdims=True) + jnp.exp(sink - m))


def _head_pair_operands(x):
    low = lax.broadcasted_iota(jnp.int32, (1, KV_W), 1) < HEAD_DIM
    swapped = pltpu.roll(x, HEAD_DIM, axis=1)
    keep = lambda a, in_low: jnp.where(low == in_low, a, 0.0).astype(BF16)
    return ((keep(x, True), keep(swapped, False)),
            (keep(swapped, True), keep(x, False)))


def _attn_prompt_body(sink_ref, q_ref, kc_ref, kp_ref, vc_ref, vp_ref, o_ref, *, nq):
    n = pl.program_id(1)
    qi = lax.broadcasted_iota(jnp.int32, (WINDOW, 2 * WINDOW), 0)
    si = lax.broadcasted_iota(jnp.int32, (WINDOW, 2 * WINDOW), 1)
    dist = WINDOW + qi - si
    band = (dist >= 0) & (dist < WINDOW)
    first = band & ((n > 0) | (si >= WINDOW))
    k_ops = _head_pair_operands(jnp.concatenate([kp_ref[...], kc_ref[...]], axis=0))
    v_ops = _head_pair_operands(jnp.concatenate([vp_ref[...], vc_ref[...]], axis=0))
    pair_w = 2 * HEAD_DIM
    for j in range(nq):
        mask = first if j == 0 else band
        q_rows = slice(j * WINDOW, (j + 1) * WINDOW)
        kv_rows = slice(j * WINDOW, (j + 2) * WINDOW)
        for hp in range(N_HEADS // 2):
            kh = (2 * hp) // GQA_GROUP
            qp = q_ref[q_rows, hp * pair_w:(hp + 1) * pair_w]
            acc = None
            for par in range(2):
                sink = sink_ref[2 * hp + par]
                s = lax.dot_general(qp, k_ops[kh][par][kv_rows], (((1,), (1,)), ((), ())),
                                    preferred_element_type=F32)
                s = jnp.where(mask, s, -jnp.inf)
                m = jnp.maximum(jnp.max(s, axis=-1, keepdims=True), sink)
                e = jnp.exp(s - m)
                den = jnp.sum(e, axis=-1, keepdims=True) + jnp.exp(sink - m)
                pv = jnp.dot(e.astype(BF16), v_ops[kh][par][kv_rows],
                             preferred_element_type=F32)
                pv = pv * (1.0 / den)
                acc = pv if acc is None else acc + pv
            o_ref[q_rows, hp * pair_w:(hp + 1) * pair_w] = acc


def _attn_prompt(q, k, v, sinks, *, batch, seq):
    nq = ATTN_Q_BLOCKS
    nb = seq // WINDOW
    steps = nb // nq
    cur = lambda w: pl.BlockSpec((nq * WINDOW, w), lambda b, n: (b * steps + n, 0))
    prev = lambda w: pl.BlockSpec(
        (WINDOW, w), lambda b, n: (b * nb + jnp.maximum(n * nq - 1, 0), 0))
    return pl.pallas_call(
        functools.partial(_attn_prompt_body, nq=nq),
        grid=(batch, steps),
        in_specs=[pl.BlockSpec(memory_space=pltpu.SMEM),
                  cur(ATTN_W), cur(KV_W), prev(KV_W), cur(KV_W), prev(KV_W)],
        out_specs=cur(ATTN_W),
        out_shape=jax.ShapeDtypeStruct((batch * seq, ATTN_W), F32),
        compiler_params=_params(2),
        name="attn_prompt",
    )(sinks, q, k, k, v, v)


def _attn_sample_body(sink_ref, q_ref, k_ref, v_ref, kb_ref, vb_ref, o_ref, ko_ref, vo_ref,
                      *, bb, t):
    w = kb_ref.shape[1]
    k_new = k_ref[...].reshape(bb, t, KV_W)
    v_new = v_ref[...].reshape(bb, t, KV_W)
    ko_ref[...] = jnp.concatenate([kb_ref[:, t:, :], k_new], axis=1)
    vo_ref[...] = jnp.concatenate([vb_ref[:, t:, :], v_new], axis=1)
    pad = jnp.zeros((bb, w - t, KV_W), F32)
    kk = jnp.concatenate([kb_ref[...], k_new, pad], axis=1)
    vv = jnp.concatenate([vb_ref[...], v_new, pad], axis=1)
    q = q_ref[...].astype(F32).reshape(bb, t, ATTN_W)
    rows = GQA_GROUP * t
    ti = lax.broadcasted_iota(jnp.int32, (rows, 2 * w), 0) % t
    si = lax.broadcasted_iota(jnp.int32, (rows, 2 * w), 1)
    dist = w + ti - si
    mask = ((dist >= 0) & (dist < WINDOW))[None]
    gi = lax.broadcasted_iota(jnp.int32, (rows, 1), 0) // t
    for kh in range(N_KV_HEADS):
        kh_sl = slice(kh * HEAD_DIM, (kh + 1) * HEAD_DIM)
        k_h = kk[:, :, kh_sl].astype(BF16)
        v_h = vv[:, :, kh_sl].astype(BF16)
        q_g = jnp.concatenate(
            [q[:, :, (kh * GQA_GROUP + g) * HEAD_DIM:(kh * GQA_GROUP + g + 1) * HEAD_DIM]
             for g in range(GQA_GROUP)], axis=1)
        sink = jnp.zeros((rows, 1), F32)
        for g in range(GQA_GROUP):
            sink = jnp.where(gi == g, sink_ref[kh * GQA_GROUP + g], sink)
        s = jnp.einsum('bqd,bkd->bqk', q_g.astype(BF16), k_h, preferred_element_type=F32)
        p = _sink_softmax(s, mask, sink[None])
        o = jnp.einsum('bqk,bkd->bqd', p.astype(BF16), v_h, preferred_element_type=F32)
        for g in range(GQA_GROUP):
            h = kh * GQA_GROUP + g
            o_ref[:, h * HEAD_DIM:(h + 1) * HEAD_DIM] = (
                o[:, g * t:(g + 1) * t].reshape(bb * t, HEAD_DIM))


def _attn_sample(q, k, v, k_buf, v_buf, sinks, *, batch, t):
    w = k_buf.shape[1]
    bb = 16
    row = lambda c: pl.BlockSpec((bb * t, c), lambda i: (i, 0))
    buf = pl.BlockSpec((bb, w, KV_W), lambda i: (i, 0, 0))
    return pl.pallas_call(
        functools.partial(_attn_sample_body, bb=bb, t=t),
        grid=(batch // bb,),
        in_specs=[pl.BlockSpec(memory_space=pltpu.SMEM),
                  row(ATTN_W), row(KV_W), row(KV_W), buf, buf],
        out_specs=[row(ATTN_W), buf, buf],
        out_shape=[jax.ShapeDtypeStruct((batch * t, ATTN_W), F32),
                   jax.ShapeDtypeStruct((batch, w, KV_W), F32),
                   jax.ShapeDtypeStruct((batch, w, KV_W), F32)],
        compiler_params=_params(1),
        name="attn_sample",
    )(sinks, q, k, v, k_buf, v_buf)


def _ssm_body(u_ref, h0_ref, are_ref, aim_ref, ldt_ref, bre_ref, bim_ref, cre_ref, cim_ref,
              d_ref, wglu_ref, bglu_ref, o_ref, s_ref, abar_scr, bbar_scr, bu_scr, h_scr,
              *, r, tc):
    c = pl.program_id(0)

    @pl.when(c == 0)
    def _():
        ar = are_ref[...]
        ai = aim_ref[...]
        dt = jnp.exp(ldt_ref[...])
        decay = jnp.exp(dt * ar)
        abr = decay * jnp.cos(dt * ai)
        abi = decay * jnp.sin(dt * ai)
        den = ar * ar + ai * ai
        nr = abr - 1.0
        fr = (nr * ar + abi * ai) / den
        fi = (abi * ar - nr * ai) / den
        abar_scr[0:1, :] = abr
        abar_scr[1:2, :] = abi
        br = bre_ref[...]
        bi = bim_ref[...]
        bbar_scr[:, :SSM_FLAT] = (fr * br - fi * bi).astype(BF16)
        bbar_scr[:, SSM_FLAT:] = (fr * bi + fi * br).astype(BF16)
        h_scr[...] = h0_ref[...]

    u = jnp.swapaxes(u_ref[...], 0, 1).reshape(tc * r, SSM_W)
    bu_scr[...] = jnp.dot(u.astype(BF16), bbar_scr[...], preferred_element_type=F32)

    for rg in range(r // 8):
        for lc in range(SSM_FLAT // SCAN_LANES):
            re_sl = slice(lc * SCAN_LANES, (lc + 1) * SCAN_LANES)
            im_sl = slice(SSM_FLAT + lc * SCAN_LANES, SSM_FLAT + (lc + 1) * SCAN_LANES)
            rg_sl = slice(rg * 8, (rg + 1) * 8)
            ar = jnp.broadcast_to(abar_scr[0:1, re_sl], (8, SCAN_LANES))
            ai = jnp.broadcast_to(abar_scr[1:2, re_sl], (8, SCAN_LANES))

            def step(t, carry, re_sl=re_sl, im_sl=im_sl, rg=rg, ar=ar, ai=ai):
                hr, hi = carry
                rows = pl.ds(pl.multiple_of(t * r + rg * 8, 8), 8)
                nr = ar * hr - ai * hi + bu_scr[rows, re_sl]
                ni = ar * hi + ai * hr + bu_scr[rows, im_sl]
                bu_scr[rows, re_sl] = nr
                bu_scr[rows, im_sl] = ni
                return nr, ni

            hr, hi = lax.fori_loop(0, tc, step, (h_scr[rg_sl, re_sl], h_scr[rg_sl, im_sl]),
                                   unroll=4)
            h_scr[rg_sl, re_sl] = hr
            h_scr[rg_sl, im_sl] = hi

    y = (jnp.dot(bu_scr[:, :SSM_FLAT].astype(BF16), cre_ref[...], preferred_element_type=F32)
         - jnp.dot(bu_scr[:, SSM_FLAT:].astype(BF16), cim_ref[...], preferred_element_type=F32))
    y = jax.nn.gelu(y + d_ref[...] * u)
    gl = jnp.dot(y.astype(BF16), wglu_ref[...], preferred_element_type=F32) + bglu_ref[...]
    o_ref[...] = jnp.swapaxes((y * jax.nn.sigmoid(gl)).reshape(tc, r, SSM_W), 0, 1)

    @pl.when(c == pl.num_programs(0) - 1)
    def _():
        s_ref[...] = h_scr[...]


def _ssm(u, h0, sp, *, tc):
    r, steps, _ = u.shape
    blk = tc * r
    seq_blk = pl.BlockSpec((r, tc, SSM_W), lambda c: (0, c, 0))
    return pl.pallas_call(
        functools.partial(_ssm_body, r=r, tc=tc),
        grid=(steps // tc,),
        in_specs=[seq_blk,
                  _full((r, 2 * SSM_FLAT)),
                  _full((1, SSM_FLAT)), _full((1, SSM_FLAT)), _full((1, SSM_FLAT)),
                  _full((SSM_W, SSM_FLAT)), _full((SSM_W, SSM_FLAT)),
                  _full((SSM_FLAT, SSM_W)), _full((SSM_FLAT, SSM_W)),
                  _full((1, SSM_W)), _full((SSM_W, SSM_W)), _full((1, SSM_W))],
        out_specs=[seq_blk, _full((r, 2 * SSM_FLAT))],
        out_shape=[jax.ShapeDtypeStruct((r, steps, SSM_W), F32),
                   jax.ShapeDtypeStruct((r, 2 * SSM_FLAT), F32)],
        scratch_shapes=[pltpu.VMEM((2, SSM_FLAT), F32),
                        pltpu.VMEM((SSM_W, 2 * SSM_FLAT), BF16),
                        pltpu.VMEM((blk, 2 * SSM_FLAT), F32),
                        pltpu.VMEM((r, 2 * SSM_FLAT), F32)],
        compiler_params=_params(1),
        name="ssm",
    )(u, h0, sp['a_re'], sp['a_im'], sp['log_dt'], sp['b_re'], sp['b_im'],
      sp['c_re'], sp['c_im'], sp['d'], sp['w_glu'], sp['b_glu'])


def _ssm_layer_params(p, i):
    eye = jnp.eye(SSM_GROUPS, dtype=F32)

    def b_blockdiag(b):
        return jnp.einsum('gpc,gh->gchp', b, eye).reshape(SSM_W, SSM_FLAT)

    def c_blockdiag(c):
        return jnp.einsum('gcp,gh->gphc', c, eye).reshape(SSM_FLAT, SSM_W)

    return {
        'a_re': p['ssm_a_re'][i].reshape(1, SSM_FLAT),
        'a_im': p['ssm_a_im'][i].reshape(1, SSM_FLAT),
        'log_dt': jnp.repeat(p['ssm_log_dt'][i], SSM_STATE).reshape(1, SSM_FLAT),
        'b_re': b_blockdiag(p['ssm_b_re'][i]),
        'b_im': b_blockdiag(p['ssm_b_im'][i]),
        'c_re': c_blockdiag(p['ssm_c_re'][i]).astype(BF16),
        'c_im': c_blockdiag(p['ssm_c_im'][i]).astype(BF16),
        'd': p['ssm_d'][i].reshape(1, SSM_W),
        'w_glu': p['ssm_w_glu'][i].astype(BF16),
        'b_glu': p['ssm_b_glu'][i].reshape(1, SSM_W),
    }


def _outproj_body(x_ref, oa_ref, os_ref, gu_ref, gvn_ref, ws_ref, bs_ref, go_ref, w_ref, o_ref,
                  *, chunk):
    tm = x_ref.shape[0]
    ri = lax.broadcasted_iota(jnp.int32, (CHUNK, CHUNK), 0)
    ci = lax.broadcasted_iota(jnp.int32, (CHUNK, CHUNK), 1)
    causal = (ri // chunk == ci // chunk) & (ci <= ri)
    lane = lax.broadcasted_iota(jnp.int32, (1, GM_W), 1)
    zs = []
    for cblk in range(tm // CHUNK):
        vn = gvn_ref[cblk * CHUNK:(cblk + 1) * CHUNK, :].astype(BF16)
        z = bs_ref[...]
        for h in range(GM_HEADS):
            w_h = jnp.where(causal, ws_ref[h], 0.0).astype(BF16)
            z_h = jnp.dot(w_h, vn, preferred_element_type=F32)
            head = (lane >= h * GM_HEAD_DIM) & (lane < (h + 1) * GM_HEAD_DIM)
            z = z + jnp.where(head, z_h, 0.0)
        zs.append(z)
    o_gm = gu_ref[...] * jnp.concatenate(zs, axis=0)
    go = go_ref[...]
    o = jnp.concatenate([
        _rms(oa_ref[...], go[:, :ATTN_W]),
        _rms(os_ref[...], go[:, ATTN_W:ATTN_W + SSM_W]),
        _rms(o_gm, go[:, ATTN_W + SSM_W:])], axis=-1)
    o_ref[...] = x_ref[...] + jnp.dot(o.astype(BF16), w_ref[...], preferred_element_type=F32)


def _outproj(x, o_attn, o_ssm, gu, gvn, ws, bs, g_out, w_bf16, j, *, chunk):
    rows, d = x.shape
    tm = min(ROW_TILE, rows)
    row = lambda w: pl.BlockSpec((tm, w), lambda i: (i, 0))
    return pl.pallas_call(
        functools.partial(_outproj_body, chunk=chunk),
        grid=(rows // tm,),
        in_specs=[row(d), row(ATTN_W), row(SSM_W), row(GM_W), row(GM_W),
                  _full((GM_HEADS, CHUNK, CHUNK)), _full((CHUNK, GM_W)),
                  _full((1, d)), _layer((d, d), j)],
        out_specs=row(d),
        out_shape=jax.ShapeDtypeStruct((rows, d), F32),
        compiler_params=_params(1),
        name="outproj",
    )(x, o_attn, o_ssm, gu, gvn, ws, bs, g_out, w_bf16)


def _gmlp_layer_params(p, i, chunk):
    rep = CHUNK // chunk
    ws = jnp.tile(p['gmlp_w_s'][i][:, :chunk, :chunk], (1, rep, rep))
    bs = jnp.tile(p['gmlp_b_s'][i][:, :chunk], (1, rep))
    bs = jnp.repeat(bs.T, GM_HEAD_DIM, axis=1)
    return ws, bs


def _swiglu_cols(h, wg_ref, wu_ref, wd_ref, n_chunks):
    ff = wg_ref.shape[-1]
    fc = ff // n_chunks
    acc = None
    for j in range(n_chunks):
        sl = slice(j * fc, (j + 1) * fc)
        a = jnp.dot(h, wg_ref[:, sl], preferred_element_type=F32)
        b = jnp.dot(h, wu_ref[:, sl], preferred_element_type=F32)
        m = (jax.nn.silu(a) * b).astype(BF16)
        y = jnp.dot(m, wd_ref[sl, :], preferred_element_type=F32)
        acc = y if acc is None else acc + y
    return acc


def _ffn_body(x_ref, g_ref, wg_ref, wu_ref, wd_ref, o_ref):
    x = x_ref[...]
    h = _rms(x, g_ref[...]).astype(BF16)
    o_ref[...] = x + _swiglu_cols(h, wg_ref, wu_ref, wd_ref, 2)


def _ffn(x, g, wg, wu, wd, j):
    rows, d = x.shape
    ff = wg.shape[-1]
    tm = min(ROW_TILE, rows)
    row = pl.BlockSpec((tm, d), lambda i: (i, 0))
    once = lambda shape: pl.BlockSpec((None,) + shape, lambda i: (j, 0, 0),
                                      pipeline_mode=pl.Buffered(1))
    return pl.pallas_call(
        _ffn_body,
        grid=(rows // tm,),
        in_specs=[row, _full((1, d)), once((d, ff)), once((d, ff)), once((ff, d))],
        out_specs=row,
        out_shape=jax.ShapeDtypeStruct((rows, d), F32),
        compiler_params=_params(1),
        name="ffn",
    )(x, g, wg, wu, wd)


def _split_bf16(x):
    hi = x.astype(BF16)
    lo = (x - hi.astype(F32)).astype(BF16)
    return hi, lo


def _router_comb(hf, wr_ref, br_ref):
    h_hi, h_lo = _split_bf16(hf)
    w_hi, w_lo = _split_bf16(wr_ref[...])
    dot = lambda a, b: jnp.dot(a, b, preferred_element_type=F32)
    logits = dot(h_hi, w_hi) + (dot(h_hi, w_lo) + dot(h_lo, w_hi)) + br_ref[...]
    lane = lax.broadcasted_iota(jnp.int32, logits.shape, 1).astype(F32)
    far = float(ROUTER_LANES)
    m1 = jnp.max(logits, axis=-1, keepdims=True)
    i1 = jnp.min(jnp.where(logits == m1, lane, far), axis=-1, keepdims=True)
    rest = jnp.where(lane == i1, -jnp.inf, logits)
    m2 = jnp.max(rest, axis=-1, keepdims=True)
    i2 = jnp.min(jnp.where(rest == m2, lane, far), axis=-1, keepdims=True)
    e2 = jnp.exp(m2 - m1)
    den = 1.0 + e2
    comb = jnp.where(lane == i1, 1.0 / den, 0.0) + jnp.where(lane == i2, e2 / den, 0.0)
    sel = jnp.where((lane == i1) | (lane == i2), 1.0, 0.0)
    return comb, sel


def _moe_body(x_ref, g_ref, wr_ref, br_ref, wg_ref, wu_ref, wd_ref, o_ref,
              h_scr, gate_scr, rank_scr, rank_t_scr, cnt_smem):
    e = pl.program_id(1)
    tm = x_ref.shape[0]

    @pl.when(e == 0)
    def _():
        x = x_ref[...]
        hf = _rms(x, g_ref[...])
        h_scr[...] = hf.astype(BF16)
        comb, sel = _router_comb(hf, wr_ref, br_ref)
        gate_scr[...] = comb
        ri = lax.broadcasted_iota(jnp.int32, (tm, tm), 0)
        ci = lax.broadcasted_iota(jnp.int32, (tm, tm), 1)
        lower = jnp.where(ci < ri, 1.0, 0.0).astype(BF16)
        rank = jnp.dot(lower, sel.astype(BF16), preferred_element_type=F32)
        rank = jnp.where(sel > 0.0, rank, -1.0)
        rank_scr[...] = rank
        rank_t_scr[...] = rank.T
        cnt = jnp.sum(sel, axis=0, keepdims=True)
        for k in range(N_EXPERTS):
            cnt_smem[k] = cnt[0, k].astype(jnp.int32)
        o_ref[...] = x

    lane = lax.broadcasted_iota(jnp.int32, (1, ROUTER_LANES), 1)
    pick = lane == e
    rank_col = jnp.sum(jnp.where(pick, rank_scr[...], 0.0), axis=-1, keepdims=True)
    gate_col = jnp.sum(jnp.where(pick, gate_scr[...], 0.0), axis=-1, keepdims=True)
    rank_row = rank_t_scr[pl.ds(e, 1), :]
    n_chunks = (cnt_smem[e] + (MOE_CHUNK - 1)) // MOE_CHUNK
    wg, wu, wd = wg_ref, wu_ref, wd_ref

    def chunk(c, carry):
        base = (c * MOE_CHUNK).astype(F32)
        slot_col = lax.broadcasted_iota(jnp.int32, (MOE_CHUNK, 1), 0).astype(F32) + base
        slot_row = lax.broadcasted_iota(jnp.int32, (1, MOE_CHUNK), 1).astype(F32) + base
        gather = jnp.where(rank_row == slot_col, 1.0, 0.0).astype(BF16)
        xe = jnp.dot(gather, h_scr[...], preferred_element_type=F32).astype(BF16)
        ye = _swiglu_cols(xe, wg, wu, wd, 1).astype(BF16)
        sr = min(MOE_SCATTER_ROWS, tm)
        for rb in range(tm // sr):
            sl = slice(rb * sr, (rb + 1) * sr)
            scatter = jnp.where(rank_col[sl] == slot_row, 1.0, 0.0).astype(BF16)
            o_ref[sl, :] += gate_col[sl] * jnp.dot(scatter, ye, preferred_element_type=F32)
        return carry

    lax.fori_loop(0, n_chunks, chunk, 0)


def _moe(x, g, wr_pad, br_pad, wg, wu, wd, j):
    rows, d = x.shape
    _, n_e, _, ff = wg.shape
    tm = min(MOE_ROW_TILE, rows)
    row = pl.BlockSpec((tm, d), lambda i, e: (i, 0))
    return pl.pallas_call(
        _moe_body,
        grid=(rows // tm, n_e),
        in_specs=[row, _full((1, d)), _full((d, ROUTER_LANES)), _full((1, ROUTER_LANES)),
                  pl.BlockSpec((None, None, d, ff), lambda i, e: (j, e, 0, 0)),
                  pl.BlockSpec((None, None, d, ff), lambda i, e: (j, e, 0, 0)),
                  pl.BlockSpec((None, None, ff, d), lambda i, e: (j, e, 0, 0))],
        out_specs=row,
        out_shape=jax.ShapeDtypeStruct((rows, d), F32),
        scratch_shapes=[pltpu.VMEM((tm, d), BF16),
                        pltpu.VMEM((tm, ROUTER_LANES), F32),
                        pltpu.VMEM((tm, ROUTER_LANES), F32),
                        pltpu.VMEM((ROUTER_LANES, tm), F32),
                        pltpu.SMEM((N_EXPERTS,), jnp.int32)],
        compiler_params=_params(2),
        name="moe",
    )(x, g, wr_pad, br_pad, wg, wu, wd)


def _final_norm_body(x_ref, g_ref, o_ref):
    o_ref[...] = _rms(x_ref[...], g_ref[...])


def _final_norm(x, g):
    rows, d = x.shape
    tm = min(2 * ROW_TILE, rows)
    row = pl.BlockSpec((tm, d), lambda i: (i, 0))
    return pl.pallas_call(
        _final_norm_body, grid=(rows // tm,),
        in_specs=[row, _full((1, d))], out_specs=row,
        out_shape=jax.ShapeDtypeStruct((rows, d), F32),
        compiler_params=_params(1), name="final_norm",
    )(x, g)


def _trunk(x, p, past):
    b, l, d = x.shape
    depth = p['w_in'].shape[0]
    rows = b * l
    xr = x.reshape(rows, d)
    prompt = past is None
    new = {'k': [], 'v': [], 're': [], 'im': [], 'gv': []}
    for i in range(depth):
        g_mix = p['norm_mix_g'][i].reshape(1, d)
        gv = p['gmlp_v_norm_g'][i].reshape(1, GM_W)
        q, k, v, u, gu, gvn = _inproj(xr, g_mix, p['w_in_bf16'], gv, i)
        u = u.reshape(b, l, SSM_W)
        sinks = p['attn_sinks'][i]
        sp = _ssm_layer_params(p, i)
        if prompt:
            o_attn = _attn_prompt(q, k, v, sinks, batch=b, seq=l)
            nw = min(WINDOW, l)
            k_win = k.reshape(b, l, N_KV_HEADS, HEAD_DIM)[:, l - nw:]
            v_win = v.reshape(b, l, N_KV_HEADS, HEAD_DIM)[:, l - nw:]
            h0 = jnp.zeros((b, 2 * SSM_FLAT), F32)
            o_ssm, s_fin = _ssm(u, h0, sp, tc=CHUNK)
        else:
            w = past[0].shape[2]
            o_attn, k_win, v_win = _attn_sample(
                q, k, v, past[0][i].reshape(b, w, KV_W), past[1][i].reshape(b, w, KV_W),
                sinks, batch=b, t=l)
            k_win = k_win.reshape(b, w, N_KV_HEADS, HEAD_DIM)
            v_win = v_win.reshape(b, w, N_KV_HEADS, HEAD_DIM)
            h0 = jnp.concatenate([past[2][i].reshape(b, SSM_FLAT),
                                  past[3][i].reshape(b, SSM_FLAT)], axis=1)
            o_ssm, s_fin = _ssm(u, h0, sp, tc=l)
            new['gv'].append(gvn.reshape(b, l, GM_HEADS, GM_HEAD_DIM))
        ws, bs = _gmlp_layer_params(p, i, min(l, CHUNK))
        xr = _outproj(xr, o_attn, o_ssm.reshape(rows, SSM_W), gu, gvn, ws, bs,
                      p['mix_out_norm_g'][i].reshape(1, d), p['w_out_bf16'], i,
                      chunk=min(l, CHUNK))
        g_ffn = p['norm_ffn_g'][i].reshape(1, d)
        j = i // 2
        if i % 2 == 0:
            xr = _ffn(xr, g_ffn, p['ffn_w_gate_bf16'], p['ffn_w_up_bf16'],
                      p['ffn_w_down_bf16'], j)
        else:
            wr = jnp.pad(p['moe_w_router'][j], ((0, 0), (0, ROUTER_LANES - N_EXPERTS)))
            br = jnp.pad(p['moe_b_router'][j], (0, ROUTER_LANES - N_EXPERTS),
                         constant_values=NEG_BIG).reshape(1, ROUTER_LANES)
            xr = _moe(xr, g_ffn, wr, br, p['moe_w_gate_bf16'], p['moe_w_up_bf16'],
                      p['moe_w_down_bf16'], j)
        new['k'].append(k_win)
        new['v'].append(v_win)
        new['re'].append(s_fin[:, :SSM_FLAT].reshape(b, SSM_GROUPS, SSM_STATE))
        new['im'].append(s_fin[:, SSM_FLAT:].reshape(b, SSM_GROUPS, SSM_STATE))
    y = _final_norm(xr, p['final_norm_g'].reshape(1, d)).reshape(b, l, d)
    return y, new


def kernel(x_prompt, x_sample, cache_k_win, cache_v_win, state_ssm_re, state_ssm_im,
           norm_mix_g, w_in, attn_sinks, ssm_a_re, ssm_a_im, ssm_log_dt, ssm_b_re, ssm_b_im,
           ssm_c_re, ssm_c_im, ssm_d, ssm_w_glu, ssm_b_glu, gmlp_v_norm_g, gmlp_w_s, gmlp_b_s,
           mix_out_norm_g, w_out, norm_ffn_g, ffn_w_gate, ffn_w_up, ffn_w_down,
           moe_w_router, moe_b_router, moe_w_gate, moe_w_up, moe_w_down, final_norm_g):
    params = {
        'norm_mix_g': norm_mix_g, 'w_in': w_in, 'attn_sinks': attn_sinks,
        'ssm_a_re': ssm_a_re, 'ssm_a_im': ssm_a_im, 'ssm_log_dt': ssm_log_dt,
        'ssm_b_re': ssm_b_re, 'ssm_b_im': ssm_b_im, 'ssm_c_re': ssm_c_re, 'ssm_c_im': ssm_c_im,
        'ssm_d': ssm_d, 'ssm_w_glu': ssm_w_glu, 'ssm_b_glu': ssm_b_glu,
        'gmlp_v_norm_g': gmlp_v_norm_g, 'gmlp_w_s': gmlp_w_s, 'gmlp_b_s': gmlp_b_s,
        'mix_out_norm_g': mix_out_norm_g, 'w_out': w_out, 'norm_ffn_g': norm_ffn_g,
        'ffn_w_gate': ffn_w_gate, 'ffn_w_up': ffn_w_up, 'ffn_w_down': ffn_w_down,
        'moe_w_router': moe_w_router, 'moe_b_router': moe_b_router,
        'moe_w_gate': moe_w_gate, 'moe_w_up': moe_w_up, 'moe_w_down': moe_w_down,
        'final_norm_g': final_norm_g,
    }
    for name in ('w_in', 'w_out', 'ffn_w_gate', 'ffn_w_up', 'ffn_w_down',
                 'moe_w_gate', 'moe_w_up', 'moe_w_down'):
        params[name + '_bf16'] = params[name].astype(BF16)
    y_p, st_p = _trunk(x_prompt, params, None)
    y_s, st_s = _trunk(x_sample, params,
                       (cache_k_win, cache_v_win, state_ssm_re, state_ssm_im))
    return (y_p, y_s,
            jnp.stack(st_p['k']), jnp.stack(st_p['v']),
            jnp.stack(st_p['re']), jnp.stack(st_p['im']),
            jnp.stack(st_s['k']), jnp.stack(st_s['v']),
            jnp.stack(st_s['re']), jnp.stack(st_s['im']),
            jnp.stack(st_s['gv']))
```

```python
import functools
import math

import jax
import jax.numpy as jnp
from jax import lax
from jax.experimental import pallas as pl
from jax.experimental.pallas import tpu as pltpu

F32 = jnp.float32
BF16 = jnp.bfloat16

EPS = 1e-6
HEAD_DIM = 64
N_HEADS = 8
N_KV_HEADS = 2
GQA_GROUP = N_HEADS // N_KV_HEADS
ATTN_W = N_HEADS * HEAD_DIM
KV_W = N_KV_HEADS * HEAD_DIM
WINDOW = 128
ATTN_SCALE = 1.0 / math.sqrt(HEAD_DIM)
LOG2E = math.log2(math.e)
SSM_W = 256
SSM_GROUP_CH = 16
SSM_GROUPS = 16
SSM_STATE = 64
SSM_FLAT = SSM_GROUPS * SSM_STATE
GM_W = 256
GM_HEADS = 4
GM_HEAD_DIM = 64
CHUNK = 128
Q_END = ATTN_W
K_END = Q_END + KV_W
V_END = K_END + KV_W
S_END = V_END + SSM_W
IN_COLS = S_END + 2 * GM_W
N_EXPERTS = 8
ROUTER_LANES = 128
NEG_BIG = -1e30

VMEM_LIMIT_BYTES = 56 * 1024 * 1024
ROW_TILE = 512
ATTN_Q_BLOCKS = 4
MOE_ROW_TILE = 1024
MOE_CHUNK = 128
MOE_SCATTER_ROWS = 256
SCAN_LANES = 512


def _params(n_axes):
    return pltpu.CompilerParams(
        dimension_semantics=("arbitrary",) * n_axes,
        vmem_limit_bytes=VMEM_LIMIT_BYTES)


def _rms(x, g):
    return x * lax.rsqrt(jnp.mean(x * x, axis=-1, keepdims=True) + EPS) * g


def _bdot(a, b):
    return jnp.dot(a.astype(BF16), b.astype(BF16), preferred_element_type=F32)


def _full(shape):
    return pl.BlockSpec(shape, lambda *_: (0,) * len(shape))


def _group_mean(a, n_groups, width):
    lane = lax.broadcasted_iota(jnp.int32, (1, n_groups * width), 1)
    out = jnp.zeros_like(a)
    for h in range(n_groups):
        m = (lane >= h * width) & (lane < (h + 1) * width)
        s = jnp.sum(jnp.where(m, a, 0.0), axis=-1, keepdims=True) * (1.0 / width)
        out = jnp.where(m, s, out)
    return out


def _inproj_body(x_ref, g_ref, w_ref, gv_ref, q_ref, k_ref, v_ref, u_ref, gu_ref, gvn_ref):
    h = _rms(x_ref[...], g_ref[...])
    z = jnp.dot(h.astype(BF16), w_ref[...], preferred_element_type=F32)
    q_ref[...] = (z[:, :Q_END] * (ATTN_SCALE * LOG2E)).astype(BF16)
    k_ref[...] = z[:, Q_END:K_END]
    v_ref[...] = z[:, K_END:V_END]
    u_ref[...] = z[:, V_END:S_END]
    g = jax.nn.gelu(z[:, S_END:])
    gu_ref[...] = g[:, :GM_W]
    vv = g[:, GM_W:]
    mu = _group_mean(vv, GM_HEADS, GM_HEAD_DIM)
    var = _group_mean(jnp.square(vv - mu), GM_HEADS, GM_HEAD_DIM)
    gvn_ref[...] = (vv - mu) * lax.rsqrt(var + EPS) * gv_ref[...]


def _layer(shape, j):
    return pl.BlockSpec((None,) + tuple(shape), lambda *_: (j,) + (0,) * len(shape))


def _inproj(x, g, w_bf16, gv, j):
    rows, d = x.shape
    tm = min(ROW_TILE, rows)
    nt = rows // tm
    row = lambda w: pl.BlockSpec((tm, w), lambda i: (i, 0))
    return pl.pallas_call(
        _inproj_body,
        grid=(nt,),
        in_specs=[row(d), _full((1, d)), _layer((d, IN_COLS), j), _full((1, GM_W))],
        out_specs=[row(ATTN_W), row(KV_W), row(KV_W), row(SSM_W), row(GM_W), row(GM_W)],
        out_shape=[jax.ShapeDtypeStruct((rows, ATTN_W), BF16),
                   jax.ShapeDtypeStruct((rows, KV_W), F32),
                   jax.ShapeDtypeStruct((rows, KV_W), F32),
                   jax.ShapeDtypeStruct((rows, SSM_W), F32),
                   jax.ShapeDtypeStruct((rows, GM_W), F32),
                   jax.ShapeDtypeStruct((rows, GM_W), F32)],
        compiler_params=_params(1),
        name="inproj",
    )(x, g, w_bf16, gv)


def _sink_softmax(s, mask, sink):
    s = jnp.where(mask, s, -jnp.inf)
    m = jnp.maximum(jnp.max(s, axis=-1, keepdims=True), sink)
    e = jnp.exp2(s - m)
    return e / (jnp.sum(e, axis=-1, keepdims=True) + jnp.exp2(sink - m))


def _head_pair_operands(x):
    low = lax.broadcasted_iota(jnp.int32, (1, KV_W), 1) < HEAD_DIM
    swapped = pltpu.roll(x, HEAD_DIM, axis=1)
    keep = lambda a, in_low: jnp.where(low == in_low, a, 0.0).astype(BF16)
    return ((keep(x, True), keep(swapped, False)),
            (keep(swapped, True), keep(x, False)))


def _attn_prompt_body(sink_ref, q_ref, kc_ref, kp_ref, vc_ref, vp_ref, o_ref, *, nq):
    n = pl.program_id(1)
    qi = lax.broadcasted_iota(jnp.int32, (WINDOW, 2 * WINDOW), 0)
    si = lax.broadcasted_iota(jnp.int32, (WINDOW, 2 * WINDOW), 1)
    dist = WINDOW + qi - si
    band = (dist >= 0) & (dist < WINDOW)
    first = band & ((n > 0) | (si >= WINDOW))
    k_ops = _head_pair_operands(jnp.concatenate([kp_ref[...], kc_ref[...]], axis=0))
    v_ops = _head_pair_operands(jnp.concatenate([vp_ref[...], vc_ref[...]], axis=0))
    pair_w = 2 * HEAD_DIM
    for j in range(nq):
        mask = first if j == 0 else band
        q_rows = slice(j * WINDOW, (j + 1) * WINDOW)
        kv_rows = slice(j * WINDOW, (j + 2) * WINDOW)
        for hp in range(N_HEADS // 2):
            kh = (2 * hp) // GQA_GROUP
            qp = q_ref[q_rows, hp * pair_w:(hp + 1) * pair_w]
            acc = None
            for par in range(2):
                sink = sink_ref[2 * hp + par] * LOG2E
                s = lax.dot_general(qp, k_ops[kh][par][kv_rows], (((1,), (1,)), ((), ())),
                                    preferred_element_type=F32)
                s = jnp.where(mask, s, -jnp.inf)
                m = jnp.maximum(jnp.max(s, axis=-1, keepdims=True), sink)
                e = jnp.exp2(s - m)
                den = jnp.sum(e, axis=-1, keepdims=True) + jnp.exp2(sink - m)
                pv = jnp.dot(e.astype(BF16), v_ops[kh][par][kv_rows],
                             preferred_element_type=F32)
                pv = pv * (1.0 / den)
                acc = pv if acc is None else acc + pv
            o_ref[q_rows, hp * pair_w:(hp + 1) * pair_w] = acc


def _attn_prompt(q, k, v, sinks, *, batch, seq):
    nq = ATTN_Q_BLOCKS
    nb = seq // WINDOW
    steps = nb // nq
    cur = lambda w: pl.BlockSpec((nq * WINDOW, w), lambda b, n: (b * steps + n, 0))
    prev = lambda w: pl.BlockSpec(
        (WINDOW, w), lambda b, n: (b * nb + jnp.maximum(n * nq - 1, 0), 0))
    return pl.pallas_call(
        functools.partial(_attn_prompt_body, nq=nq),
        grid=(batch, steps),
        in_specs=[pl.BlockSpec(memory_space=pltpu.SMEM),
                  cur(ATTN_W), cur(KV_W), prev(KV_W), cur(KV_W), prev(KV_W)],
        out_specs=cur(ATTN_W),
        out_shape=jax.ShapeDtypeStruct((batch * seq, ATTN_W), F32),
        compiler_params=_params(2),
        name="attn_prompt",
    )(sinks, q, k, k, v, v)


def _attn_sample_body(sink_ref, q_ref, k_ref, v_ref, kb_ref, vb_ref, o_ref, ko_ref, vo_ref,
                      *, bb, t):
    w = kb_ref.shape[1]
    k_new = k_ref[...].reshape(bb, t, KV_W)
    v_new = v_ref[...].reshape(bb, t, KV_W)
    ko_ref[...] = jnp.concatenate([kb_ref[:, t:, :], k_new], axis=1)
    vo_ref[...] = jnp.concatenate([vb_ref[:, t:, :], v_new], axis=1)
    pad = jnp.zeros((bb, w - t, KV_W), F32)
    kk = jnp.concatenate([kb_ref[...], k_new, pad], axis=1)
    vv = jnp.concatenate([vb_ref[...], v_new, pad], axis=1)
    q = q_ref[...].astype(F32).reshape(bb, t, ATTN_W)
    rows = GQA_GROUP * t
    ti = lax.broadcasted_iota(jnp.int32, (rows, 2 * w), 0) % t
    si = lax.broadcasted_iota(jnp.int32, (rows, 2 * w), 1)
    dist = w + ti - si
    mask = ((dist >= 0) & (dist < WINDOW))[None]
    gi = lax.broadcasted_iota(jnp.int32, (rows, 1), 0) // t
    for kh in range(N_KV_HEADS):
        kh_sl = slice(kh * HEAD_DIM, (kh + 1) * HEAD_DIM)
        k_h = kk[:, :, kh_sl].astype(BF16)
        v_h = vv[:, :, kh_sl].astype(BF16)
        q_g = jnp.concatenate(
            [q[:, :, (kh * GQA_GROUP + g) * HEAD_DIM:(kh * GQA_GROUP + g + 1) * HEAD_DIM]
             for g in range(GQA_GROUP)], axis=1)
        sink = jnp.zeros((rows, 1), F32)
        for g in range(GQA_GROUP):
            sink = jnp.where(gi == g, sink_ref[kh * GQA_GROUP + g] * LOG2E, sink)
        s = jnp.einsum('bqd,bkd->bqk', q_g.astype(BF16), k_h, preferred_element_type=F32)
        p = _sink_softmax(s, mask, sink[None])
        o = jnp.einsum('bqk,bkd->bqd', p.astype(BF16), v_h, preferred_element_type=F32)
        for g in range(GQA_GROUP):
            h = kh * GQA_GROUP + g
            o_ref[:, h * HEAD_DIM:(h + 1) * HEAD_DIM] = (
                o[:, g * t:(g + 1) * t].reshape(bb * t, HEAD_DIM))


def _attn_sample(q, k, v, k_buf, v_buf, sinks, *, batch, t):
    w = k_buf.shape[1]
    bb = 16
    row = lambda c: pl.BlockSpec((bb * t, c), lambda i: (i, 0))
    buf = pl.BlockSpec((bb, w, KV_W), lambda i: (i, 0, 0))
    return pl.pallas_call(
        functools.partial(_attn_sample_body, bb=bb, t=t),
        grid=(batch // bb,),
        in_specs=[pl.BlockSpec(memory_space=pltpu.SMEM),
                  row(ATTN_W), row(KV_W), row(KV_W), buf, buf],
        out_specs=[row(ATTN_W), buf, buf],
        out_shape=[jax.ShapeDtypeStruct((batch * t, ATTN_W), F32),
                   jax.ShapeDtypeStruct((batch, w, KV_W), F32),
                   jax.ShapeDtypeStruct((batch, w, KV_W), F32)],
        compiler_params=_params(1),
        name="attn_sample",
    )(sinks, q, k, v, k_buf, v_buf)


def _ssm_body(u_ref, h0_ref, are_ref, aim_ref, ldt_ref, bre_ref, bim_ref, cre_ref, cim_ref,
              d_ref, wglu_ref, bglu_ref, o_ref, s_ref, abar_scr, bbar_scr, bu_scr, h_scr,
              *, r, tc):
    c = pl.program_id(0)

    @pl.when(c == 0)
    def _():
        ar = are_ref[...]
        ai = aim_ref[...]
        dt = jnp.exp(ldt_ref[...])
        decay = jnp.exp(dt * ar)
        abr = decay * jnp.cos(dt * ai)
        abi = decay * jnp.sin(dt * ai)
        den = ar * ar + ai * ai
        nr = abr - 1.0
        fr = (nr * ar + abi * ai) / den
        fi = (abi * ar - nr * ai) / den
        abar_scr[0:1, :] = abr
        abar_scr[1:2, :] = abi
        br = bre_ref[...]
        bi = bim_ref[...]
        bbar_scr[:, :SSM_FLAT] = (fr * br - fi * bi).astype(BF16)
        bbar_scr[:, SSM_FLAT:] = (fr * bi + fi * br).astype(BF16)
        h_scr[...] = h0_ref[...]

    u = jnp.swapaxes(u_ref[...], 0, 1).reshape(tc * r, SSM_W)
    bu_scr[...] = jnp.dot(u.astype(BF16), bbar_scr[...], preferred_element_type=F32)

    for rg in range(r // 8):
        for lc in range(SSM_FLAT // SCAN_LANES):
            re_sl = slice(lc * SCAN_LANES, (lc + 1) * SCAN_LANES)
            im_sl = slice(SSM_FLAT + lc * SCAN_LANES, SSM_FLAT + (lc + 1) * SCAN_LANES)
            rg_sl = slice(rg * 8, (rg + 1) * 8)
            ar = jnp.broadcast_to(abar_scr[0:1, re_sl], (8, SCAN_LANES))
            ai = jnp.broadcast_to(abar_scr[1:2, re_sl], (8, SCAN_LANES))

            def step(t, carry, re_sl=re_sl, im_sl=im_sl, rg=rg, ar=ar, ai=ai):
                hr, hi = carry
                rows = pl.ds(pl.multiple_of(t * r + rg * 8, 8), 8)
                nr = ar * hr - ai * hi + bu_scr[rows, re_sl]
                ni = ar * hi + ai * hr + bu_scr[rows, im_sl]
                bu_scr[rows, re_sl] = nr
                bu_scr[rows, im_sl] = ni
                return nr, ni

            hr, hi = lax.fori_loop(0, tc, step, (h_scr[rg_sl, re_sl], h_scr[rg_sl, im_sl]),
                                   unroll=4)
            h_scr[rg_sl, re_sl] = hr
            h_scr[rg_sl, im_sl] = hi

    y = (jnp.dot(bu_scr[:, :SSM_FLAT].astype(BF16), cre_ref[...], preferred_element_type=F32)
         - jnp.dot(bu_scr[:, SSM_FLAT:].astype(BF16), cim_ref[...], preferred_element_type=F32))
    y = jax.nn.gelu(y + d_ref[...] * u)
    gl = jnp.dot(y.astype(BF16), wglu_ref[...], preferred_element_type=F32) + bglu_ref[...]
    o_ref[...] = jnp.swapaxes((y * jax.nn.sigmoid(gl)).reshape(tc, r, SSM_W), 0, 1)

    @pl.when(c == pl.num_programs(0) - 1)
    def _():
        s_ref[...] = h_scr[...]


def _ssm(u, h0, sp, *, tc):
    r, steps, _ = u.shape
    blk = tc * r
    seq_blk = pl.BlockSpec((r, tc, SSM_W), lambda c: (0, c, 0))
    return pl.pallas_call(
        functools.partial(_ssm_body, r=r, tc=tc),
        grid=(steps // tc,),
        in_specs=[seq_blk,
                  _full((r, 2 * SSM_FLAT)),
                  _full((1, SSM_FLAT)), _full((1, SSM_FLAT)), _full((1, SSM_FLAT)),
                  _full((SSM_W, SSM_FLAT)), _full((SSM_W, SSM_FLAT)),
                  _full((SSM_FLAT, SSM_W)), _full((SSM_FLAT, SSM_W)),
                  _full((1, SSM_W)), _full((SSM_W, SSM_W)), _full((1, SSM_W))],
        out_specs=[seq_blk, _full((r, 2 * SSM_FLAT))],
        out_shape=[jax.ShapeDtypeStruct((r, steps, SSM_W), F32),
                   jax.ShapeDtypeStruct((r, 2 * SSM_FLAT), F32)],
        scratch_shapes=[pltpu.VMEM((2, SSM_FLAT), F32),
                        pltpu.VMEM((SSM_W, 2 * SSM_FLAT), BF16),
                        pltpu.VMEM((blk, 2 * SSM_FLAT), F32),
                        pltpu.VMEM((r, 2 * SSM_FLAT), F32)],
        compiler_params=_params(1),
        name="ssm",
    )(u, h0, sp['a_re'], sp['a_im'], sp['log_dt'], sp['b_re'], sp['b_im'],
      sp['c_re'], sp['c_im'], sp['d'], sp['w_glu'], sp['b_glu'])


def _ssm_layer_params(p, i):
    eye = jnp.eye(SSM_GROUPS, dtype=F32)

    def b_blockdiag(b):
        return jnp.einsum('gpc,gh->gchp', b, eye).reshape(SSM_W, SSM_FLAT)

    def c_blockdiag(c):
        return jnp.einsum('gcp,gh->gphc', c, eye).reshape(SSM_FLAT, SSM_W)

    return {
        'a_re': p['ssm_a_re'][i].reshape(1, SSM_FLAT),
        'a_im': p['ssm_a_im'][i].reshape(1, SSM_FLAT),
        'log_dt': jnp.repeat(p['ssm_log_dt'][i], SSM_STATE).reshape(1, SSM_FLAT),
        'b_re': b_blockdiag(p['ssm_b_re'][i]),
        'b_im': b_blockdiag(p['ssm_b_im'][i]),
        'c_re': c_blockdiag(p['ssm_c_re'][i]).astype(BF16),
        'c_im': c_blockdiag(p['ssm_c_im'][i]).astype(BF16),
        'd': p['ssm_d'][i].reshape(1, SSM_W),
        'w_glu': p['ssm_w_glu'][i].astype(BF16),
        'b_glu': p['ssm_b_glu'][i].reshape(1, SSM_W),
    }


def _outproj_body(x_ref, oa_ref, os_ref, gu_ref, gvn_ref, ws_ref, bs_ref, go_ref, w_ref, o_ref,
                  *, chunk):
    tm = x_ref.shape[0]
    ri = lax.broadcasted_iota(jnp.int32, (CHUNK, CHUNK), 0)
    ci = lax.broadcasted_iota(jnp.int32, (CHUNK, CHUNK), 1)
    causal = (ri // chunk == ci // chunk) & (ci <= ri)
    lane = lax.broadcasted_iota(jnp.int32, (1, GM_W), 1)
    zs = []
    for cblk in range(tm // CHUNK):
        vn = gvn_ref[cblk * CHUNK:(cblk + 1) * CHUNK, :].astype(BF16)
        z = bs_ref[...]
        for h in range(GM_HEADS):
            w_h = jnp.where(causal, ws_ref[h], 0.0).astype(BF16)
            z_h = jnp.dot(w_h, vn, preferred_element_type=F32)
            head = (lane >= h * GM_HEAD_DIM) & (lane < (h + 1) * GM_HEAD_DIM)
            z = z + jnp.where(head, z_h, 0.0)
        zs.append(z)
    o_gm = gu_ref[...] * jnp.concatenate(zs, axis=0)
    go = go_ref[...]
    o = jnp.concatenate([
        _rms(oa_ref[...], go[:, :ATTN_W]),
        _rms(os_ref[...], go[:, ATTN_W:ATTN_W + SSM_W]),
        _rms(o_gm, go[:, ATTN_W + SSM_W:])], axis=-1)
    o_ref[...] = x_ref[...] + jnp.dot(o.astype(BF16), w_ref[...], preferred_element_type=F32)


def _outproj(x, o_attn, o_ssm, gu, gvn, ws, bs, g_out, w_bf16, j, *, chunk):
    rows, d = x.shape
    tm = min(ROW_TILE, rows)
    row = lambda w: pl.BlockSpec((tm, w), lambda i: (i, 0))
    return pl.pallas_call(
        functools.partial(_outproj_body, chunk=chunk),
        grid=(rows // tm,),
        in_specs=[row(d), row(ATTN_W), row(SSM_W), row(GM_W), row(GM_W),
                  _full((GM_HEADS, CHUNK, CHUNK)), _full((CHUNK, GM_W)),
                  _full((1, d)), _layer((d, d), j)],
        out_specs=row(d),
        out_shape=jax.ShapeDtypeStruct((rows, d), F32),
        compiler_params=_params(1),
        name="outproj",
    )(x, o_attn, o_ssm, gu, gvn, ws, bs, g_out, w_bf16)


def _gmlp_layer_params(p, i, chunk):
    rep = CHUNK // chunk
    ws = jnp.tile(p['gmlp_w_s'][i][:, :chunk, :chunk], (1, rep, rep))
    bs = jnp.tile(p['gmlp_b_s'][i][:, :chunk], (1, rep))
    bs = jnp.repeat(bs.T, GM_HEAD_DIM, axis=1)
    return ws, bs


def _swiglu_cols(h, wg_ref, wu_ref, wd_ref, n_chunks):
    ff = wg_ref.shape[-1]
    fc = ff // n_chunks
    acc = None
    for j in range(n_chunks):
        sl = slice(j * fc, (j + 1) * fc)
        a = jnp.dot(h, wg_ref[:, sl], preferred_element_type=F32)
        b = jnp.dot(h, wu_ref[:, sl], preferred_element_type=F32)
        m = (jax.nn.silu(a) * b).astype(BF16)
        y = jnp.dot(m, wd_ref[sl, :], preferred_element_type=F32)
        acc = y if acc is None else acc + y
    return acc


def _ffn_body(x_ref, g_ref, wg_ref, wu_ref, wd_ref, *rest, final_norm):
    gf_ref = rest[0] if final_norm else None
    o_ref = rest[-1]
    x = x_ref[...]
    h = _rms(x, g_ref[...]).astype(BF16)
    y = x + _swiglu_cols(h, wg_ref, wu_ref, wd_ref, 2)
    o_ref[...] = _rms(y, gf_ref[...]) if final_norm else y


def _ffn(x, g, wg, wu, wd, j, final_g):
    rows, d = x.shape
    ff = wg.shape[-1]
    tm = min(ROW_TILE, rows)
    row = pl.BlockSpec((tm, d), lambda i: (i, 0))
    once = lambda shape: pl.BlockSpec((None,) + shape, lambda i: (j, 0, 0),
                                      pipeline_mode=pl.Buffered(1))
    final_norm = final_g is not None
    return pl.pallas_call(
        functools.partial(_ffn_body, final_norm=final_norm),
        grid=(rows // tm,),
        in_specs=[row, _full((1, d)), once((d, ff)), once((d, ff)), once((ff, d))]
        + [_full((1, d))] * final_norm,
        out_specs=row,
        out_shape=jax.ShapeDtypeStruct((rows, d), F32),
        compiler_params=_params(1),
        name="ffn",
    )(x, g, wg, wu, wd, *([final_g] * final_norm))


def _split_bf16(x):
    hi = x.astype(BF16)
    lo = (x - hi.astype(F32)).astype(BF16)
    return hi, lo


def _router_comb(hf, wr_ref, br_ref):
    h_hi, h_lo = _split_bf16(hf)
    w_hi, w_lo = _split_bf16(wr_ref[...])
    dot = lambda a, b: jnp.dot(a, b, preferred_element_type=F32)
    logits = dot(h_hi, w_hi) + (dot(h_hi, w_lo) + dot(h_lo, w_hi)) + br_ref[...]
    lane = lax.broadcasted_iota(jnp.int32, logits.shape, 1).astype(F32)
    far = float(ROUTER_LANES)
    m1 = jnp.max(logits, axis=-1, keepdims=True)
    i1 = jnp.min(jnp.where(logits == m1, lane, far), axis=-1, keepdims=True)
    rest = jnp.where(lane == i1, -jnp.inf, logits)
    m2 = jnp.max(rest, axis=-1, keepdims=True)
    i2 = jnp.min(jnp.where(rest == m2, lane, far), axis=-1, keepdims=True)
    e2 = jnp.exp(m2 - m1)
    den = 1.0 + e2
    comb = jnp.where(lane == i1, 1.0 / den, 0.0) + jnp.where(lane == i2, e2 / den, 0.0)
    sel = jnp.where((lane == i1) | (lane == i2), 1.0, 0.0)
    return comb, sel


def _moe_body(x_ref, g_ref, wr_ref, br_ref, wg_ref, wu_ref, wd_ref, *rest, final_norm):
    gf_ref = rest[0] if final_norm else None
    o_ref, h_scr, gate_scr, rank_scr, rank_t_scr, lower_scr, cnt_smem = rest[int(final_norm):]
    i = pl.program_id(0)
    e = pl.program_id(1)
    tm = x_ref.shape[0]

    @pl.when((i == 0) & (e == 0))
    def _():
        ri = lax.broadcasted_iota(jnp.int32, (tm, tm), 0)
        ci = lax.broadcasted_iota(jnp.int32, (tm, tm), 1)
        lower_scr[...] = jnp.where(ci < ri, 1.0, 0.0).astype(BF16)

    @pl.when(e == 0)
    def _():
        x = x_ref[...]
        hf = _rms(x, g_ref[...])
        h_scr[...] = hf.astype(BF16)
        comb, sel = _router_comb(hf, wr_ref, br_ref)
        gate_scr[...] = comb
        rank = jnp.dot(lower_scr[...], sel.astype(BF16), preferred_element_type=F32)
        rank = jnp.where(sel > 0.0, rank, -1.0)
        rank_scr[...] = rank
        rank_t_scr[...] = rank.T
        cnt = jnp.sum(sel, axis=0, keepdims=True)
        for k in range(N_EXPERTS):
            cnt_smem[k] = cnt[0, k].astype(jnp.int32)
        o_ref[...] = x

    lane = lax.broadcasted_iota(jnp.int32, (1, ROUTER_LANES), 1)
    pick = lane == e
    rank_col = jnp.sum(jnp.where(pick, rank_scr[...], 0.0), axis=-1, keepdims=True)
    gate_col = jnp.sum(jnp.where(pick, gate_scr[...], 0.0), axis=-1, keepdims=True)
    rank_row = rank_t_scr[pl.ds(e, 1), :]
    n_chunks = (cnt_smem[e] + (MOE_CHUNK - 1)) // MOE_CHUNK
    n_pairs = n_chunks // 2

    def run_chunk(first_slot, size):
        base = first_slot.astype(F32)
        slot_col = lax.broadcasted_iota(jnp.int32, (size, 1), 0).astype(F32) + base
        slot_row = lax.broadcasted_iota(jnp.int32, (1, size), 1).astype(F32) + base
        gather = jnp.where(rank_row == slot_col, 1.0, 0.0).astype(BF16)
        xe = jnp.dot(gather, h_scr[...], preferred_element_type=F32).astype(BF16)
        ye = _swiglu_cols(xe, wg_ref, wu_ref, wd_ref, 1).astype(BF16)
        sr = min(MOE_SCATTER_ROWS, tm)
        for rb in range(tm // sr):
            sl = slice(rb * sr, (rb + 1) * sr)
            scatter = jnp.where(rank_col[sl] == slot_row, 1.0, 0.0).astype(BF16)
            o_ref[sl, :] += gate_col[sl] * jnp.dot(scatter, ye, preferred_element_type=F32)

    def pair(c, carry):
        run_chunk(c * (2 * MOE_CHUNK), 2 * MOE_CHUNK)
        return carry

    lax.fori_loop(0, n_pairs, pair, 0)

    @pl.when(n_chunks % 2 == 1)
    def _():
        run_chunk(n_pairs * (2 * MOE_CHUNK), MOE_CHUNK)

    if final_norm:
        @pl.when(e == pl.num_programs(1) - 1)
        def _():
            o_ref[...] = _rms(o_ref[...], gf_ref[...])


def _moe(x, g, wr_pad, br_pad, wg, wu, wd, j, final_g):
    rows, d = x.shape
    _, n_e, _, ff = wg.shape
    tm = min(MOE_ROW_TILE, rows)
    row = pl.BlockSpec((tm, d), lambda i, e: (i, 0))
    final_norm = final_g is not None
    return pl.pallas_call(
        functools.partial(_moe_body, final_norm=final_norm),
        grid=(rows // tm, n_e),
        in_specs=[row, _full((1, d)), _full((d, ROUTER_LANES)), _full((1, ROUTER_LANES)),
                  pl.BlockSpec((None, None, d, ff), lambda i, e: (j, e, 0, 0)),
                  pl.BlockSpec((None, None, d, ff), lambda i, e: (j, e, 0, 0)),
                  pl.BlockSpec((None, None, ff, d), lambda i, e: (j, e, 0, 0))]
        + [_full((1, d))] * final_norm,
        out_specs=row,
        out_shape=jax.ShapeDtypeStruct((rows, d), F32),
        scratch_shapes=[pltpu.VMEM((tm, d), BF16),
                        pltpu.VMEM((tm, ROUTER_LANES), F32),
                        pltpu.VMEM((tm, ROUTER_LANES), F32),
                        pltpu.VMEM((ROUTER_LANES, tm), F32),
                        pltpu.VMEM((tm, tm), BF16),
                        pltpu.SMEM((N_EXPERTS,), jnp.int32)],
        compiler_params=_params(2),
        name="moe",
    )(x, g, wr_pad, br_pad, wg, wu, wd, *([final_g] * final_norm))


def _trunk(x, p, past):
    b, l, d = x.shape
    depth = p['w_in'].shape[0]
    rows = b * l
    xr = x.reshape(rows, d)
    prompt = past is None
    new = {'k': [], 'v': [], 're': [], 'im': [], 'gv': []}
    for i in range(depth):
        g_mix = p['norm_mix_g'][i].reshape(1, d)
        gv = p['gmlp_v_norm_g'][i].reshape(1, GM_W)
        q, k, v, u, gu, gvn = _inproj(xr, g_mix, p['w_in_bf16'], gv, i)
        u = u.reshape(b, l, SSM_W)
        sinks = p['attn_sinks'][i]
        sp = _ssm_layer_params(p, i)
        if prompt:
            o_attn = _attn_prompt(q, k, v, sinks, batch=b, seq=l)
            nw = min(WINDOW, l)
            k_win = k.reshape(b, l, N_KV_HEADS, HEAD_DIM)[:, l - nw:]
            v_win = v.reshape(b, l, N_KV_HEADS, HEAD_DIM)[:, l - nw:]
            h0 = jnp.zeros((b, 2 * SSM_FLAT), F32)
            o_ssm, s_fin = _ssm(u, h0, sp, tc=CHUNK)
        else:
            w = past[0].shape[2]
            o_attn, k_win, v_win = _attn_sample(
                q, k, v, past[0][i].reshape(b, w, KV_W), past[1][i].reshape(b, w, KV_W),
                sinks, batch=b, t=l)
            k_win = k_win.reshape(b, w, N_KV_HEADS, HEAD_DIM)
            v_win = v_win.reshape(b, w, N_KV_HEADS, HEAD_DIM)
            h0 = jnp.concatenate([past[2][i].reshape(b, SSM_FLAT),
                                  past[3][i].reshape(b, SSM_FLAT)], axis=1)
            o_ssm, s_fin = _ssm(u, h0, sp, tc=l)
            new['gv'].append(gvn.reshape(b, l, GM_HEADS, GM_HEAD_DIM))
        ws, bs = _gmlp_layer_params(p, i, min(l, CHUNK))
        xr = _outproj(xr, o_attn, o_ssm.reshape(rows, SSM_W), gu, gvn, ws, bs,
                      p['mix_out_norm_g'][i].reshape(1, d), p['w_out_bf16'], i,
                      chunk=min(l, CHUNK))
        g_ffn = p['norm_ffn_g'][i].reshape(1, d)
        final_g = p['final_norm_g'].reshape(1, d) if i == depth - 1 else None
        j = i // 2
        if i % 2 == 0:
            xr = _ffn(xr, g_ffn, p['ffn_w_gate_bf16'], p['ffn_w_up_bf16'],
                      p['ffn_w_down_bf16'], j, final_g)
        else:
            wr = jnp.pad(p['moe_w_router'][j], ((0, 0), (0, ROUTER_LANES - N_EXPERTS)))
            br = jnp.pad(p['moe_b_router'][j], (0, ROUTER_LANES - N_EXPERTS),
                         constant_values=NEG_BIG).reshape(1, ROUTER_LANES)
            xr = _moe(xr, g_ffn, wr, br, p['moe_w_gate_bf16'], p['moe_w_up_bf16'],
                      p['moe_w_down_bf16'], j, final_g)
        new['k'].append(k_win)
        new['v'].append(v_win)
        new['re'].append(s_fin[:, :SSM_FLAT].reshape(b, SSM_GROUPS, SSM_STATE))
        new['im'].append(s_fin[:, SSM_FLAT:].reshape(b, SSM_GROUPS, SSM_STATE))
    return xr.reshape(b, l, d), new


def kernel(x_prompt, x_sample, cache_k_win, cache_v_win, state_ssm_re, state_ssm_im,
           norm_mix_g, w_in, attn_sinks, ssm_a_re, ssm_a_im, ssm_log_dt, ssm_b_re, ssm_b_im,
           ssm_c_re, ssm_c_im, ssm_d, ssm_w_glu, ssm_b_glu, gmlp_v_norm_g, gmlp_w_s, gmlp_b_s,
           mix_out_norm_g, w_out, norm_ffn_g, ffn_w_gate, ffn_w_up, ffn_w_down,
           moe_w_router, moe_b_router, moe_w_gate, moe_w_up, moe_w_down, final_norm_g):
    params = {
        'norm_mix_g': norm_mix_g, 'w_in': w_in, 'attn_sinks': attn_sinks,
        'ssm_a_re': ssm_a_re, 'ssm_a_im': ssm_a_im, 'ssm_log_dt': ssm_log_dt,
        'ssm_b_re': ssm_b_re, 'ssm_b_im': ssm_b_im, 'ssm_c_re': ssm_c_re, 'ssm_c_im': ssm_c_im,
        'ssm_d': ssm_d, 'ssm_w_glu': ssm_w_glu, 'ssm_b_glu': ssm_b_glu,
        'gmlp_v_norm_g': gmlp_v_norm_g, 'gmlp_w_s': gmlp_w_s, 'gmlp_b_s': gmlp_b_s,
        'mix_out_norm_g': mix_out_norm_g, 'w_out': w_out, 'norm_ffn_g': norm_ffn_g,
        'ffn_w_gate': ffn_w_gate, 'ffn_w_up': ffn_w_up, 'ffn_w_down': ffn_w_down,
        'moe_w_router': moe_w_router, 'moe_b_router': moe_b_router,
        'moe_w_gate': moe_w_gate, 'moe_w_up': moe_w_up, 'moe_w_down': moe_w_down,
        'final_norm_g': final_norm_g,
    }
    for name in ('w_in', 'w_out', 'ffn_w_gate', 'ffn_w_up', 'ffn_w_down',
                 'moe_w_gate', 'moe_w_up', 'moe_w_down'):
        params[name + '_bf16'] = params[name].astype(BF16)
    y_p, st_p = _trunk(x_prompt, params, None)
    y_s, st_s = _trunk(x_sample, params,
                       (cache_k_win, cache_v_win, state_ssm_re, state_ssm_im))
    return (y_p, y_s,
            jnp.stack(st_p['k']), jnp.stack(st_p['v']),
            jnp.stack(st_p['re']), jnp.stack(st_p['im']),
            jnp.stack(st_s['k']), jnp.stack(st_s['v']),
            jnp.stack(st_s['re']), jnp.stack(st_s['im']),
            jnp.stack(st_s['gv']))
```

```python
import functools
import math

import jax
import jax.numpy as jnp
from jax import lax
from jax.experimental import pallas as pl
from jax.experimental.pallas import tpu as pltpu

F32 = jnp.float32
BF16 = jnp.bfloat16

EPS = 1e-6
HEAD_DIM = 64
N_HEADS = 8
N_KV_HEADS = 2
GQA_GROUP = N_HEADS // N_KV_HEADS
ATTN_W = N_HEADS * HEAD_DIM
KV_W = N_KV_HEADS * HEAD_DIM
WINDOW = 128
ATTN_SCALE = 1.0 / math.sqrt(HEAD_DIM)
LOG2E = math.log2(math.e)
SSM_W = 256
SSM_GROUP_CH = 16
SSM_GROUPS = 16
SSM_STATE = 64
SSM_FLAT = SSM_GROUPS * SSM_STATE
GM_W = 256
GM_HEADS = 4
GM_HEAD_DIM = 64
CHUNK = 128
Q_END = ATTN_W
K_END = Q_END + KV_W
V_END = K_END + KV_W
S_END = V_END + SSM_W
IN_COLS = S_END + 2 * GM_W
N_EXPERTS = 8
ROUTER_LANES = 128
NEG_BIG = -1e30

VMEM_LIMIT_BYTES = 56 * 1024 * 1024
ROW_TILE = 512
ATTN_Q_BLOCKS = 4
MOE_ROW_TILE = 1024
MOE_CHUNK = 128
MOE_SCATTER_ROWS = 256
SCAN_LANES = 512


def _params(n_axes):
    return pltpu.CompilerParams(
        dimension_semantics=("arbitrary",) * n_axes,
        vmem_limit_bytes=VMEM_LIMIT_BYTES)


def _rms(x, g):
    return x * lax.rsqrt(jnp.mean(x * x, axis=-1, keepdims=True) + EPS) * g


def _bdot(a, b):
    return jnp.dot(a.astype(BF16), b.astype(BF16), preferred_element_type=F32)


def _full(shape):
    return pl.BlockSpec(shape, lambda *_: (0,) * len(shape))


def _group_mean(a, n_groups, width):
    lane = lax.broadcasted_iota(jnp.int32, (1, n_groups * width), 1)
    out = jnp.zeros_like(a)
    for h in range(n_groups):
        m = (lane >= h * width) & (lane < (h + 1) * width)
        s = jnp.sum(jnp.where(m, a, 0.0), axis=-1, keepdims=True) * (1.0 / width)
        out = jnp.where(m, s, out)
    return out


def _inproj_body(x_ref, g_ref, w_ref, gv_ref, q_ref, k_ref, v_ref, u_ref, gu_ref, gvn_ref):
    h = _rms(x_ref[...], g_ref[...])
    z = jnp.dot(h.astype(BF16), w_ref[...], preferred_element_type=F32)
    q_ref[...] = (z[:, :Q_END] * (ATTN_SCALE * LOG2E)).astype(BF16)
    k_ref[...] = z[:, Q_END:K_END]
    v_ref[...] = z[:, K_END:V_END]
    u_ref[...] = z[:, V_END:S_END]
    g = jax.nn.gelu(z[:, S_END:])
    gu_ref[...] = g[:, :GM_W]
    vv = g[:, GM_W:]
    mu = _group_mean(vv, GM_HEADS, GM_HEAD_DIM)
    var = _group_mean(jnp.square(vv - mu), GM_HEADS, GM_HEAD_DIM)
    gvn_ref[...] = (vv - mu) * lax.rsqrt(var + EPS) * gv_ref[...]


def _layer(shape, j):
    return pl.BlockSpec((None,) + tuple(shape), lambda *_: (j,) + (0,) * len(shape))


def _inproj(x, g, w_bf16, gv, j):
    rows, d = x.shape
    tm = min(ROW_TILE, rows)
    nt = rows // tm
    row = lambda w: pl.BlockSpec((tm, w), lambda i: (i, 0))
    return pl.pallas_call(
        _inproj_body,
        grid=(nt,),
        in_specs=[row(d), _full((1, d)), _layer((d, IN_COLS), j), _full((1, GM_W))],
        out_specs=[row(ATTN_W), row(KV_W), row(KV_W), row(SSM_W), row(GM_W), row(GM_W)],
        out_shape=[jax.ShapeDtypeStruct((rows, ATTN_W), BF16),
                   jax.ShapeDtypeStruct((rows, KV_W), F32),
                   jax.ShapeDtypeStruct((rows, KV_W), F32),
                   jax.ShapeDtypeStruct((rows, SSM_W), F32),
                   jax.ShapeDtypeStruct((rows, GM_W), F32),
                   jax.ShapeDtypeStruct((rows, GM_W), F32)],
        compiler_params=_params(1),
        name="inproj",
    )(x, g, w_bf16, gv)


def _sink_softmax(s, mask, sink):
    s = jnp.where(mask, s, -jnp.inf)
    m = jnp.maximum(jnp.max(s, axis=-1, keepdims=True), sink)
    e = jnp.exp2(s - m)
    return e / (jnp.sum(e, axis=-1, keepdims=True) + jnp.exp2(sink - m))


def _head_pair_operands(x):
    low = lax.broadcasted_iota(jnp.int32, (1, KV_W), 1) < HEAD_DIM
    swapped = pltpu.roll(x, HEAD_DIM, axis=1)
    keep = lambda a, in_low: jnp.where(low == in_low, a, 0.0).astype(BF16)
    return ((keep(x, True), keep(swapped, False)),
            (keep(swapped, True), keep(x, False)))


def _attn_prompt_body(sink_ref, q_ref, kc_ref, kp_ref, vc_ref, vp_ref, o_ref, *, nq):
    n = pl.program_id(1)
    qi = lax.broadcasted_iota(jnp.int32, (WINDOW, 2 * WINDOW), 0)
    si = lax.broadcasted_iota(jnp.int32, (WINDOW, 2 * WINDOW), 1)
    dist = WINDOW + qi - si
    band = (dist >= 0) & (dist < WINDOW)
    first = band & ((n > 0) | (si >= WINDOW))
    k_ops = _head_pair_operands(jnp.concatenate([kp_ref[...], kc_ref[...]], axis=0))
    v_ops = _head_pair_operands(jnp.concatenate([vp_ref[...], vc_ref[...]], axis=0))
    pair_w = 2 * HEAD_DIM
    for j in range(nq):
        mask = first if j == 0 else band
        q_rows = slice(j * WINDOW, (j + 1) * WINDOW)
        kv_rows = slice(j * WINDOW, (j + 2) * WINDOW)
        for hp in range(N_HEADS // 2):
            kh = (2 * hp) // GQA_GROUP
            qp = q_ref[q_rows, hp * pair_w:(hp + 1) * pair_w]
            acc = None
            for par in range(2):
                sink = sink_ref[2 * hp + par] * LOG2E
                s = lax.dot_general(qp, k_ops[kh][par][kv_rows], (((1,), (1,)), ((), ())),
                                    preferred_element_type=F32)
                s = jnp.where(mask, s, -jnp.inf)
                m = jnp.maximum(jnp.max(s, axis=-1, keepdims=True), sink)
                e = jnp.exp2(s - m)
                den = jnp.sum(e, axis=-1, keepdims=True) + jnp.exp2(sink - m)
                pv = jnp.dot(e.astype(BF16), v_ops[kh][par][kv_rows],
                             preferred_element_type=F32)
                pv = pv * (1.0 / den)
                acc = pv if acc is None else acc + pv
            o_ref[q_rows, hp * pair_w:(hp + 1) * pair_w] = acc


def _attn_prompt(q, k, v, sinks, *, batch, seq):
    nq = ATTN_Q_BLOCKS
    nb = seq // WINDOW
    steps = nb // nq
    cur = lambda w: pl.BlockSpec((nq * WINDOW, w), lambda b, n: (b * steps + n, 0))
    prev = lambda w: pl.BlockSpec(
        (WINDOW, w), lambda b, n: (b * nb + jnp.maximum(n * nq - 1, 0), 0))
    return pl.pallas_call(
        functools.partial(_attn_prompt_body, nq=nq),
        grid=(batch, steps),
        in_specs=[pl.BlockSpec(memory_space=pltpu.SMEM),
                  cur(ATTN_W), cur(KV_W), prev(KV_W), cur(KV_W), prev(KV_W)],
        out_specs=cur(ATTN_W),
        out_shape=jax.ShapeDtypeStruct((batch * seq, ATTN_W), F32),
        compiler_params=_params(2),
        name="attn_prompt",
    )(sinks, q, k, k, v, v)


def _attn_sample_body(sink_ref, q_ref, k_ref, v_ref, kc_ref, vc_ref, kin_ref, vin_ref,
                      o_ref, ko_ref, vo_ref, *, bb, t):
    del kin_ref, vin_ref
    w = kc_ref.shape[-1]
    pair_w = 2 * HEAD_DIM
    lane = lax.broadcasted_iota(jnp.int32, (1, w), 1)
    new_col = lane < t
    k_new_t = k_ref[...].T
    v_new_t = v_ref[...].T
    qf = q_ref[...].astype(F32)
    unit = 2 * t
    upper = lax.broadcasted_iota(jnp.int32, (unit, 1), 0) < t
    zeros_half = jnp.zeros((HEAD_DIM, 2 * w), BF16)
    placed = lambda a, par: jnp.concatenate((a, zeros_half) if par == 0 else (zeros_half, a), axis=0)
    scores, sinks, values = [], [], []
    for b in range(bb):
        shift = (w - b * t) % w
        k_cols = jnp.where(new_col, pltpu.roll(k_new_t, shift, axis=1), 0.0)
        v_cols = jnp.where(new_col, pltpu.roll(v_new_t, shift, axis=1), 0.0)
        k_old = kc_ref[b]
        v_old = vc_ref[b]
        ko_ref[b] = pltpu.roll(jnp.where(new_col, k_cols, k_old), w - t, axis=1)
        vo_ref[b] = pltpu.roll(jnp.where(new_col, v_cols, v_old), w - t, axis=1)
        k_all = jnp.concatenate([k_old, k_cols], axis=1).astype(BF16)
        v_all = jnp.concatenate([v_old, v_cols], axis=1).astype(BF16)
        for kh in range(N_KV_HEADS):
            kv_sl = slice(kh * HEAD_DIM, (kh + 1) * HEAD_DIM)
            q2 = jnp.concatenate(
                [qf[b * t:(b + 1) * t, (2 * kh + c) * pair_w:(2 * kh + c + 1) * pair_w]
                 for c in range(2)], axis=0).astype(BF16)
            for par in range(2):
                h_a = GQA_GROUP * kh + par
                sinks.append(jnp.where(upper, sink_ref[h_a], sink_ref[h_a + 2]) * LOG2E)
                scores.append(jnp.dot(q2, placed(k_all[kv_sl], par),
                                      preferred_element_type=F32))
                values.append(placed(v_all[kv_sl], par))
    s = jnp.concatenate(scores, axis=0)
    ti = lax.broadcasted_iota(jnp.int32, s.shape, 0) % t
    si = lax.broadcasted_iota(jnp.int32, s.shape, 1)
    dist = w + ti - si
    p = _sink_softmax(s, (dist >= 0) & (dist < WINDOW), jnp.concatenate(sinks, axis=0)).astype(BF16)
    for b in range(bb):
        for kh in range(N_KV_HEADS):
            acc = None
            for par in range(2):
                u = (b * N_KV_HEADS + kh) * 2 + par
                pv = lax.dot_general(p[u * unit:(u + 1) * unit], values[u],
                                     (((1,), (1,)), ((), ())), preferred_element_type=F32)
                acc = pv if acc is None else acc + pv
            for c in range(2):
                o_ref[b * t:(b + 1) * t, (2 * kh + c) * pair_w:(2 * kh + c + 1) * pair_w] = (
                    acc[c * t:(c + 1) * t])


def _attn_sample(q, k, v, kc_t, vc_t, k_out, v_out, sinks, layer, *, batch, t):
    w = kc_t.shape[-1]
    bb = w // t
    row = lambda c: pl.BlockSpec((bb * t, c), lambda i: (i, 0))
    buf = pl.BlockSpec((None, bb, KV_W, w), lambda i: (layer, i, 0, 0))
    anywhere = pl.BlockSpec(memory_space=pl.ANY)
    return pl.pallas_call(
        functools.partial(_attn_sample_body, bb=bb, t=t),
        grid=(batch // bb,),
        in_specs=[pl.BlockSpec(memory_space=pltpu.SMEM),
                  row(ATTN_W), row(KV_W), row(KV_W), buf, buf, anywhere, anywhere],
        out_specs=[row(ATTN_W), buf, buf],
        out_shape=[jax.ShapeDtypeStruct((batch * t, ATTN_W), F32),
                   jax.ShapeDtypeStruct(k_out.shape, F32),
                   jax.ShapeDtypeStruct(v_out.shape, F32)],
        input_output_aliases={6: 1, 7: 2},
        compiler_params=_params(1),
        name="attn_sample",
    )(sinks, q, k, v, kc_t, vc_t, k_out, v_out)


def _ssm_body(u_ref, h0_ref, are_ref, aim_ref, ldt_ref, bre_ref, bim_ref, cre_ref, cim_ref,
              d_ref, wglu_ref, bglu_ref, o_ref, s_ref, abar_scr, bbar_scr, bu_scr, h_scr,
              *, r, tc):
    c = pl.program_id(0)

    @pl.when(c == 0)
    def _():
        ar = are_ref[...]
        ai = aim_ref[...]
        dt = jnp.exp(ldt_ref[...])
        decay = jnp.exp(dt * ar)
        abr = decay * jnp.cos(dt * ai)
        abi = decay * jnp.sin(dt * ai)
        den = ar * ar + ai * ai
        nr = abr - 1.0
        fr = (nr * ar + abi * ai) / den
        fi = (abi * ar - nr * ai) / den
        abar_scr[0:1, :] = abr
        abar_scr[1:2, :] = abi
        br = bre_ref[...]
        bi = bim_ref[...]
        bbar_scr[:, :SSM_FLAT] = (fr * br - fi * bi).astype(BF16)
        bbar_scr[:, SSM_FLAT:] = (fr * bi + fi * br).astype(BF16)
        h_scr[...] = h0_ref[...]

    u = jnp.swapaxes(u_ref[...], 0, 1).reshape(tc * r, SSM_W)
    bu_scr[...] = jnp.dot(u.astype(BF16), bbar_scr[...], preferred_element_type=F32)

    for rg in range(r // 8):
        for lc in range(SSM_FLAT // SCAN_LANES):
            re_sl = slice(lc * SCAN_LANES, (lc + 1) * SCAN_LANES)
            im_sl = slice(SSM_FLAT + lc * SCAN_LANES, SSM_FLAT + (lc + 1) * SCAN_LANES)
            rg_sl = slice(rg * 8, (rg + 1) * 8)
            ar = jnp.broadcast_to(abar_scr[0:1, re_sl], (8, SCAN_LANES))
            ai = jnp.broadcast_to(abar_scr[1:2, re_sl], (8, SCAN_LANES))

            def step(t, carry, re_sl=re_sl, im_sl=im_sl, rg=rg, ar=ar, ai=ai):
                hr, hi = carry
                rows = pl.ds(pl.multiple_of(t * r + rg * 8, 8), 8)
                nr = ar * hr - ai * hi + bu_scr[rows, re_sl]
                ni = ar * hi + ai * hr + bu_scr[rows, im_sl]
                bu_scr[rows, re_sl] = nr
                bu_scr[rows, im_sl] = ni
                return nr, ni

            hr, hi = lax.fori_loop(0, tc, step, (h_scr[rg_sl, re_sl], h_scr[rg_sl, im_sl]),
                                   unroll=4)
            h_scr[rg_sl, re_sl] = hr
            h_scr[rg_sl, im_sl] = hi

    y = (jnp.dot(bu_scr[:, :SSM_FLAT].astype(BF16), cre_ref[...], preferred_element_type=F32)
         - jnp.dot(bu_scr[:, SSM_FLAT:].astype(BF16), cim_ref[...], preferred_element_type=F32))
    y = jax.nn.gelu(y + d_ref[...] * u)
    gl = jnp.dot(y.astype(BF16), wglu_ref[...], preferred_element_type=F32) + bglu_ref[...]
    o_ref[...] = jnp.swapaxes((y * jax.nn.sigmoid(gl)).reshape(tc, r, SSM_W), 0, 1)

    @pl.when(c == pl.num_programs(0) - 1)
    def _():
        s_ref[...] = h_scr[...]


def _ssm(u, h0, sp, *, tc):
    r, steps, _ = u.shape
    blk = tc * r
    seq_blk = pl.BlockSpec((r, tc, SSM_W), lambda c: (0, c, 0))
    return pl.pallas_call(
        functools.partial(_ssm_body, r=r, tc=tc),
        grid=(steps // tc,),
        in_specs=[seq_blk,
                  _full((r, 2 * SSM_FLAT)),
                  _full((1, SSM_FLAT)), _full((1, SSM_FLAT)), _full((1, SSM_FLAT)),
                  _full((SSM_W, SSM_FLAT)), _full((SSM_W, SSM_FLAT)),
                  _full((SSM_FLAT, SSM_W)), _full((SSM_FLAT, SSM_W)),
                  _full((1, SSM_W)), _full((SSM_W, SSM_W)), _full((1, SSM_W))],
        out_specs=[seq_blk, _full((r, 2 * SSM_FLAT))],
        out_shape=[jax.ShapeDtypeStruct((r, steps, SSM_W), F32),
                   jax.ShapeDtypeStruct((r, 2 * SSM_FLAT), F32)],
        scratch_shapes=[pltpu.VMEM((2, SSM_FLAT), F32),
                        pltpu.VMEM((SSM_W, 2 * SSM_FLAT), BF16),
                        pltpu.VMEM((blk, 2 * SSM_FLAT), F32),
                        pltpu.VMEM((r, 2 * SSM_FLAT), F32)],
        compiler_params=_params(1),
        name="ssm",
    )(u, h0, sp['a_re'], sp['a_im'], sp['log_dt'], sp['b_re'], sp['b_im'],
      sp['c_re'], sp['c_im'], sp['d'], sp['w_glu'], sp['b_glu'])


def _ssm_layer_params(p, i):
    eye = jnp.eye(SSM_GROUPS, dtype=F32)

    def b_blockdiag(b):
        return jnp.einsum('gpc,gh->gchp', b, eye).reshape(SSM_W, SSM_FLAT)

    def c_blockdiag(c):
        return jnp.einsum('gcp,gh->gphc', c, eye).reshape(SSM_FLAT, SSM_W)

    return {
        'a_re': p['ssm_a_re'][i].reshape(1, SSM_FLAT),
        'a_im': p['ssm_a_im'][i].reshape(1, SSM_FLAT),
        'log_dt': jnp.repeat(p['ssm_log_dt'][i], SSM_STATE).reshape(1, SSM_FLAT),
        'b_re': b_blockdiag(p['ssm_b_re'][i]),
        'b_im': b_blockdiag(p['ssm_b_im'][i]),
        'c_re': c_blockdiag(p['ssm_c_re'][i]).astype(BF16),
        'c_im': c_blockdiag(p['ssm_c_im'][i]).astype(BF16),
        'd': p['ssm_d'][i].reshape(1, SSM_W),
        'w_glu': p['ssm_w_glu'][i].astype(BF16),
        'b_glu': p['ssm_b_glu'][i].reshape(1, SSM_W),
    }


def _outproj_body(x_ref, oa_ref, os_ref, gu_ref, gvn_ref, ws_ref, bs_ref, go_ref, w_ref, o_ref,
                  *, chunk):
    tm = x_ref.shape[0]
    ri = lax.broadcasted_iota(jnp.int32, (CHUNK, CHUNK), 0)
    ci = lax.broadcasted_iota(jnp.int32, (CHUNK, CHUNK), 1)
    causal = (ri // chunk == ci // chunk) & (ci <= ri)
    lane = lax.broadcasted_iota(jnp.int32, (1, GM_W), 1)
    zs = []
    for cblk in range(tm // CHUNK):
        vn = gvn_ref[cblk * CHUNK:(cblk + 1) * CHUNK, :].astype(BF16)
        z = bs_ref[...]
        for h in range(GM_HEADS):
            w_h = jnp.where(causal, ws_ref[h], 0.0).astype(BF16)
            z_h = jnp.dot(w_h, vn, preferred_element_type=F32)
            head = (lane >= h * GM_HEAD_DIM) & (lane < (h + 1) * GM_HEAD_DIM)
            z = z + jnp.where(head, z_h, 0.0)
        zs.append(z)
    o_gm = gu_ref[...] * jnp.concatenate(zs, axis=0)
    go = go_ref[...]
    o = jnp.concatenate([
        _rms(oa_ref[...], go[:, :ATTN_W]),
        _rms(os_ref[...], go[:, ATTN_W:ATTN_W + SSM_W]),
        _rms(o_gm, go[:, ATTN_W + SSM_W:])], axis=-1)
    o_ref[...] = x_ref[...] + jnp.dot(o.astype(BF16), w_ref[...], preferred_element_type=F32)


def _outproj(x, o_attn, o_ssm, gu, gvn, ws, bs, g_out, w_bf16, j, *, chunk):
    rows, d = x.shape
    tm = min(ROW_TILE, rows)
    row = lambda w: pl.BlockSpec((tm, w), lambda i: (i, 0))
    return pl.pallas_call(
        functools.partial(_outproj_body, chunk=chunk),
        grid=(rows // tm,),
        in_specs=[row(d), row(ATTN_W), row(SSM_W), row(GM_W), row(GM_W),
                  _full((GM_HEADS, CHUNK, CHUNK)), _full((CHUNK, GM_W)),
                  _full((1, d)), _layer((d, d), j)],
        out_specs=row(d),
        out_shape=jax.ShapeDtypeStruct((rows, d), F32),
        compiler_params=_params(1),
        name="outproj",
    )(x, o_attn, o_ssm, gu, gvn, ws, bs, g_out, w_bf16)


def _gmlp_layer_params(p, i, chunk):
    rep = CHUNK // chunk
    ws = jnp.tile(p['gmlp_w_s'][i][:, :chunk, :chunk], (1, rep, rep))
    bs = jnp.tile(p['gmlp_b_s'][i][:, :chunk], (1, rep))
    bs = jnp.repeat(bs.T, GM_HEAD_DIM, axis=1)
    return ws, bs


def _swiglu_cols(h, wg_ref, wu_ref, wd_ref, n_chunks):
    ff = wg_ref.shape[-1]
    fc = ff // n_chunks
    acc = None
    for j in range(n_chunks):
        sl = slice(j * fc, (j + 1) * fc)
        a = jnp.dot(h, wg_ref[:, sl], preferred_element_type=F32)
        b = jnp.dot(h, wu_ref[:, sl], preferred_element_type=F32)
        m = (jax.nn.silu(a) * b).astype(BF16)
        y = jnp.dot(m, wd_ref[sl, :], preferred_element_type=F32)
        acc = y if acc is None else acc + y
    return acc


def _ffn_body(x_ref, g_ref, wg_ref, wu_ref, wd_ref, *rest, final_norm):
    gf_ref = rest[0] if final_norm else None
    o_ref = rest[-1]
    x = x_ref[...]
    h = _rms(x, g_ref[...]).astype(BF16)
    y = x + _swiglu_cols(h, wg_ref, wu_ref, wd_ref, 2)
    o_ref[...] = _rms(y, gf_ref[...]) if final_norm else y


def _ffn(x, g, wg, wu, wd, j, final_g):
    rows, d = x.shape
    ff = wg.shape[-1]
    tm = min(ROW_TILE, rows)
    row = pl.BlockSpec((tm, d), lambda i: (i, 0))
    once = lambda shape: pl.BlockSpec((None,) + shape, lambda i: (j, 0, 0),
                                      pipeline_mode=pl.Buffered(1))
    final_norm = final_g is not None
    return pl.pallas_call(
        functools.partial(_ffn_body, final_norm=final_norm),
        grid=(rows // tm,),
        in_specs=[row, _full((1, d)), once((d, ff)), once((d, ff)), once((ff, d))]
        + [_full((1, d))] * final_norm,
        out_specs=row,
        out_shape=jax.ShapeDtypeStruct((rows, d), F32),
        compiler_params=_params(1),
        name="ffn",
    )(x, g, wg, wu, wd, *([final_g] * final_norm))


def _split_bf16(x):
    hi = x.astype(BF16)
    lo = (x - hi.astype(F32)).astype(BF16)
    return hi, lo


def _router_comb(hf, wr_ref, br_ref):
    h_hi, h_lo = _split_bf16(hf)
    w_hi, w_lo = _split_bf16(wr_ref[...])
    dot = lambda a, b: jnp.dot(a, b, preferred_element_type=F32)
    logits = dot(h_hi, w_hi) + (dot(h_hi, w_lo) + dot(h_lo, w_hi)) + br_ref[...]
    lane = lax.broadcasted_iota(jnp.int32, logits.shape, 1).astype(F32)
    far = float(ROUTER_LANES)
    m1 = jnp.max(logits, axis=-1, keepdims=True)
    i1 = jnp.min(jnp.where(logits == m1, lane, far), axis=-1, keepdims=True)
    rest = jnp.where(lane == i1, -jnp.inf, logits)
    m2 = jnp.max(rest, axis=-1, keepdims=True)
    i2 = jnp.min(jnp.where(rest == m2, lane, far), axis=-1, keepdims=True)
    e2 = jnp.exp(m2 - m1)
    den = 1.0 + e2
    comb = jnp.where(lane == i1, 1.0 / den, 0.0) + jnp.where(lane == i2, e2 / den, 0.0)
    sel = jnp.where((lane == i1) | (lane == i2), 1.0, 0.0)
    return comb, sel


def _moe_body(x_ref, g_ref, wr_ref, br_ref, wg_ref, wu_ref, wd_ref, *rest, final_norm):
    gf_ref = rest[0] if final_norm else None
    o_ref, h_scr, gate_scr, rank_scr, rank_t_scr, lower_scr, cnt_smem = rest[int(final_norm):]
    i = pl.program_id(0)
    e = pl.program_id(1)
    tm = x_ref.shape[0]

    @pl.when((i == 0) & (e == 0))
    def _():
        ri = lax.broadcasted_iota(jnp.int32, (tm, tm), 0)
        ci = lax.broadcasted_iota(jnp.int32, (tm, tm), 1)
        lower_scr[...] = jnp.where(ci < ri, 1.0, 0.0).astype(BF16)

    @pl.when(e == 0)
    def _():
        x = x_ref[...]
        hf = _rms(x, g_ref[...])
        h_scr[...] = hf.astype(BF16)
        comb, sel = _router_comb(hf, wr_ref, br_ref)
        gate_scr[...] = comb
        rank = jnp.dot(lower_scr[...], sel.astype(BF16), preferred_element_type=F32)
        rank = jnp.where(sel > 0.0, rank, -1.0)
        rank_scr[...] = rank
        rank_t_scr[...] = rank.T
        cnt = jnp.sum(sel, axis=0, keepdims=True)
        for k in range(N_EXPERTS):
            cnt_smem[k] = cnt[0, k].astype(jnp.int32)
        o_ref[...] = x

    lane = lax.broadcasted_iota(jnp.int32, (1, ROUTER_LANES), 1)
    pick = lane == e
    rank_col = jnp.sum(jnp.where(pick, rank_scr[...], 0.0), axis=-1, keepdims=True)
    gate_col = jnp.sum(jnp.where(pick, gate_scr[...], 0.0), axis=-1, keepdims=True)
    rank_row = rank_t_scr[pl.ds(e, 1), :]
    n_chunks = (cnt_smem[e] + (MOE_CHUNK - 1)) // MOE_CHUNK
    n_pairs = n_chunks // 2

    def run_chunk(first_slot, size):
        base = first_slot.astype(F32)
        slot_col = lax.broadcasted_iota(jnp.int32, (size, 1), 0).astype(F32) + base
        slot_row = lax.broadcasted_iota(jnp.int32, (1, size), 1).astype(F32) + base
        gather = jnp.where(rank_row == slot_col, 1.0, 0.0).astype(BF16)
        xe = jnp.dot(gather, h_scr[...], preferred_element_type=F32).astype(BF16)
        ye = _swiglu_cols(xe, wg_ref, wu_ref, wd_ref, 1).astype(BF16)
        sr = min(MOE_SCATTER_ROWS, tm)
        for rb in range(tm // sr):
            sl = slice(rb * sr, (rb + 1) * sr)
            scatter = jnp.where(rank_col[sl] == slot_row, 1.0, 0.0).astype(BF16)
            o_ref[sl, :] += gate_col[sl] * jnp.dot(scatter, ye, preferred_element_type=F32)

    def pair(c, carry):
        run_chunk(c * (2 * MOE_CHUNK), 2 * MOE_CHUNK)
        return carry

    lax.fori_loop(0, n_pairs, pair, 0)

    @pl.when(n_chunks % 2 == 1)
    def _():
        run_chunk(n_pairs * (2 * MOE_CHUNK), MOE_CHUNK)

    if final_norm:
        @pl.when(e == pl.num_programs(1) - 1)
        def _():
            o_ref[...] = _rms(o_ref[...], gf_ref[...])


def _moe(x, g, wr_pad, br_pad, wg, wu, wd, j, final_g):
    rows, d = x.shape
    _, n_e, _, ff = wg.shape
    tm = min(MOE_ROW_TILE, rows)
    row = pl.BlockSpec((tm, d), lambda i, e: (i, 0))
    final_norm = final_g is not None
    return pl.pallas_call(
        functools.partial(_moe_body, final_norm=final_norm),
        grid=(rows // tm, n_e),
        in_specs=[row, _full((1, d)), _full((d, ROUTER_LANES)), _full((1, ROUTER_LANES)),
                  pl.BlockSpec((None, None, d, ff), lambda i, e: (j, e, 0, 0)),
                  pl.BlockSpec((None, None, d, ff), lambda i, e: (j, e, 0, 0)),
                  pl.BlockSpec((None, None, ff, d), lambda i, e: (j, e, 0, 0))]
        + [_full((1, d))] * final_norm,
        out_specs=row,
        out_shape=jax.ShapeDtypeStruct((rows, d), F32),
        scratch_shapes=[pltpu.VMEM((tm, d), BF16),
                        pltpu.VMEM((tm, ROUTER_LANES), F32),
                        pltpu.VMEM((tm, ROUTER_LANES), F32),
                        pltpu.VMEM((ROUTER_LANES, tm), F32),
                        pltpu.VMEM((tm, tm), BF16),
                        pltpu.SMEM((N_EXPERTS,), jnp.int32)],
        compiler_params=_params(2),
        name="moe",
    )(x, g, wr_pad, br_pad, wg, wu, wd, *([final_g] * final_norm))


def _trunk(x, p, past):
    b, l, d = x.shape
    depth = p['w_in'].shape[0]
    rows = b * l
    xr = x.reshape(rows, d)
    prompt = past is None
    new = {'k': [], 'v': [], 're': [], 'im': [], 'gv': []}
    if not prompt:
        w = past[0].shape[2]
        kc_t = past[0].transpose(0, 1, 3, 4, 2).reshape(depth, b, KV_W, w)
        vc_t = past[1].transpose(0, 1, 3, 4, 2).reshape(depth, b, KV_W, w)
        k_out = jnp.zeros_like(kc_t)
        v_out = jnp.zeros_like(vc_t)
    for i in range(depth):
        g_mix = p['norm_mix_g'][i].reshape(1, d)
        gv = p['gmlp_v_norm_g'][i].reshape(1, GM_W)
        q, k, v, u, gu, gvn = _inproj(xr, g_mix, p['w_in_bf16'], gv, i)
        u = u.reshape(b, l, SSM_W)
        sinks = p['attn_sinks'][i]
        sp = _ssm_layer_params(p, i)
        if prompt:
            o_attn = _attn_prompt(q, k, v, sinks, batch=b, seq=l)
            nw = min(WINDOW, l)
            k_win = k.reshape(b, l, N_KV_HEADS, HEAD_DIM)[:, l - nw:]
            v_win = v.reshape(b, l, N_KV_HEADS, HEAD_DIM)[:, l - nw:]
            h0 = jnp.zeros((b, 2 * SSM_FLAT), F32)
            o_ssm, s_fin = _ssm(u, h0, sp, tc=CHUNK)
        else:
            o_attn, k_out, v_out = _attn_sample(q, k, v, kc_t, vc_t, k_out, v_out, sinks, i,
                                                batch=b, t=l)
            h0 = jnp.concatenate([past[2][i].reshape(b, SSM_FLAT),
                                  past[3][i].reshape(b, SSM_FLAT)], axis=1)
            o_ssm, s_fin = _ssm(u, h0, sp, tc=l)
            new['gv'].append(gvn.reshape(b, l, GM_HEADS, GM_HEAD_DIM))
        ws, bs = _gmlp_layer_params(p, i, min(l, CHUNK))
        xr = _outproj(xr, o_attn, o_ssm.reshape(rows, SSM_W), gu, gvn, ws, bs,
                      p['mix_out_norm_g'][i].reshape(1, d), p['w_out_bf16'], i,
                      chunk=min(l, CHUNK))
        g_ffn = p['norm_ffn_g'][i].reshape(1, d)
        final_g = p['final_norm_g'].reshape(1, d) if i == depth - 1 else None
        j = i // 2
        if i % 2 == 0:
            xr = _ffn(xr, g_ffn, p['ffn_w_gate_bf16'], p['ffn_w_up_bf16'],
                      p['ffn_w_down_bf16'], j, final_g)
        else:
            wr = jnp.pad(p['moe_w_router'][j], ((0, 0), (0, ROUTER_LANES - N_EXPERTS)))
            br = jnp.pad(p['moe_b_router'][j], (0, ROUTER_LANES - N_EXPERTS),
                         constant_values=NEG_BIG).reshape(1, ROUTER_LANES)
            xr = _moe(xr, g_ffn, wr, br, p['moe_w_gate_bf16'], p['moe_w_up_bf16'],
                      p['moe_w_down_bf16'], j, final_g)
        if prompt:
            new['k'].append(k_win)
            new['v'].append(v_win)
        new['re'].append(s_fin[:, :SSM_FLAT].reshape(b, SSM_GROUPS, SSM_STATE))
        new['im'].append(s_fin[:, SSM_FLAT:].reshape(b, SSM_GROUPS, SSM_STATE))
    for name in ('re', 'im', 'gv') + (('k', 'v') if prompt else ()):
        new[name] = jnp.stack(new[name]) if new[name] else None
    if not prompt:
        back = lambda c: c.reshape(depth, b, N_KV_HEADS, HEAD_DIM, w).transpose(0, 1, 4, 2, 3)
        new['k'], new['v'] = back(k_out), back(v_out)
    return xr.reshape(b, l, d), new


def kernel(x_prompt, x_sample, cache_k_win, cache_v_win, state_ssm_re, state_ssm_im,
           norm_mix_g, w_in, attn_sinks, ssm_a_re, ssm_a_im, ssm_log_dt, ssm_b_re, ssm_b_im,
           ssm_c_re, ssm_c_im, ssm_d, ssm_w_glu, ssm_b_glu, gmlp_v_norm_g, gmlp_w_s, gmlp_b_s,
           mix_out_norm_g, w_out, norm_ffn_g, ffn_w_gate, ffn_w_up, ffn_w_down,
           moe_w_router, moe_b_router, moe_w_gate, moe_w_up, moe_w_down, final_norm_g):
    params = {
        'norm_mix_g': norm_mix_g, 'w_in': w_in, 'attn_sinks': attn_sinks,
        'ssm_a_re': ssm_a_re, 'ssm_a_im': ssm_a_im, 'ssm_log_dt': ssm_log_dt,
        'ssm_b_re': ssm_b_re, 'ssm_b_im': ssm_b_im, 'ssm_c_re': ssm_c_re, 'ssm_c_im': ssm_c_im,
        'ssm_d': ssm_d, 'ssm_w_glu': ssm_w_glu, 'ssm_b_glu': ssm_b_glu,
        'gmlp_v_norm_g': gmlp_v_norm_g, 'gmlp_w_s': gmlp_w_s, 'gmlp_b_s': gmlp_b_s,
        'mix_out_norm_g': mix_out_norm_g, 'w_out': w_out, 'norm_ffn_g': norm_ffn_g,
        'ffn_w_gate': ffn_w_gate, 'ffn_w_up': ffn_w_up, 'ffn_w_down': ffn_w_down,
        'moe_w_router': moe_w_router, 'moe_b_router': moe_b_router,
        'moe_w_gate': moe_w_gate, 'moe_w_up': moe_w_up, 'moe_w_down': moe_w_down,
        'final_norm_g': final_norm_g,
    }
    for name in ('w_in', 'w_out', 'ffn_w_gate', 'ffn_w_up', 'ffn_w_down',
                 'moe_w_gate', 'moe_w_up', 'moe_w_down'):
        params[name + '_bf16'] = params[name].astype(BF16)
    y_p, st_p = _trunk(x_prompt, params, None)
    y_s, st_s = _trunk(x_sample, params,
                       (cache_k_win, cache_v_win, state_ssm_re, state_ssm_im))
    return (y_p, y_s,
            st_p['k'], st_p['v'], st_p['re'], st_p['im'],
            st_s['k'], st_s['v'], st_s['re'], st_s['im'], st_s['gv'])
```

```python
import functools
import math

import jax
import jax.numpy as jnp
from jax import lax
from jax.experimental import pallas as pl
from jax.experimental.pallas import tpu as pltpu

F32 = jnp.float32
BF16 = jnp.bfloat16

EPS = 1e-6
HEAD_DIM = 64
N_HEADS = 8
N_KV_HEADS = 2
GQA_GROUP = N_HEADS // N_KV_HEADS
ATTN_W = N_HEADS * HEAD_DIM
KV_W = N_KV_HEADS * HEAD_DIM
WINDOW = 128
ATTN_SCALE = 1.0 / math.sqrt(HEAD_DIM)
LOG2E = math.log2(math.e)
SSM_W = 256
SSM_GROUP_CH = 16
SSM_GROUPS = 16
SSM_STATE = 64
SSM_FLAT = SSM_GROUPS * SSM_STATE
GM_W = 256
GM_HEADS = 4
GM_HEAD_DIM = 64
CHUNK = 128
Q_END = ATTN_W
K_END = Q_END + KV_W
V_END = K_END + KV_W
S_END = V_END + SSM_W
IN_COLS = S_END + 2 * GM_W
N_EXPERTS = 8
ROUTER_LANES = 128
NEG_BIG = -1e30

VMEM_LIMIT_BYTES = 56 * 1024 * 1024
ROW_TILE = 512
SUB_ROWS = 128
OUTPROJ_PASS_ROWS = 256
ATTN_Q_BLOCKS = 4
MOE_ROW_TILE = 1024
MOE_CHUNK = 128
MOE_SCATTER_ROWS = 256
SCAN_LANES = 512


def _params(n_axes):
    return pltpu.CompilerParams(
        dimension_semantics=("arbitrary",) * n_axes,
        vmem_limit_bytes=VMEM_LIMIT_BYTES)


def _rms(x, g):
    return x * lax.rsqrt(jnp.mean(x * x, axis=-1, keepdims=True) + EPS) * g


def _bdot(a, b):
    return jnp.dot(a.astype(BF16), b.astype(BF16), preferred_element_type=F32)


def _full(shape):
    return pl.BlockSpec(shape, lambda *_: (0,) * len(shape))


def _group_mean(a, n_groups, width):
    lane = lax.broadcasted_iota(jnp.int32, (1, n_groups * width), 1)
    out = jnp.zeros_like(a)
    for h in range(n_groups):
        m = (lane >= h * width) & (lane < (h + 1) * width)
        s = jnp.sum(jnp.where(m, a, 0.0), axis=-1, keepdims=True) * (1.0 / width)
        out = jnp.where(m, s, out)
    return out


def _inproj_body(x_ref, g_ref, w_ref, gv_ref, q_ref, k_ref, v_ref, u_ref, gu_ref, gvn_ref):
    for r0 in range(0, x_ref.shape[0], SUB_ROWS):
        rs = slice(r0, r0 + SUB_ROWS)
        h = _rms(x_ref[rs, :], g_ref[...])
        z = jnp.dot(h.astype(BF16), w_ref[...], preferred_element_type=F32)
        q_ref[rs, :] = (z[:, :Q_END] * (ATTN_SCALE * LOG2E)).astype(BF16)
        k_ref[rs, :] = z[:, Q_END:K_END]
        v_ref[rs, :] = z[:, K_END:V_END]
        u_ref[rs, :] = z[:, V_END:S_END]
        g = jax.nn.gelu(z[:, S_END:])
        gu_ref[rs, :] = g[:, :GM_W]
        vv = g[:, GM_W:]
        mu = _group_mean(vv, GM_HEADS, GM_HEAD_DIM)
        var = _group_mean(jnp.square(vv - mu), GM_HEADS, GM_HEAD_DIM)
        gvn_ref[rs, :] = (vv - mu) * lax.rsqrt(var + EPS) * gv_ref[...]


def _layer(shape, j):
    return pl.BlockSpec((None,) + tuple(shape), lambda *_: (j,) + (0,) * len(shape))


def _inproj(x, g, w_bf16, gv, j):
    rows, d = x.shape
    tm = min(ROW_TILE, rows)
    nt = rows // tm
    row = lambda w: pl.BlockSpec((tm, w), lambda i: (i, 0))
    return pl.pallas_call(
        _inproj_body,
        grid=(nt,),
        in_specs=[row(d), _full((1, d)), _layer((d, IN_COLS), j), _full((1, GM_W))],
        out_specs=[row(ATTN_W), row(KV_W), row(KV_W), row(SSM_W), row(GM_W), row(GM_W)],
        out_shape=[jax.ShapeDtypeStruct((rows, ATTN_W), BF16),
                   jax.ShapeDtypeStruct((rows, KV_W), F32),
                   jax.ShapeDtypeStruct((rows, KV_W), F32),
                   jax.ShapeDtypeStruct((rows, SSM_W), F32),
                   jax.ShapeDtypeStruct((rows, GM_W), F32),
                   jax.ShapeDtypeStruct((rows, GM_W), F32)],
        compiler_params=_params(1),
        name="inproj",
    )(x, g, w_bf16, gv)


def _sink_softmax(s, mask, sink):
    s = jnp.where(mask, s, -jnp.inf)
    m = jnp.maximum(jnp.max(s, axis=-1, keepdims=True), sink)
    e = jnp.exp2(s - m)
    return e / (jnp.sum(e, axis=-1, keepdims=True) + jnp.exp2(sink - m))


def _head_pair_operands(x):
    low = lax.broadcasted_iota(jnp.int32, (1, KV_W), 1) < HEAD_DIM
    swapped = pltpu.roll(x, HEAD_DIM, axis=1)
    keep = lambda a, in_low: jnp.where(low == in_low, a, 0.0).astype(BF16)
    return ((keep(x, True), keep(swapped, False)),
            (keep(swapped, True), keep(x, False)))


def _attn_prompt_body(sink_ref, q_ref, kc_ref, kp_ref, vc_ref, vp_ref, o_ref, *, nq):
    n = pl.program_id(1)
    qi = lax.broadcasted_iota(jnp.int32, (WINDOW, 2 * WINDOW), 0)
    si = lax.broadcasted_iota(jnp.int32, (WINDOW, 2 * WINDOW), 1)
    dist = WINDOW + qi - si
    band = (dist >= 0) & (dist < WINDOW)
    first = band & ((n > 0) | (si >= WINDOW))
    k_ops = _head_pair_operands(jnp.concatenate([kp_ref[...], kc_ref[...]], axis=0))
    v_ops = _head_pair_operands(jnp.concatenate([vp_ref[...], vc_ref[...]], axis=0))
    pair_w = 2 * HEAD_DIM
    for j in range(nq):
        mask = first if j == 0 else band
        q_rows = slice(j * WINDOW, (j + 1) * WINDOW)
        kv_rows = slice(j * WINDOW, (j + 2) * WINDOW)
        for hp in range(N_HEADS // 2):
            kh = (2 * hp) // GQA_GROUP
            qp = q_ref[q_rows, hp * pair_w:(hp + 1) * pair_w]
            acc = None
            for par in range(2):
                sink = sink_ref[2 * hp + par] * LOG2E
                s = lax.dot_general(qp, k_ops[kh][par][kv_rows], (((1,), (1,)), ((), ())),
                                    preferred_element_type=F32)
                s = jnp.where(mask, s, -jnp.inf)
                m = jnp.maximum(jnp.max(s, axis=-1, keepdims=True), sink)
                e = jnp.exp2(s - m)
                den = jnp.sum(e, axis=-1, keepdims=True) + jnp.exp2(sink - m)
                pv = jnp.dot(e.astype(BF16), v_ops[kh][par][kv_rows],
                             preferred_element_type=F32)
                pv = pv * (1.0 / den)
                acc = pv if acc is None else acc + pv
            o_ref[q_rows, hp * pair_w:(hp + 1) * pair_w] = acc


def _attn_prompt(q, k, v, sinks, *, batch, seq):
    nq = ATTN_Q_BLOCKS
    nb = seq // WINDOW
    steps = nb // nq
    cur = lambda w: pl.BlockSpec((nq * WINDOW, w), lambda b, n: (b * steps + n, 0))
    prev = lambda w: pl.BlockSpec(
        (WINDOW, w), lambda b, n: (b * nb + jnp.maximum(n * nq - 1, 0), 0))
    return pl.pallas_call(
        functools.partial(_attn_prompt_body, nq=nq),
        grid=(batch, steps),
        in_specs=[pl.BlockSpec(memory_space=pltpu.SMEM),
                  cur(ATTN_W), cur(KV_W), prev(KV_W), cur(KV_W), prev(KV_W)],
        out_specs=cur(ATTN_W),
        out_shape=jax.ShapeDtypeStruct((batch * seq, ATTN_W), F32),
        compiler_params=_params(2),
        name="attn_prompt",
    )(sinks, q, k, k, v, v)


def _attn_sample_body(sink_ref, q_ref, k_ref, v_ref, kc_ref, vc_ref, kin_ref, vin_ref,
                      o_ref, ko_ref, vo_ref, *, bb, t):
    del kin_ref, vin_ref
    w = kc_ref.shape[-1]
    pair_w = 2 * HEAD_DIM
    lane = lax.broadcasted_iota(jnp.int32, (1, w), 1)
    new_col = lane < t
    k_new_t = k_ref[...].T
    v_new_t = v_ref[...].T
    qf = q_ref[...].astype(F32)
    unit = 2 * t
    upper = lax.broadcasted_iota(jnp.int32, (unit, 1), 0) < t
    zeros_half = jnp.zeros((HEAD_DIM, 2 * w), BF16)
    placed = lambda a, par: jnp.concatenate((a, zeros_half) if par == 0 else (zeros_half, a), axis=0)
    scores, sinks, values = [], [], []
    for b in range(bb):
        shift = (w - b * t) % w
        k_cols = jnp.where(new_col, pltpu.roll(k_new_t, shift, axis=1), 0.0)
        v_cols = jnp.where(new_col, pltpu.roll(v_new_t, shift, axis=1), 0.0)
        k_old = kc_ref[b]
        v_old = vc_ref[b]
        ko_ref[b] = pltpu.roll(jnp.where(new_col, k_cols, k_old), w - t, axis=1)
        vo_ref[b] = pltpu.roll(jnp.where(new_col, v_cols, v_old), w - t, axis=1)
        k_all = jnp.concatenate([k_old, k_cols], axis=1).astype(BF16)
        v_all = jnp.concatenate([v_old, v_cols], axis=1).astype(BF16)
        for kh in range(N_KV_HEADS):
            kv_sl = slice(kh * HEAD_DIM, (kh + 1) * HEAD_DIM)
            q2 = jnp.concatenate(
                [qf[b * t:(b + 1) * t, (2 * kh + c) * pair_w:(2 * kh + c + 1) * pair_w]
                 for c in range(2)], axis=0).astype(BF16)
            for par in range(2):
                h_a = GQA_GROUP * kh + par
                sinks.append(jnp.where(upper, sink_ref[h_a], sink_ref[h_a + 2]) * LOG2E)
                scores.append(jnp.dot(q2, placed(k_all[kv_sl], par),
                                      preferred_element_type=F32))
                values.append(placed(v_all[kv_sl], par))
    s = jnp.concatenate(scores, axis=0)
    ti = lax.broadcasted_iota(jnp.int32, s.shape, 0) % t
    si = lax.broadcasted_iota(jnp.int32, s.shape, 1)
    dist = w + ti - si
    p = _sink_softmax(s, (dist >= 0) & (dist < WINDOW), jnp.concatenate(sinks, axis=0)).astype(BF16)
    for b in range(bb):
        for kh in range(N_KV_HEADS):
            acc = None
            for par in range(2):
                u = (b * N_KV_HEADS + kh) * 2 + par
                pv = lax.dot_general(p[u * unit:(u + 1) * unit], values[u],
                                     (((1,), (1,)), ((), ())), preferred_element_type=F32)
                acc = pv if acc is None else acc + pv
            for c in range(2):
                o_ref[b * t:(b + 1) * t, (2 * kh + c) * pair_w:(2 * kh + c + 1) * pair_w] = (
                    acc[c * t:(c + 1) * t])


def _attn_sample(q, k, v, kc_t, vc_t, k_out, v_out, sinks, layer, *, batch, t):
    w = kc_t.shape[-1]
    bb = w // t
    row = lambda c: pl.BlockSpec((bb * t, c), lambda i: (i, 0))
    buf = pl.BlockSpec((None, bb, KV_W, w), lambda i: (layer, i, 0, 0))
    anywhere = pl.BlockSpec(memory_space=pl.ANY)
    return pl.pallas_call(
        functools.partial(_attn_sample_body, bb=bb, t=t),
        grid=(batch // bb,),
        in_specs=[pl.BlockSpec(memory_space=pltpu.SMEM),
                  row(ATTN_W), row(KV_W), row(KV_W), buf, buf, anywhere, anywhere],
        out_specs=[row(ATTN_W), buf, buf],
        out_shape=[jax.ShapeDtypeStruct((batch * t, ATTN_W), F32),
                   jax.ShapeDtypeStruct(k_out.shape, F32),
                   jax.ShapeDtypeStruct(v_out.shape, F32)],
        input_output_aliases={6: 1, 7: 2},
        compiler_params=_params(1),
        name="attn_sample",
    )(sinks, q, k, v, kc_t, vc_t, k_out, v_out)


def _ssm_body(u_ref, h0_ref, are_ref, aim_ref, ldt_ref, bre_ref, bim_ref, cre_ref, cim_ref,
              d_ref, wglu_ref, bglu_ref, o_ref, s_ref, abar_scr, bbar_scr, h_scr, *x_scrs,
              r, tc):
    c = pl.program_id(0)

    @pl.when(c == 0)
    def _():
        ar = are_ref[...]
        ai = aim_ref[...]
        dt = jnp.exp(ldt_ref[...])
        decay = jnp.exp(dt * ar)
        abr = decay * jnp.cos(dt * ai)
        abi = decay * jnp.sin(dt * ai)
        den = ar * ar + ai * ai
        nr = abr - 1.0
        fr = (nr * ar + abi * ai) / den
        fi = (abi * ar - nr * ai) / den
        abar_scr[0:1, :] = abr
        abar_scr[1:2, :] = abi
        br = bre_ref[...]
        bi = bim_ref[...]
        bbar_scr[:, :SSM_FLAT] = (fr * br - fi * bi).astype(BF16)
        bbar_scr[:, SSM_FLAT:] = (fr * bi + fi * br).astype(BF16)
        h_scr[...] = h0_ref[...]

    u = jnp.swapaxes(u_ref[...], 0, 1).reshape(tc * r, SSM_W)
    ub = u.astype(BF16)

    y = None
    for lc in range(SSM_FLAT // SCAN_LANES):
        lanes = slice(lc * SCAN_LANES, (lc + 1) * SCAN_LANES)
        re_sl = lanes
        im_sl = slice(SSM_FLAT + lc * SCAN_LANES, SSM_FLAT + (lc + 1) * SCAN_LANES)
        xr_scr, xi_scr = x_scrs[2 * lc], x_scrs[2 * lc + 1]
        xr_scr[...] = jnp.dot(ub, bbar_scr[:, re_sl], preferred_element_type=F32)
        xi_scr[...] = jnp.dot(ub, bbar_scr[:, im_sl], preferred_element_type=F32)
        ar = jnp.broadcast_to(abar_scr[0:1, lanes], (8, SCAN_LANES))
        ai = jnp.broadcast_to(abar_scr[1:2, lanes], (8, SCAN_LANES))
        for rg in range(r // 8):
            rg_sl = slice(rg * 8, (rg + 1) * 8)
            hr, hi = h_scr[rg_sl, re_sl], h_scr[rg_sl, im_sl]
            for t in range(tc):
                rows = slice(t * r + rg * 8, t * r + rg * 8 + 8)
                hr, hi = (ar * hr - ai * hi + xr_scr[rows, :],
                          ar * hi + ai * hr + xi_scr[rows, :])
                xr_scr[rows, :] = hr
                xi_scr[rows, :] = hi
            h_scr[rg_sl, re_sl] = hr
            h_scr[rg_sl, im_sl] = hi
        part = (jnp.dot(xr_scr[...].astype(BF16), cre_ref[lanes, :], preferred_element_type=F32)
                - jnp.dot(xi_scr[...].astype(BF16), cim_ref[lanes, :],
                          preferred_element_type=F32))
        y = part if y is None else y + part
    y = jax.nn.gelu(y + d_ref[...] * u)
    gl = jnp.dot(y.astype(BF16), wglu_ref[...], preferred_element_type=F32) + bglu_ref[...]
    o_ref[...] = jnp.swapaxes((y * jax.nn.sigmoid(gl)).reshape(tc, r, SSM_W), 0, 1)

    @pl.when(c == pl.num_programs(0) - 1)
    def _():
        s_ref[...] = h_scr[...]


def _ssm(u, h0, sp, *, tc):
    r, steps, _ = u.shape
    blk = tc * r
    seq_blk = pl.BlockSpec((r, tc, SSM_W), lambda c: (0, c, 0))
    return pl.pallas_call(
        functools.partial(_ssm_body, r=r, tc=tc),
        grid=(steps // tc,),
        in_specs=[seq_blk,
                  _full((r, 2 * SSM_FLAT)),
                  _full((1, SSM_FLAT)), _full((1, SSM_FLAT)), _full((1, SSM_FLAT)),
                  _full((SSM_W, SSM_FLAT)), _full((SSM_W, SSM_FLAT)),
                  _full((SSM_FLAT, SSM_W)), _full((SSM_FLAT, SSM_W)),
                  _full((1, SSM_W)), _full((SSM_W, SSM_W)), _full((1, SSM_W))],
        out_specs=[seq_blk, _full((r, 2 * SSM_FLAT))],
        out_shape=[jax.ShapeDtypeStruct((r, steps, SSM_W), F32),
                   jax.ShapeDtypeStruct((r, 2 * SSM_FLAT), F32)],
        scratch_shapes=[pltpu.VMEM((2, SSM_FLAT), F32),
                        pltpu.VMEM((SSM_W, 2 * SSM_FLAT), BF16),
                        pltpu.VMEM((r, 2 * SSM_FLAT), F32)]
        + [pltpu.VMEM((blk, SCAN_LANES), F32)] * (2 * SSM_FLAT // SCAN_LANES),
        compiler_params=_params(1),
        name="ssm",
    )(u, h0, sp['a_re'], sp['a_im'], sp['log_dt'], sp['b_re'], sp['b_im'],
      sp['c_re'], sp['c_im'], sp['d'], sp['w_glu'], sp['b_glu'])


def _ssm_layer_params(p, i):
    eye = jnp.eye(SSM_GROUPS, dtype=F32)

    def b_blockdiag(b):
        return jnp.einsum('gpc,gh->gchp', b, eye).reshape(SSM_W, SSM_FLAT)

    def c_blockdiag(c):
        return jnp.einsum('gcp,gh->gphc', c, eye).reshape(SSM_FLAT, SSM_W)

    return {
        'a_re': p['ssm_a_re'][i].reshape(1, SSM_FLAT),
        'a_im': p['ssm_a_im'][i].reshape(1, SSM_FLAT),
        'log_dt': jnp.repeat(p['ssm_log_dt'][i], SSM_STATE).reshape(1, SSM_FLAT),
        'b_re': b_blockdiag(p['ssm_b_re'][i]),
        'b_im': b_blockdiag(p['ssm_b_im'][i]),
        'c_re': c_blockdiag(p['ssm_c_re'][i]).astype(BF16),
        'c_im': c_blockdiag(p['ssm_c_im'][i]).astype(BF16),
        'd': p['ssm_d'][i].reshape(1, SSM_W),
        'w_glu': p['ssm_w_glu'][i].astype(BF16),
        'b_glu': p['ssm_b_glu'][i].reshape(1, SSM_W),
    }


def _outproj_body(x_ref, oa_ref, os_ref, gu_ref, gvn_ref, ws_ref, bs_ref, go_ref, w_ref, o_ref,
                  *, chunk):
    tm = x_ref.shape[0]
    ri = lax.broadcasted_iota(jnp.int32, (CHUNK, CHUNK), 0)
    ci = lax.broadcasted_iota(jnp.int32, (CHUNK, CHUNK), 1)
    causal = (ri // chunk == ci // chunk) & (ci <= ri)
    lane = lax.broadcasted_iota(jnp.int32, (1, GM_W), 1)
    w_heads = [jnp.where(causal, ws_ref[h], 0.0).astype(BF16) for h in range(GM_HEADS)]
    go = go_ref[...]
    per_pass = OUTPROJ_PASS_ROWS // CHUNK
    for r0 in range(0, tm, OUTPROJ_PASS_ROWS):
        rs = slice(r0, r0 + OUTPROJ_PASS_ROWS)
        zs = []
        for cblk in range(per_pass):
            vn = gvn_ref[r0 + cblk * CHUNK:r0 + (cblk + 1) * CHUNK, :].astype(BF16)
            z = bs_ref[...]
            for h in range(GM_HEADS):
                z_h = jnp.dot(w_heads[h], vn, preferred_element_type=F32)
                head = (lane >= h * GM_HEAD_DIM) & (lane < (h + 1) * GM_HEAD_DIM)
                z = z + jnp.where(head, z_h, 0.0)
            zs.append(z)
        o = jnp.concatenate([
            _rms(oa_ref[rs, :], go[:, :ATTN_W]),
            _rms(os_ref[rs, :], go[:, ATTN_W:ATTN_W + SSM_W]),
            _rms(gu_ref[rs, :] * jnp.concatenate(zs, axis=0), go[:, ATTN_W + SSM_W:])], axis=-1)
        o_ref[rs, :] = x_ref[rs, :] + jnp.dot(o.astype(BF16), w_ref[...],
                                              preferred_element_type=F32)


def _outproj(x, o_attn, o_ssm, gu, gvn, ws, bs, g_out, w_bf16, j, *, chunk):
    rows, d = x.shape
    tm = min(ROW_TILE, rows)
    row = lambda w: pl.BlockSpec((tm, w), lambda i: (i, 0))
    return pl.pallas_call(
        functools.partial(_outproj_body, chunk=chunk),
        grid=(rows // tm,),
        in_specs=[row(d), row(ATTN_W), row(SSM_W), row(GM_W), row(GM_W),
                  _full((GM_HEADS, CHUNK, CHUNK)), _full((CHUNK, GM_W)),
                  _full((1, d)), _layer((d, d), j)],
        out_specs=row(d),
        out_shape=jax.ShapeDtypeStruct((rows, d), F32),
        compiler_params=_params(1),
        name="outproj",
    )(x, o_attn, o_ssm, gu, gvn, ws, bs, g_out, w_bf16)


def _gmlp_layer_params(p, i, chunk):
    rep = CHUNK // chunk
    ws = jnp.tile(p['gmlp_w_s'][i][:, :chunk, :chunk], (1, rep, rep))
    bs = jnp.tile(p['gmlp_b_s'][i][:, :chunk], (1, rep))
    bs = jnp.repeat(bs.T, GM_HEAD_DIM, axis=1)
    return ws, bs


def _swiglu_cols(h, wg_ref, wu_ref, wd_ref, n_chunks):
    ff = wg_ref.shape[-1]
    fc = ff // n_chunks
    acc = None
    for j in range(n_chunks):
        sl = slice(j * fc, (j + 1) * fc)
        a = jnp.dot(h, wg_ref[:, sl], preferred_element_type=F32)
        b = jnp.dot(h, wu_ref[:, sl], preferred_element_type=F32)
        m = (jax.nn.silu(a) * b).astype(BF16)
        y = jnp.dot(m, wd_ref[sl, :], preferred_element_type=F32)
        acc = y if acc is None else acc + y
    return acc


def _ffn_body(x_ref, g_ref, wg_ref, wu_ref, wd_ref, *rest, final_norm):
    gf_ref = rest[0] if final_norm else None
    o_ref = rest[-1]
    x = x_ref[...]
    h = _rms(x, g_ref[...]).astype(BF16)
    y = x + _swiglu_cols(h, wg_ref, wu_ref, wd_ref, 2)
    o_ref[...] = _rms(y, gf_ref[...]) if final_norm else y


def _ffn(x, g, wg, wu, wd, j, final_g):
    rows, d = x.shape
    ff = wg.shape[-1]
    tm = min(ROW_TILE, rows)
    row = pl.BlockSpec((tm, d), lambda i: (i, 0))
    once = lambda shape: pl.BlockSpec((None,) + shape, lambda i: (j, 0, 0),
                                      pipeline_mode=pl.Buffered(1))
    final_norm = final_g is not None
    return pl.pallas_call(
        functools.partial(_ffn_body, final_norm=final_norm),
        grid=(rows // tm,),
        in_specs=[row, _full((1, d)), once((d, ff)), once((d, ff)), once((ff, d))]
        + [_full((1, d))] * final_norm,
        out_specs=row,
        out_shape=jax.ShapeDtypeStruct((rows, d), F32),
        compiler_params=_params(1),
        name="ffn",
    )(x, g, wg, wu, wd, *([final_g] * final_norm))


def _split_bf16(x):
    hi = x.astype(BF16)
    lo = (x - hi.astype(F32)).astype(BF16)
    return hi, lo


def _router_comb(hf, wr_ref, br_ref):
    h_hi, h_lo = _split_bf16(hf)
    w_hi, w_lo = _split_bf16(wr_ref[...])
    dot = lambda a, b: jnp.dot(a, b, preferred_element_type=F32)
    logits = dot(h_hi, w_hi) + (dot(h_hi, w_lo) + dot(h_lo, w_hi)) + br_ref[...]
    lane = lax.broadcasted_iota(jnp.int32, logits.shape, 1).astype(F32)
    far = float(ROUTER_LANES)
    m1 = jnp.max(logits, axis=-1, keepdims=True)
    i1 = jnp.min(jnp.where(logits == m1, lane, far), axis=-1, keepdims=True)
    rest = jnp.where(lane == i1, -jnp.inf, logits)
    m2 = jnp.max(rest, axis=-1, keepdims=True)
    i2 = jnp.min(jnp.where(rest == m2, lane, far), axis=-1, keepdims=True)
    e2 = jnp.exp(m2 - m1)
    den = 1.0 + e2
    comb = jnp.where(lane == i1, 1.0 / den, 0.0) + jnp.where(lane == i2, e2 / den, 0.0)
    sel = jnp.where((lane == i1) | (lane == i2), 1.0, 0.0)
    return comb, sel


def _moe_body(x_ref, g_ref, wr_ref, br_ref, wg_ref, wu_ref, wd_ref, *rest, final_norm):
    gf_ref = rest[0] if final_norm else None
    o_ref, h_scr, gate_scr, rank_scr, rank_t_scr, lower_scr, cnt_smem = rest[int(final_norm):]
    i = pl.program_id(0)
    e = pl.program_id(1)
    tm = x_ref.shape[0]

    @pl.when((i == 0) & (e == 0))
    def _():
        ri = lax.broadcasted_iota(jnp.int32, (tm, tm), 0)
        ci = lax.broadcasted_iota(jnp.int32, (tm, tm), 1)
        lower_scr[...] = jnp.where(ci < ri, 1.0, 0.0).astype(BF16)

    @pl.when(e == 0)
    def _():
        x = x_ref[...]
        hf = _rms(x, g_ref[...])
        h_scr[...] = hf.astype(BF16)
        comb, sel = _router_comb(hf, wr_ref, br_ref)
        gate_scr[...] = comb
        rank = jnp.dot(lower_scr[...], sel.astype(BF16), preferred_element_type=F32)
        rank = jnp.where(sel > 0.0, rank, -1.0)
        rank_scr[...] = rank
        rank_t_scr[...] = rank.T
        cnt = jnp.sum(sel, axis=0, keepdims=True)
        for k in range(N_EXPERTS):
            cnt_smem[k] = cnt[0, k].astype(jnp.int32)
        o_ref[...] = x

    lane = lax.broadcasted_iota(jnp.int32, (1, ROUTER_LANES), 1)
    pick = lane == e
    rank_col = jnp.sum(jnp.where(pick, rank_scr[...], 0.0), axis=-1, keepdims=True)
    gate_col = jnp.sum(jnp.where(pick, gate_scr[...], 0.0), axis=-1, keepdims=True)
    rank_row = rank_t_scr[pl.ds(e, 1), :]
    n_chunks = (cnt_smem[e] + (MOE_CHUNK - 1)) // MOE_CHUNK
    n_pairs = n_chunks // 2

    def run_chunk(first_slot, size):
        base = first_slot.astype(F32)
        slot_col = lax.broadcasted_iota(jnp.int32, (size, 1), 0).astype(F32) + base
        slot_row = lax.broadcasted_iota(jnp.int32, (1, size), 1).astype(F32) + base
        gather = jnp.where(rank_row == slot_col, 1.0, 0.0).astype(BF16)
        xe = jnp.dot(gather, h_scr[...], preferred_element_type=F32).astype(BF16)
        ye = _swiglu_cols(xe, wg_ref, wu_ref, wd_ref, 1).astype(BF16)
        sr = min(MOE_SCATTER_ROWS, tm)
        for rb in range(tm // sr):
            sl = slice(rb * sr, (rb + 1) * sr)
            scatter = jnp.where(rank_col[sl] == slot_row, 1.0, 0.0).astype(BF16)
            o_ref[sl, :] += gate_col[sl] * jnp.dot(scatter, ye, preferred_element_type=F32)

    def pair(c, carry):
        run_chunk(c * (2 * MOE_CHUNK), 2 * MOE_CHUNK)
        return carry

    lax.fori_loop(0, n_pairs, pair, 0)

    @pl.when(n_chunks % 2 == 1)
    def _():
        run_chunk(n_pairs * (2 * MOE_CHUNK), MOE_CHUNK)

    if final_norm:
        @pl.when(e == pl.num_programs(1) - 1)
        def _():
            o_ref[...] = _rms(o_ref[...], gf_ref[...])


def _moe(x, g, wr_pad, br_pad, wg, wu, wd, j, final_g):
    rows, d = x.shape
    _, n_e, _, ff = wg.shape
    tm = min(MOE_ROW_TILE, rows)
    row = pl.BlockSpec((tm, d), lambda i, e: (i, 0))
    final_norm = final_g is not None
    return pl.pallas_call(
        functools.partial(_moe_body, final_norm=final_norm),
        grid=(rows // tm, n_e),
        in_specs=[row, _full((1, d)), _full((d, ROUTER_LANES)), _full((1, ROUTER_LANES)),
                  pl.BlockSpec((None, None, d, ff), lambda i, e: (j, e, 0, 0)),
                  pl.BlockSpec((None, None, d, ff), lambda i, e: (j, e, 0, 0)),
                  pl.BlockSpec((None, None, ff, d), lambda i, e: (j, e, 0, 0))]
        + [_full((1, d))] * final_norm,
        out_specs=row,
        out_shape=jax.ShapeDtypeStruct((rows, d), F32),
        scratch_shapes=[pltpu.VMEM((tm, d), BF16),
                        pltpu.VMEM((tm, ROUTER_LANES), F32),
                        pltpu.VMEM((tm, ROUTER_LANES), F32),
                        pltpu.VMEM((ROUTER_LANES, tm), F32),
                        pltpu.VMEM((tm, tm), BF16),
                        pltpu.SMEM((N_EXPERTS,), jnp.int32)],
        compiler_params=_params(2),
        name="moe",
    )(x, g, wr_pad, br_pad, wg, wu, wd, *([final_g] * final_norm))


def _trunk(x, p, past):
    b, l, d = x.shape
    depth = p['w_in'].shape[0]
    rows = b * l
    xr = x.reshape(rows, d)
    prompt = past is None
    new = {'k': [], 'v': [], 're': [], 'im': [], 'gv': []}
    if not prompt:
        w = past[0].shape[2]
        kc_t = past[0].transpose(0, 1, 3, 4, 2).reshape(depth, b, KV_W, w)
        vc_t = past[1].transpose(0, 1, 3, 4, 2).reshape(depth, b, KV_W, w)
        k_out = jnp.zeros_like(kc_t)
        v_out = jnp.zeros_like(vc_t)
    for i in range(depth):
        g_mix = p['norm_mix_g'][i].reshape(1, d)
        gv = p['gmlp_v_norm_g'][i].reshape(1, GM_W)
        q, k, v, u, gu, gvn = _inproj(xr, g_mix, p['w_in_bf16'], gv, i)
        u = u.reshape(b, l, SSM_W)
        sinks = p['attn_sinks'][i]
        sp = _ssm_layer_params(p, i)
        if prompt:
            o_attn = _attn_prompt(q, k, v, sinks, batch=b, seq=l)
            nw = min(WINDOW, l)
            k_win = k.reshape(b, l, N_KV_HEADS, HEAD_DIM)[:, l - nw:]
            v_win = v.reshape(b, l, N_KV_HEADS, HEAD_DIM)[:, l - nw:]
            h0 = jnp.zeros((b, 2 * SSM_FLAT), F32)
            o_ssm, s_fin = _ssm(u, h0, sp, tc=CHUNK)
        else:
            o_attn, k_out, v_out = _attn_sample(q, k, v, kc_t, vc_t, k_out, v_out, sinks, i,
                                                batch=b, t=l)
            h0 = jnp.concatenate([past[2][i].reshape(b, SSM_FLAT),
                                  past[3][i].reshape(b, SSM_FLAT)], axis=1)
            o_ssm, s_fin = _ssm(u, h0, sp, tc=l)
            new['gv'].append(gvn.reshape(b, l, GM_HEADS, GM_HEAD_DIM))
        ws, bs = _gmlp_layer_params(p, i, min(l, CHUNK))
        xr = _outproj(xr, o_attn, o_ssm.reshape(rows, SSM_W), gu, gvn, ws, bs,
                      p['mix_out_norm_g'][i].reshape(1, d), p['w_out_bf16'], i,
                      chunk=min(l, CHUNK))
        g_ffn = p['norm_ffn_g'][i].reshape(1, d)
        final_g = p['final_norm_g'].reshape(1, d) if i == depth - 1 else None
        j = i // 2
        if i % 2 == 0:
            xr = _ffn(xr, g_ffn, p['ffn_w_gate_bf16'], p['ffn_w_up_bf16'],
                      p['ffn_w_down_bf16'], j, final_g)
        else:
            wr = jnp.pad(p['moe_w_router'][j], ((0, 0), (0, ROUTER_LANES - N_EXPERTS)))
            br = jnp.pad(p['moe_b_router'][j], (0, ROUTER_LANES - N_EXPERTS),
                         constant_values=NEG_BIG).reshape(1, ROUTER_LANES)
            xr = _moe(xr, g_ffn, wr, br, p['moe_w_gate_bf16'], p['moe_w_up_bf16'],
                      p['moe_w_down_bf16'], j, final_g)
        if prompt:
            new['k'].append(k_win)
            new['v'].append(v_win)
        new['re'].append(s_fin[:, :SSM_FLAT].reshape(b, SSM_GROUPS, SSM_STATE))
        new['im'].append(s_fin[:, SSM_FLAT:].reshape(b, SSM_GROUPS, SSM_STATE))
    for name in ('re', 'im', 'gv') + (('k', 'v') if prompt else ()):
        new[name] = jnp.stack(new[name]) if new[name] else None
    if not prompt:
        back = lambda c: c.reshape(depth, b, N_KV_HEADS, HEAD_DIM, w).transpose(0, 1, 4, 2, 3)
        new['k'], new['v'] = back(k_out), back(v_out)
    return xr.reshape(b, l, d), new


def kernel(x_prompt, x_sample, cache_k_win, cache_v_win, state_ssm_re, state_ssm_im,
           norm_mix_g, w_in, attn_sinks, ssm_a_re, ssm_a_im, ssm_log_dt, ssm_b_re, ssm_b_im,
           ssm_c_re, ssm_c_im, ssm_d, ssm_w_glu, ssm_b_glu, gmlp_v_norm_g, gmlp_w_s, gmlp_b_s,
           mix_out_norm_g, w_out, norm_ffn_g, ffn_w_gate, ffn_w_up, ffn_w_down,
           moe_w_router, moe_b_router, moe_w_gate, moe_w_up, moe_w_down, final_norm_g):
    params = {
        'norm_mix_g': norm_mix_g, 'w_in': w_in, 'attn_sinks': attn_sinks,
        'ssm_a_re': ssm_a_re, 'ssm_a_im': ssm_a_im, 'ssm_log_dt': ssm_log_dt,
        'ssm_b_re': ssm_b_re, 'ssm_b_im': ssm_b_im, 'ssm_c_re': ssm_c_re, 'ssm_c_im': ssm_c_im,
        'ssm_d': ssm_d, 'ssm_w_glu': ssm_w_glu, 'ssm_b_glu': ssm_b_glu,
        'gmlp_v_norm_g': gmlp_v_norm_g, 'gmlp_w_s': gmlp_w_s, 'gmlp_b_s': gmlp_b_s,
        'mix_out_norm_g': mix_out_norm_g, 'w_out': w_out, 'norm_ffn_g': norm_ffn_g,
        'ffn_w_gate': ffn_w_gate, 'ffn_w_up': ffn_w_up, 'ffn_w_down': ffn_w_down,
        'moe_w_router': moe_w_router, 'moe_b_router': moe_b_router,
        'moe_w_gate': moe_w_gate, 'moe_w_up': moe_w_up, 'moe_w_down': moe_w_down,
        'final_norm_g': final_norm_g,
    }
    for name in ('w_in', 'w_out', 'ffn_w_gate', 'ffn_w_up', 'ffn_w_down',
                 'moe_w_gate', 'moe_w_up', 'moe_w_down'):
        params[name + '_bf16'] = params[name].astype(BF16)
    y_p, st_p = _trunk(x_prompt, params, None)
    y_s, st_s = _trunk(x_sample, params,
                       (cache_k_win, cache_v_win, state_ssm_re, state_ssm_im))
    return (y_p, y_s,
            st_p['k'], st_p['v'], st_p['re'], st_p['im'],
            st_s['k'], st_s['v'], st_s['re'], st_s['im'], st_s['gv'])
```

```python
import functools
import math

import jax
import jax.numpy as jnp
from jax import lax
from jax.experimental import pallas as pl
from jax.experimental.pallas import tpu as pltpu

F32 = jnp.float32
BF16 = jnp.bfloat16

EPS = 1e-6
HEAD_DIM = 64
N_HEADS = 8
N_KV_HEADS = 2
GQA_GROUP = N_HEADS // N_KV_HEADS
ATTN_W = N_HEADS * HEAD_DIM
KV_W = N_KV_HEADS * HEAD_DIM
WINDOW = 128
ATTN_SCALE = 1.0 / math.sqrt(HEAD_DIM)
LOG2E = math.log2(math.e)
SSM_W = 256
SSM_GROUP_CH = 16
SSM_GROUPS = 16
SSM_STATE = 64
SSM_FLAT = SSM_GROUPS * SSM_STATE
GM_W = 256
GM_HEADS = 4
GM_HEAD_DIM = 64
CHUNK = 128
Q_END = ATTN_W
K_END = Q_END + KV_W
V_END = K_END + KV_W
S_END = V_END + SSM_W
IN_COLS = S_END + 2 * GM_W
N_EXPERTS = 8
ROUTER_LANES = 128
NEG_BIG = -1e30

VMEM_LIMIT_BYTES = 56 * 1024 * 1024
ROW_TILE = 512
SUB_ROWS = 128
OUTPROJ_PASS_ROWS = 256
ATTN_Q_BLOCKS = 4
MOE_ROW_TILE = 1024
MOE_CHUNK = 128
MOE_SCATTER_ROWS = 256
SCAN_LANES = 512


def _params(n_axes):
    return pltpu.CompilerParams(
        dimension_semantics=("arbitrary",) * n_axes,
        vmem_limit_bytes=VMEM_LIMIT_BYTES)


def _rms(x, g):
    return x * lax.rsqrt(jnp.mean(x * x, axis=-1, keepdims=True) + EPS) * g


def _bdot(a, b):
    return jnp.dot(a.astype(BF16), b.astype(BF16), preferred_element_type=F32)


def _full(shape):
    return pl.BlockSpec(shape, lambda *_: (0,) * len(shape))


def _group_mean(a, n_groups, width):
    lane = lax.broadcasted_iota(jnp.int32, (1, n_groups * width), 1)
    out = jnp.zeros_like(a)
    for h in range(n_groups):
        m = (lane >= h * width) & (lane < (h + 1) * width)
        s = jnp.sum(jnp.where(m, a, 0.0), axis=-1, keepdims=True) * (1.0 / width)
        out = jnp.where(m, s, out)
    return out


def _inproj_body(x_ref, g_ref, w_ref, gv_ref, q_ref, k_ref, v_ref, u_ref, gu_ref, gvn_ref):
    for r0 in range(0, x_ref.shape[0], SUB_ROWS):
        rs = slice(r0, r0 + SUB_ROWS)
        h = _rms(x_ref[rs, :], g_ref[...])
        z = jnp.dot(h.astype(BF16), w_ref[...], preferred_element_type=F32)
        q_ref[rs, :] = (z[:, :Q_END] * (ATTN_SCALE * LOG2E)).astype(BF16)
        k_ref[rs, :] = z[:, Q_END:K_END]
        v_ref[rs, :] = z[:, K_END:V_END]
        u_ref[rs, :] = z[:, V_END:S_END]
        g = jax.nn.gelu(z[:, S_END:])
        gu_ref[rs, :] = g[:, :GM_W]
        vv = g[:, GM_W:]
        mu = _group_mean(vv, GM_HEADS, GM_HEAD_DIM)
        var = _group_mean(jnp.square(vv - mu), GM_HEADS, GM_HEAD_DIM)
        gvn_ref[rs, :] = (vv - mu) * lax.rsqrt(var + EPS) * gv_ref[...]


def _layer(shape, j):
    return pl.BlockSpec((None,) + tuple(shape), lambda *_: (j,) + (0,) * len(shape))


def _inproj(x, g, w_bf16, gv, j):
    rows, d = x.shape
    tm = min(ROW_TILE, rows)
    nt = rows // tm
    row = lambda w: pl.BlockSpec((tm, w), lambda i: (i, 0))
    return pl.pallas_call(
        _inproj_body,
        grid=(nt,),
        in_specs=[row(d), _full((1, d)), _layer((d, IN_COLS), j), _full((1, GM_W))],
        out_specs=[row(ATTN_W), row(KV_W), row(KV_W), row(SSM_W), row(GM_W), row(GM_W)],
        out_shape=[jax.ShapeDtypeStruct((rows, ATTN_W), BF16),
                   jax.ShapeDtypeStruct((rows, KV_W), F32),
                   jax.ShapeDtypeStruct((rows, KV_W), F32),
                   jax.ShapeDtypeStruct((rows, SSM_W), F32),
                   jax.ShapeDtypeStruct((rows, GM_W), F32),
                   jax.ShapeDtypeStruct((rows, GM_W), F32)],
        compiler_params=_params(1),
        name="inproj",
    )(x, g, w_bf16, gv)


def _sink_softmax(s, mask, sink):
    s = jnp.where(mask, s, -jnp.inf)
    m = jnp.maximum(jnp.max(s, axis=-1, keepdims=True), sink)
    e = jnp.exp2(s - m)
    return e / (jnp.sum(e, axis=-1, keepdims=True) + jnp.exp2(sink - m))


def _head_pair_operands(x):
    low = lax.broadcasted_iota(jnp.int32, (1, KV_W), 1) < HEAD_DIM
    swapped = pltpu.roll(x, HEAD_DIM, axis=1)
    keep = lambda a, in_low: jnp.where(low == in_low, a, 0.0).astype(BF16)
    return ((keep(x, True), keep(swapped, False)),
            (keep(swapped, True), keep(x, False)))


def _attn_prompt_body(sink_ref, q_ref, kc_ref, kp_ref, vc_ref, vp_ref, o_ref, *, nq):
    n = pl.program_id(1)
    qi = lax.broadcasted_iota(jnp.int32, (WINDOW, 2 * WINDOW), 0)
    si = lax.broadcasted_iota(jnp.int32, (WINDOW, 2 * WINDOW), 1)
    dist = WINDOW + qi - si
    band = (dist >= 0) & (dist < WINDOW)
    first = band & ((n > 0) | (si >= WINDOW))
    k_ops = _head_pair_operands(jnp.concatenate([kp_ref[...], kc_ref[...]], axis=0))
    v_ops = _head_pair_operands(jnp.concatenate([vp_ref[...], vc_ref[...]], axis=0))
    pair_w = 2 * HEAD_DIM
    for j in range(nq):
        mask = first if j == 0 else band
        q_rows = slice(j * WINDOW, (j + 1) * WINDOW)
        kv_rows = slice(j * WINDOW, (j + 2) * WINDOW)
        for hp in range(N_HEADS // 2):
            kh = (2 * hp) // GQA_GROUP
            qp = q_ref[q_rows, hp * pair_w:(hp + 1) * pair_w]
            acc = None
            for par in range(2):
                sink = sink_ref[2 * hp + par] * LOG2E
                s = lax.dot_general(qp, k_ops[kh][par][kv_rows], (((1,), (1,)), ((), ())),
                                    preferred_element_type=F32)
                s = jnp.where(mask, s, -jnp.inf)
                m = jnp.maximum(jnp.max(s, axis=-1, keepdims=True), sink)
                e = jnp.exp2(s - m)
                den = jnp.sum(e, axis=-1, keepdims=True) + jnp.exp2(sink - m)
                pv = jnp.dot(e.astype(BF16), v_ops[kh][par][kv_rows],
                             preferred_element_type=F32)
                pv = pv * (1.0 / den)
                acc = pv if acc is None else acc + pv
            o_ref[q_rows, hp * pair_w:(hp + 1) * pair_w] = acc


def _attn_prompt(q, k, v, sinks, *, batch, seq):
    nq = ATTN_Q_BLOCKS
    nb = seq // WINDOW
    steps = nb // nq
    cur = lambda w: pl.BlockSpec((nq * WINDOW, w), lambda b, n: (b * steps + n, 0))
    prev = lambda w: pl.BlockSpec(
        (WINDOW, w), lambda b, n: (b * nb + jnp.maximum(n * nq - 1, 0), 0))
    return pl.pallas_call(
        functools.partial(_attn_prompt_body, nq=nq),
        grid=(batch, steps),
        in_specs=[pl.BlockSpec(memory_space=pltpu.SMEM),
                  cur(ATTN_W), cur(KV_W), prev(KV_W), cur(KV_W), prev(KV_W)],
        out_specs=cur(ATTN_W),
        out_shape=jax.ShapeDtypeStruct((batch * seq, ATTN_W), F32),
        compiler_params=_params(2),
        name="attn_prompt",
    )(sinks, q, k, k, v, v)


def _attn_sample_body(sink_ref, q_ref, k_ref, v_ref, kc_ref, vc_ref, kin_ref, vin_ref,
                      o_ref, ko_ref, vo_ref, *, bb, t):
    del kin_ref, vin_ref
    w = kc_ref.shape[-1]
    pair_w = 2 * HEAD_DIM
    lane = lax.broadcasted_iota(jnp.int32, (1, w), 1)
    new_col = lane < t
    k_new_t = k_ref[...].T
    v_new_t = v_ref[...].T
    qf = q_ref[...].astype(F32)
    unit = 2 * t
    upper = lax.broadcasted_iota(jnp.int32, (unit, 1), 0) < t
    zeros_half = jnp.zeros((HEAD_DIM, 2 * w), BF16)
    placed = lambda a, par: jnp.concatenate((a, zeros_half) if par == 0 else (zeros_half, a), axis=0)
    scores, sinks, values = [], [], []
    for b in range(bb):
        shift = (w - b * t) % w
        k_cols = jnp.where(new_col, pltpu.roll(k_new_t, shift, axis=1), 0.0)
        v_cols = jnp.where(new_col, pltpu.roll(v_new_t, shift, axis=1), 0.0)
        k_old = kc_ref[b]
        v_old = vc_ref[b]
        ko_ref[b] = pltpu.roll(jnp.where(new_col, k_cols, k_old), w - t, axis=1)
        vo_ref[b] = pltpu.roll(jnp.where(new_col, v_cols, v_old), w - t, axis=1)
        k_all = jnp.concatenate([k_old, k_cols], axis=1).astype(BF16)
        v_all = jnp.concatenate([v_old, v_cols], axis=1).astype(BF16)
        for kh in range(N_KV_HEADS):
            kv_sl = slice(kh * HEAD_DIM, (kh + 1) * HEAD_DIM)
            q2 = jnp.concatenate(
                [qf[b * t:(b + 1) * t, (2 * kh + c) * pair_w:(2 * kh + c + 1) * pair_w]
                 for c in range(2)], axis=0).astype(BF16)
            for par in range(2):
                h_a = GQA_GROUP * kh + par
                sinks.append(jnp.where(upper, sink_ref[h_a], sink_ref[h_a + 2]) * LOG2E)
                scores.append(jnp.dot(q2, placed(k_all[kv_sl], par),
                                      preferred_element_type=F32))
                values.append(placed(v_all[kv_sl], par))
    s = jnp.concatenate(scores, axis=0)
    ti = lax.broadcasted_iota(jnp.int32, s.shape, 0) % t
    si = lax.broadcasted_iota(jnp.int32, s.shape, 1)
    dist = w + ti - si
    p = _sink_softmax(s, (dist >= 0) & (dist < WINDOW), jnp.concatenate(sinks, axis=0)).astype(BF16)
    for b in range(bb):
        for kh in range(N_KV_HEADS):
            acc = None
            for par in range(2):
                u = (b * N_KV_HEADS + kh) * 2 + par
                pv = lax.dot_general(p[u * unit:(u + 1) * unit], values[u],
                                     (((1,), (1,)), ((), ())), preferred_element_type=F32)
                acc = pv if acc is None else acc + pv
            for c in range(2):
                o_ref[b * t:(b + 1) * t, (2 * kh + c) * pair_w:(2 * kh + c + 1) * pair_w] = (
                    acc[c * t:(c + 1) * t])


def _attn_sample(q, k, v, kc_t, vc_t, k_out, v_out, sinks, layer, *, batch, t):
    w = kc_t.shape[-1]
    bb = w // t
    row = lambda c: pl.BlockSpec((bb * t, c), lambda i: (i, 0))
    buf = pl.BlockSpec((None, bb, KV_W, w), lambda i: (layer, i, 0, 0))
    anywhere = pl.BlockSpec(memory_space=pl.ANY)
    return pl.pallas_call(
        functools.partial(_attn_sample_body, bb=bb, t=t),
        grid=(batch // bb,),
        in_specs=[pl.BlockSpec(memory_space=pltpu.SMEM),
                  row(ATTN_W), row(KV_W), row(KV_W), buf, buf, anywhere, anywhere],
        out_specs=[row(ATTN_W), buf, buf],
        out_shape=[jax.ShapeDtypeStruct((batch * t, ATTN_W), F32),
                   jax.ShapeDtypeStruct(k_out.shape, F32),
                   jax.ShapeDtypeStruct(v_out.shape, F32)],
        input_output_aliases={6: 1, 7: 2},
        compiler_params=_params(1),
        name="attn_sample",
    )(sinks, q, k, v, kc_t, vc_t, k_out, v_out)


def _ssm_body(u_ref, *refs, r, tc, has_h0):
    h0_refs, refs = refs[:2 * has_h0], refs[2 * has_h0:]
    (are_ref, aim_ref, ldt_ref, bre_ref, bim_ref, cre_ref, cim_ref, d_ref, wglu_ref, bglu_ref,
     o_ref, s_ref, abar_scr, bbar_scr, h_scr, *x_scrs) = refs
    c = pl.program_id(0)

    @pl.when(c == 0)
    def _():
        ar = are_ref[...]
        ai = aim_ref[...]
        dt = jnp.exp(ldt_ref[...])
        decay = jnp.exp(dt * ar)
        abr = decay * jnp.cos(dt * ai)
        abi = decay * jnp.sin(dt * ai)
        den = ar * ar + ai * ai
        nr = abr - 1.0
        fr = (nr * ar + abi * ai) / den
        fi = (abi * ar - nr * ai) / den
        abar_scr[0:1, :] = abr
        abar_scr[1:2, :] = abi
        br = bre_ref[...]
        bi = bim_ref[...]
        bbar_scr[:, :SSM_FLAT] = (fr * br - fi * bi).astype(BF16)
        bbar_scr[:, SSM_FLAT:] = (fr * bi + fi * br).astype(BF16)
        if has_h0:
            h_scr[:, :SSM_FLAT] = h0_refs[0][...].T
            h_scr[:, SSM_FLAT:] = h0_refs[1][...].T
        else:
            h_scr[...] = jnp.zeros_like(h_scr)

    u = jnp.swapaxes(u_ref[...], 0, 1).reshape(tc * r, SSM_W)
    ub = u.astype(BF16)

    y = None
    for lc in range(SSM_FLAT // SCAN_LANES):
        lanes = slice(lc * SCAN_LANES, (lc + 1) * SCAN_LANES)
        re_sl = lanes
        im_sl = slice(SSM_FLAT + lc * SCAN_LANES, SSM_FLAT + (lc + 1) * SCAN_LANES)
        xr_scr, xi_scr = x_scrs[2 * lc], x_scrs[2 * lc + 1]
        xr_scr[...] = jnp.dot(ub, bbar_scr[:, re_sl], preferred_element_type=F32)
        xi_scr[...] = jnp.dot(ub, bbar_scr[:, im_sl], preferred_element_type=F32)
        ar = jnp.broadcast_to(abar_scr[0:1, lanes], (8, SCAN_LANES))
        ai = jnp.broadcast_to(abar_scr[1:2, lanes], (8, SCAN_LANES))
        for rg in range(r // 8):
            rg_sl = slice(rg * 8, (rg + 1) * 8)
            hr, hi = h_scr[rg_sl, re_sl], h_scr[rg_sl, im_sl]
            for t in range(tc):
                rows = slice(t * r + rg * 8, t * r + rg * 8 + 8)
                hr, hi = (ar * hr - ai * hi + xr_scr[rows, :],
                          ar * hi + ai * hr + xi_scr[rows, :])
                xr_scr[rows, :] = hr
                xi_scr[rows, :] = hi
            h_scr[rg_sl, re_sl] = hr
            h_scr[rg_sl, im_sl] = hi
        part = (jnp.dot(xr_scr[...].astype(BF16), cre_ref[lanes, :], preferred_element_type=F32)
                - jnp.dot(xi_scr[...].astype(BF16), cim_ref[lanes, :],
                          preferred_element_type=F32))
        y = part if y is None else y + part
    y = jax.nn.gelu(y + d_ref[...] * u)
    gl = jnp.dot(y.astype(BF16), wglu_ref[...], preferred_element_type=F32) + bglu_ref[...]
    o_ref[...] = jnp.swapaxes((y * jax.nn.sigmoid(gl)).reshape(tc, r, SSM_W), 0, 1)

    @pl.when(c == pl.num_programs(0) - 1)
    def _():
        if has_h0:
            s_ref[0] = h_scr[:, :SSM_FLAT].T
            s_ref[1] = h_scr[:, SSM_FLAT:].T
        else:
            s_ref[...] = h_scr[...]


def _ssm(u, h0_t, sp, layer, *, tc):
    r, steps, _ = u.shape
    blk = tc * r
    has_h0 = h0_t is not None
    seq_blk = pl.BlockSpec((r, tc, SSM_W), lambda c: (0, c, 0))
    state_shape = (2, SSM_FLAT, r) if has_h0 else (r, 2 * SSM_FLAT)
    return pl.pallas_call(
        functools.partial(_ssm_body, r=r, tc=tc, has_h0=has_h0),
        grid=(steps // tc,),
        in_specs=[seq_blk]
        + [_layer((SSM_FLAT, r), layer)] * (2 * has_h0)
        + [_layer((1, SSM_FLAT), layer)] * 3
        + [_layer((SSM_W, SSM_FLAT), layer)] * 2
        + [_layer((SSM_FLAT, SSM_W), layer)] * 2
        + [_layer((1, SSM_W), layer), _layer((SSM_W, SSM_W), layer), _layer((1, SSM_W), layer)],
        out_specs=[seq_blk, _full(state_shape)],
        out_shape=[jax.ShapeDtypeStruct((r, steps, SSM_W), F32),
                   jax.ShapeDtypeStruct(state_shape, F32)],
        scratch_shapes=[pltpu.VMEM((2, SSM_FLAT), F32),
                        pltpu.VMEM((SSM_W, 2 * SSM_FLAT), BF16),
                        pltpu.VMEM((r, 2 * SSM_FLAT), F32)]
        + [pltpu.VMEM((blk, SCAN_LANES), F32)] * (2 * SSM_FLAT // SCAN_LANES),
        compiler_params=_params(1),
        name="ssm",
    )(u, *(h0_t if has_h0 else ()), sp['a_re'], sp['a_im'], sp['log_dt'], sp['b_re'], sp['b_im'],
      sp['c_re'], sp['c_im'], sp['d'], sp['w_glu'], sp['b_glu'])


def _ssm_params(p):
    eye = jnp.eye(SSM_GROUPS, dtype=F32)
    depth = p['ssm_a_re'].shape[0]

    def b_blockdiag(b):
        return jnp.einsum('lgpc,gh->lgchp', b, eye).reshape(depth, SSM_W, SSM_FLAT)

    def c_blockdiag(c):
        return jnp.einsum('lgcp,gh->lgphc', c, eye).reshape(depth, SSM_FLAT, SSM_W)

    return {
        'a_re': p['ssm_a_re'].reshape(depth, 1, SSM_FLAT),
        'a_im': p['ssm_a_im'].reshape(depth, 1, SSM_FLAT),
        'log_dt': jnp.repeat(p['ssm_log_dt'], SSM_STATE, axis=1).reshape(depth, 1, SSM_FLAT),
        'b_re': b_blockdiag(p['ssm_b_re']),
        'b_im': b_blockdiag(p['ssm_b_im']),
        'c_re': c_blockdiag(p['ssm_c_re']).astype(BF16),
        'c_im': c_blockdiag(p['ssm_c_im']).astype(BF16),
        'd': p['ssm_d'].reshape(depth, 1, SSM_W),
        'w_glu': p['ssm_w_glu'].astype(BF16),
        'b_glu': p['ssm_b_glu'].reshape(depth, 1, SSM_W),
    }


def _outproj_body(x_ref, oa_ref, os_ref, gu_ref, gvn_ref, ws_ref, bs_ref, go_ref, w_ref, o_ref,
                  *, chunk):
    tm = x_ref.shape[0]
    ri = lax.broadcasted_iota(jnp.int32, (CHUNK, CHUNK), 0)
    ci = lax.broadcasted_iota(jnp.int32, (CHUNK, CHUNK), 1)
    causal = (ri // chunk == ci // chunk) & (ci <= ri)
    lane = lax.broadcasted_iota(jnp.int32, (1, GM_W), 1)
    w_heads = [jnp.where(causal, ws_ref[h], 0.0).astype(BF16) for h in range(GM_HEADS)]
    go = go_ref[...]
    per_pass = OUTPROJ_PASS_ROWS // CHUNK
    for r0 in range(0, tm, OUTPROJ_PASS_ROWS):
        rs = slice(r0, r0 + OUTPROJ_PASS_ROWS)
        zs = []
        for cblk in range(per_pass):
            vn = gvn_ref[r0 + cblk * CHUNK:r0 + (cblk + 1) * CHUNK, :].astype(BF16)
            z = bs_ref[...]
            for h in range(GM_HEADS):
                z_h = jnp.dot(w_heads[h], vn, preferred_element_type=F32)
                head = (lane >= h * GM_HEAD_DIM) & (lane < (h + 1) * GM_HEAD_DIM)
                z = z + jnp.where(head, z_h, 0.0)
            zs.append(z)
        o = jnp.concatenate([
            _rms(oa_ref[rs, :], go[:, :ATTN_W]),
            _rms(os_ref[rs, :], go[:, ATTN_W:ATTN_W + SSM_W]),
            _rms(gu_ref[rs, :] * jnp.concatenate(zs, axis=0), go[:, ATTN_W + SSM_W:])], axis=-1)
        o_ref[rs, :] = x_ref[rs, :] + jnp.dot(o.astype(BF16), w_ref[...],
                                              preferred_element_type=F32)


def _outproj(x, o_attn, o_ssm, gu, gvn, ws, bs, g_out, w_bf16, j, *, chunk):
    rows, d = x.shape
    tm = min(ROW_TILE, rows)
    row = lambda w: pl.BlockSpec((tm, w), lambda i: (i, 0))
    return pl.pallas_call(
        functools.partial(_outproj_body, chunk=chunk),
        grid=(rows // tm,),
        in_specs=[row(d), row(ATTN_W), row(SSM_W), row(GM_W), row(GM_W),
                  _layer((GM_HEADS, CHUNK, CHUNK), j), _layer((CHUNK, GM_W), j),
                  _full((1, d)), _layer((d, d), j)],
        out_specs=row(d),
        out_shape=jax.ShapeDtypeStruct((rows, d), F32),
        compiler_params=_params(1),
        name="outproj",
    )(x, o_attn, o_ssm, gu, gvn, ws, bs, g_out, w_bf16)


def _gmlp_params(p, chunk):
    rep = CHUNK // chunk
    ws = jnp.tile(p['gmlp_w_s'][:, :, :chunk, :chunk], (1, 1, rep, rep))
    bs = jnp.tile(p['gmlp_b_s'][:, :, :chunk], (1, 1, rep))
    bs = jnp.repeat(bs.transpose(0, 2, 1), GM_HEAD_DIM, axis=2)
    return ws, bs


def _swiglu_cols(h, wg_ref, wu_ref, wd_ref, n_chunks):
    ff = wg_ref.shape[-1]
    fc = ff // n_chunks
    acc = None
    for j in range(n_chunks):
        sl = slice(j * fc, (j + 1) * fc)
        a = jnp.dot(h, wg_ref[:, sl], preferred_element_type=F32)
        b = jnp.dot(h, wu_ref[:, sl], preferred_element_type=F32)
        m = (jax.nn.silu(a) * b).astype(BF16)
        y = jnp.dot(m, wd_ref[sl, :], preferred_element_type=F32)
        acc = y if acc is None else acc + y
    return acc


def _ffn_body(x_ref, g_ref, wg_ref, wu_ref, wd_ref, *rest, final_norm):
    gf_ref = rest[0] if final_norm else None
    o_ref = rest[-1]
    x = x_ref[...]
    h = _rms(x, g_ref[...]).astype(BF16)
    y = x + _swiglu_cols(h, wg_ref, wu_ref, wd_ref, 2)
    o_ref[...] = _rms(y, gf_ref[...]) if final_norm else y


def _ffn(x, g, wg, wu, wd, j, final_g):
    rows, d = x.shape
    ff = wg.shape[-1]
    tm = min(ROW_TILE, rows)
    row = pl.BlockSpec((tm, d), lambda i: (i, 0))
    once = lambda shape: pl.BlockSpec((None,) + shape, lambda i: (j, 0, 0),
                                      pipeline_mode=pl.Buffered(1))
    final_norm = final_g is not None
    return pl.pallas_call(
        functools.partial(_ffn_body, final_norm=final_norm),
        grid=(rows // tm,),
        in_specs=[row, _full((1, d)), once((d, ff)), once((d, ff)), once((ff, d))]
        + [_full((1, d))] * final_norm,
        out_specs=row,
        out_shape=jax.ShapeDtypeStruct((rows, d), F32),
        compiler_params=_params(1),
        name="ffn",
    )(x, g, wg, wu, wd, *([final_g] * final_norm))


def _split_bf16(x):
    hi = x.astype(BF16)
    lo = (x - hi.astype(F32)).astype(BF16)
    return hi, lo


def _router_comb(hf, wr_ref, br_ref):
    h_hi, h_lo = _split_bf16(hf)
    w_hi, w_lo = _split_bf16(wr_ref[...])
    dot = lambda a, b: jnp.dot(a, b, preferred_element_type=F32)
    logits = dot(h_hi, w_hi) + (dot(h_hi, w_lo) + dot(h_lo, w_hi)) + br_ref[...]
    lane = lax.broadcasted_iota(jnp.int32, logits.shape, 1).astype(F32)
    far = float(ROUTER_LANES)
    m1 = jnp.max(logits, axis=-1, keepdims=True)
    i1 = jnp.min(jnp.where(logits == m1, lane, far), axis=-1, keepdims=True)
    rest = jnp.where(lane == i1, -jnp.inf, logits)
    m2 = jnp.max(rest, axis=-1, keepdims=True)
    i2 = jnp.min(jnp.where(rest == m2, lane, far), axis=-1, keepdims=True)
    e2 = jnp.exp(m2 - m1)
    den = 1.0 + e2
    comb = jnp.where(lane == i1, 1.0 / den, 0.0) + jnp.where(lane == i2, e2 / den, 0.0)
    sel = jnp.where((lane == i1) | (lane == i2), 1.0, 0.0)
    return comb, sel


def _moe_body(x_ref, g_ref, wr_ref, br_ref, wgu_ref, wd_ref, *rest, final_norm):
    gf_ref = rest[0] if final_norm else None
    o_ref, h_scr, gate_scr, rank_scr, rank_t_scr, lower_scr, cnt_smem = rest[int(final_norm):]
    i = pl.program_id(0)
    e = pl.program_id(1)
    tm = x_ref.shape[0]
    ff = wd_ref.shape[0]

    @pl.when((i == 0) & (e == 0))
    def _():
        ri = lax.broadcasted_iota(jnp.int32, (tm, tm), 0)
        ci = lax.broadcasted_iota(jnp.int32, (tm, tm), 1)
        lower_scr[...] = jnp.where(ci < ri, 1.0, 0.0).astype(BF16)

    @pl.when(e == 0)
    def _():
        x = x_ref[...]
        hf = _rms(x, g_ref[...])
        h_scr[...] = hf.astype(BF16)
        comb, sel = _router_comb(hf, wr_ref, br_ref)
        gate_scr[...] = comb
        rank = jnp.dot(lower_scr[...], sel.astype(BF16), preferred_element_type=F32)
        rank = jnp.where(sel > 0.0, rank, -1.0)
        rank_scr[...] = rank
        rank_t_scr[...] = rank.T
        cnt = jnp.sum(sel, axis=0, keepdims=True)
        for k in range(N_EXPERTS):
            cnt_smem[k] = cnt[0, k].astype(jnp.int32)
        o_ref[...] = x

    lane = lax.broadcasted_iota(jnp.int32, (1, ROUTER_LANES), 1)
    pick = lane == e
    rank_col = jnp.sum(jnp.where(pick, rank_scr[...], 0.0), axis=-1, keepdims=True)
    gate_col = jnp.sum(jnp.where(pick, gate_scr[...], 0.0), axis=-1, keepdims=True)
    rank_row = rank_t_scr[pl.ds(e, 1), :]
    n_chunks = (cnt_smem[e] + (MOE_CHUNK - 1)) // MOE_CHUNK
    n_pairs = n_chunks // 2

    def run_chunk(first_slot, size):
        base = first_slot.astype(F32)
        slot_col = lax.broadcasted_iota(jnp.int32, (size, 1), 0).astype(F32) + base
        slot_row = lax.broadcasted_iota(jnp.int32, (1, size), 1).astype(F32) + base
        gather = jnp.where(rank_row == slot_col, 1.0, 0.0).astype(BF16)
        xe = jnp.dot(gather, h_scr[...], preferred_element_type=F32).astype(BF16)
        ab = jnp.dot(xe, wgu_ref[...], preferred_element_type=F32)
        m = (jax.nn.silu(ab[:, :ff]) * ab[:, ff:]).astype(BF16)
        ye = jnp.dot(m, wd_ref[...], preferred_element_type=F32).astype(BF16)
        sr = min(MOE_SCATTER_ROWS, tm)
        for rb in range(tm // sr):
            sl = slice(rb * sr, (rb + 1) * sr)
            scatter = jnp.where(rank_col[sl] == slot_row, 1.0, 0.0).astype(BF16)
            o_ref[sl, :] += gate_col[sl] * jnp.dot(scatter, ye, preferred_element_type=F32)

    def pair(c, carry):
        run_chunk(c * (2 * MOE_CHUNK), 2 * MOE_CHUNK)
        return carry

    lax.fori_loop(0, n_pairs, pair, 0)

    @pl.when(n_chunks % 2 == 1)
    def _():
        run_chunk(n_pairs * (2 * MOE_CHUNK), MOE_CHUNK)

    if final_norm:
        @pl.when(e == pl.num_programs(1) - 1)
        def _():
            o_ref[...] = _rms(o_ref[...], gf_ref[...])


def _moe(x, g, wr_pad, br_pad, wgu, wd, j, final_g):
    rows, d = x.shape
    _, n_e, ff, _ = wd.shape
    tm = min(MOE_ROW_TILE, rows)
    row = pl.BlockSpec((tm, d), lambda i, e: (i, 0))
    final_norm = final_g is not None
    return pl.pallas_call(
        functools.partial(_moe_body, final_norm=final_norm),
        grid=(rows // tm, n_e),
        in_specs=[row, _full((1, d)), _full((d, ROUTER_LANES)), _full((1, ROUTER_LANES)),
                  pl.BlockSpec((None, None, d, 2 * ff), lambda i, e: (j, e, 0, 0)),
                  pl.BlockSpec((None, None, ff, d), lambda i, e: (j, e, 0, 0))]
        + [_full((1, d))] * final_norm,
        out_specs=row,
        out_shape=jax.ShapeDtypeStruct((rows, d), F32),
        scratch_shapes=[pltpu.VMEM((tm, d), BF16),
                        pltpu.VMEM((tm, ROUTER_LANES), F32),
                        pltpu.VMEM((tm, ROUTER_LANES), F32),
                        pltpu.VMEM((ROUTER_LANES, tm), F32),
                        pltpu.VMEM((tm, tm), BF16),
                        pltpu.SMEM((N_EXPERTS,), jnp.int32)],
        compiler_params=_params(2),
        name="moe",
    )(x, g, wr_pad, br_pad, wgu, wd, *([final_g] * final_norm))


def _trunk(x, p, past):
    b, l, d = x.shape
    depth = p['w_in'].shape[0]
    rows = b * l
    xr = x.reshape(rows, d)
    prompt = past is None
    new = {'k': [], 'v': [], 're': [], 'im': [], 'gv': []}
    sp = p['ssm']
    ws, bs = _gmlp_params(p, min(l, CHUNK))
    if not prompt:
        w = past[0].shape[2]
        kc_t = past[0].transpose(0, 1, 3, 4, 2).reshape(depth, b, KV_W, w)
        vc_t = past[1].transpose(0, 1, 3, 4, 2).reshape(depth, b, KV_W, w)
        k_out = jnp.zeros_like(kc_t)
        v_out = jnp.zeros_like(vc_t)
        h0_t = tuple(s.transpose(0, 2, 3, 1).reshape(depth, SSM_FLAT, b) for s in past[2:])
    for i in range(depth):
        g_mix = p['norm_mix_g'][i].reshape(1, d)
        gv = p['gmlp_v_norm_g'][i].reshape(1, GM_W)
        q, k, v, u, gu, gvn = _inproj(xr, g_mix, p['w_in_bf16'], gv, i)
        u = u.reshape(b, l, SSM_W)
        sinks = p['attn_sinks'][i]
        if prompt:
            o_attn = _attn_prompt(q, k, v, sinks, batch=b, seq=l)
            nw = min(WINDOW, l)
            k_win = k.reshape(b, l, N_KV_HEADS, HEAD_DIM)[:, l - nw:]
            v_win = v.reshape(b, l, N_KV_HEADS, HEAD_DIM)[:, l - nw:]
            o_ssm, s_fin = _ssm(u, None, sp, i, tc=CHUNK)
            s_re = s_fin[:, :SSM_FLAT].reshape(b, SSM_GROUPS, SSM_STATE)
            s_im = s_fin[:, SSM_FLAT:].reshape(b, SSM_GROUPS, SSM_STATE)
        else:
            o_attn, k_out, v_out = _attn_sample(q, k, v, kc_t, vc_t, k_out, v_out, sinks, i,
                                                batch=b, t=l)
            o_ssm, s_fin = _ssm(u, h0_t, sp, i, tc=l)
            s_re, s_im = (s.reshape(SSM_GROUPS, SSM_STATE, b).transpose(2, 0, 1) for s in s_fin)
            new['gv'].append(gvn.reshape(b, l, GM_HEADS, GM_HEAD_DIM))
        xr = _outproj(xr, o_attn, o_ssm.reshape(rows, SSM_W), gu, gvn, ws, bs,
                      p['mix_out_norm_g'][i].reshape(1, d), p['w_out_bf16'], i,
                      chunk=min(l, CHUNK))
        g_ffn = p['norm_ffn_g'][i].reshape(1, d)
        final_g = p['final_norm_g'].reshape(1, d) if i == depth - 1 else None
        j = i // 2
        if i % 2 == 0:
            xr = _ffn(xr, g_ffn, p['ffn_w_gate_bf16'], p['ffn_w_up_bf16'],
                      p['ffn_w_down_bf16'], j, final_g)
        else:
            wr = jnp.pad(p['moe_w_router'][j], ((0, 0), (0, ROUTER_LANES - N_EXPERTS)))
            br = jnp.pad(p['moe_b_router'][j], (0, ROUTER_LANES - N_EXPERTS),
                         constant_values=NEG_BIG).reshape(1, ROUTER_LANES)
            xr = _moe(xr, g_ffn, wr, br, p['moe_w_gate_up_bf16'], p['moe_w_down_bf16'], j,
                      final_g)
        if prompt:
            new['k'].append(k_win)
            new['v'].append(v_win)
        new['re'].append(s_re)
        new['im'].append(s_im)
    for name in ('re', 'im', 'gv') + (('k', 'v') if prompt else ()):
        new[name] = jnp.stack(new[name]) if new[name] else None
    if not prompt:
        back = lambda c: c.reshape(depth, b, N_KV_HEADS, HEAD_DIM, w).transpose(0, 1, 4, 2, 3)
        new['k'], new['v'] = back(k_out), back(v_out)
    return xr.reshape(b, l, d), new


def kernel(x_prompt, x_sample, cache_k_win, cache_v_win, state_ssm_re, state_ssm_im,
           norm_mix_g, w_in, attn_sinks, ssm_a_re, ssm_a_im, ssm_log_dt, ssm_b_re, ssm_b_im,
           ssm_c_re, ssm_c_im, ssm_d, ssm_w_glu, ssm_b_glu, gmlp_v_norm_g, gmlp_w_s, gmlp_b_s,
           mix_out_norm_g, w_out, norm_ffn_g, ffn_w_gate, ffn_w_up, ffn_w_down,
           moe_w_router, moe_b_router, moe_w_gate, moe_w_up, moe_w_down, final_norm_g):
    params = {
        'norm_mix_g': norm_mix_g, 'w_in': w_in, 'attn_sinks': attn_sinks,
        'ssm_a_re': ssm_a_re, 'ssm_a_im': ssm_a_im, 'ssm_log_dt': ssm_log_dt,
        'ssm_b_re': ssm_b_re, 'ssm_b_im': ssm_b_im, 'ssm_c_re': ssm_c_re, 'ssm_c_im': ssm_c_im,
        'ssm_d': ssm_d, 'ssm_w_glu': ssm_w_glu, 'ssm_b_glu': ssm_b_glu,
        'gmlp_v_norm_g': gmlp_v_norm_g, 'gmlp_w_s': gmlp_w_s, 'gmlp_b_s': gmlp_b_s,
        'mix_out_norm_g': mix_out_norm_g, 'w_out': w_out, 'norm_ffn_g': norm_ffn_g,
        'ffn_w_gate': ffn_w_gate, 'ffn_w_up': ffn_w_up, 'ffn_w_down': ffn_w_down,
        'moe_w_router': moe_w_router, 'moe_b_router': moe_b_router,
        'moe_w_gate': moe_w_gate, 'moe_w_up': moe_w_up, 'moe_w_down': moe_w_down,
        'final_norm_g': final_norm_g,
    }
    for name in ('w_in', 'w_out', 'ffn_w_gate', 'ffn_w_up', 'ffn_w_down', 'moe_w_down'):
        params[name + '_bf16'] = params[name].astype(BF16)
    params['moe_w_gate_up_bf16'] = jnp.concatenate(
        [moe_w_gate.astype(BF16), moe_w_up.astype(BF16)], axis=-1)
    params['ssm'] = _ssm_params(params)
    y_p, st_p = _trunk(x_prompt, params, None)
    y_s, st_s = _trunk(x_sample, params,
                       (cache_k_win, cache_v_win, state_ssm_re, state_ssm_im))
    return (y_p, y_s,
            st_p['k'], st_p['v'], st_p['re'], st_p['im'],
            st_s['k'], st_s['v'], st_s['re'], st_s['im'], st_s['gv'])
```

```python
import functools
import math

import jax
import jax.numpy as jnp
from jax import lax
from jax.experimental import pallas as pl
from jax.experimental.pallas import tpu as pltpu

F32 = jnp.float32
BF16 = jnp.bfloat16

EPS = 1e-6
HEAD_DIM = 64
N_HEADS = 8
N_KV_HEADS = 2
GQA_GROUP = N_HEADS // N_KV_HEADS
ATTN_W = N_HEADS * HEAD_DIM
KV_W = N_KV_HEADS * HEAD_DIM
WINDOW = 128
ATTN_SCALE = 1.0 / math.sqrt(HEAD_DIM)
LOG2E = math.log2(math.e)
SSM_W = 256
SSM_GROUP_CH = 16
SSM_GROUPS = 16
SSM_STATE = 64
SSM_FLAT = SSM_GROUPS * SSM_STATE
GM_W = 256
GM_HEADS = 4
GM_HEAD_DIM = 64
CHUNK = 128
Q_END = ATTN_W
K_END = Q_END + KV_W
V_END = K_END + KV_W
S_END = V_END + SSM_W
IN_COLS = S_END + 2 * GM_W
N_EXPERTS = 8
ROUTER_LANES = 128
NEG_BIG = -1e30

VMEM_LIMIT_BYTES = 56 * 1024 * 1024
ROW_TILE = 512
SUB_ROWS = 128
OUTPROJ_PASS_ROWS = 256
ATTN_Q_BLOCKS = 4
MOE_ROW_TILE = 1024
MOE_CHUNK = 128
MOE_SCATTER_ROWS = 256
SCAN_LANES = 512


def _params(n_axes):
    return pltpu.CompilerParams(
        dimension_semantics=("arbitrary",) * n_axes,
        vmem_limit_bytes=VMEM_LIMIT_BYTES)


def _rms(x, g):
    return x * lax.rsqrt(jnp.mean(x * x, axis=-1, keepdims=True) + EPS) * g


def _bdot(a, b):
    return jnp.dot(a.astype(BF16), b.astype(BF16), preferred_element_type=F32)


def _full(shape):
    return pl.BlockSpec(shape, lambda *_: (0,) * len(shape))


def _group_mean(a, n_groups, width):
    lane = lax.broadcasted_iota(jnp.int32, (1, n_groups * width), 1)
    out = jnp.zeros_like(a)
    for h in range(n_groups):
        m = (lane >= h * width) & (lane < (h + 1) * width)
        s = jnp.sum(jnp.where(m, a, 0.0), axis=-1, keepdims=True) * (1.0 / width)
        out = jnp.where(m, s, out)
    return out


def _inproj_body(x_ref, g_ref, w_ref, gv_ref, q_ref, k_ref, v_ref, u_ref, gu_ref, gvn_ref):
    for r0 in range(0, x_ref.shape[0], SUB_ROWS):
        rs = slice(r0, r0 + SUB_ROWS)
        h = _rms(x_ref[rs, :], g_ref[...])
        z = jnp.dot(h.astype(BF16), w_ref[...], preferred_element_type=F32)
        q_ref[rs, :] = (z[:, :Q_END] * (ATTN_SCALE * LOG2E)).astype(BF16)
        k_ref[rs, :] = z[:, Q_END:K_END]
        v_ref[rs, :] = z[:, K_END:V_END]
        u_ref[rs, :] = z[:, V_END:S_END]
        g = jax.nn.gelu(z[:, S_END:])
        gu_ref[rs, :] = g[:, :GM_W]
        vv = g[:, GM_W:]
        mu = _group_mean(vv, GM_HEADS, GM_HEAD_DIM)
        var = _group_mean(jnp.square(vv - mu), GM_HEADS, GM_HEAD_DIM)
        gvn_ref[rs, :] = (vv - mu) * lax.rsqrt(var + EPS) * gv_ref[...]


def _layer(shape, j):
    return pl.BlockSpec((None,) + tuple(shape), lambda *_: (j,) + (0,) * len(shape))


def _inproj(x, g, w_bf16, gv, j):
    rows, d = x.shape
    tm = min(ROW_TILE, rows)
    nt = rows // tm
    row = lambda w: pl.BlockSpec((tm, w), lambda i: (i, 0))
    return pl.pallas_call(
        _inproj_body,
        grid=(nt,),
        in_specs=[row(d), _full((1, d)), _layer((d, IN_COLS), j), _full((1, GM_W))],
        out_specs=[row(ATTN_W), row(KV_W), row(KV_W), row(SSM_W), row(GM_W), row(GM_W)],
        out_shape=[jax.ShapeDtypeStruct((rows, ATTN_W), BF16),
                   jax.ShapeDtypeStruct((rows, KV_W), F32),
                   jax.ShapeDtypeStruct((rows, KV_W), F32),
                   jax.ShapeDtypeStruct((rows, SSM_W), F32),
                   jax.ShapeDtypeStruct((rows, GM_W), F32),
                   jax.ShapeDtypeStruct((rows, GM_W), F32)],
        compiler_params=_params(1),
        name="inproj",
    )(x, g, w_bf16, gv)


def _sink_softmax(s, mask, sink):
    s = jnp.where(mask, s, -jnp.inf)
    m = jnp.maximum(jnp.max(s, axis=-1, keepdims=True), sink)
    e = jnp.exp2(s - m)
    return e / (jnp.sum(e, axis=-1, keepdims=True) + jnp.exp2(sink - m))


def _head_pair_operands(x):
    low = lax.broadcasted_iota(jnp.int32, (1, KV_W), 1) < HEAD_DIM
    swapped = pltpu.roll(x, HEAD_DIM, axis=1)
    keep = lambda a, in_low: jnp.where(low == in_low, a, 0.0).astype(BF16)
    return ((keep(x, True), keep(swapped, False)),
            (keep(swapped, True), keep(x, False)))


def _attn_prompt_body(sink_ref, q_ref, kc_ref, kp_ref, vc_ref, vp_ref, o_ref, *, nq):
    n = pl.program_id(1)
    qi = lax.broadcasted_iota(jnp.int32, (WINDOW, 2 * WINDOW), 0)
    si = lax.broadcasted_iota(jnp.int32, (WINDOW, 2 * WINDOW), 1)
    dist = WINDOW + qi - si
    band = (dist >= 0) & (dist < WINDOW)
    first = band & ((n > 0) | (si >= WINDOW))
    k_ops = _head_pair_operands(jnp.concatenate([kp_ref[...], kc_ref[...]], axis=0))
    v_ops = _head_pair_operands(jnp.concatenate([vp_ref[...], vc_ref[...]], axis=0))
    pair_w = 2 * HEAD_DIM
    for j in range(nq):
        mask = first if j == 0 else band
        q_rows = slice(j * WINDOW, (j + 1) * WINDOW)
        kv_rows = slice(j * WINDOW, (j + 2) * WINDOW)
        for hp in range(N_HEADS // 2):
            kh = (2 * hp) // GQA_GROUP
            qp = q_ref[q_rows, hp * pair_w:(hp + 1) * pair_w]
            acc = None
            for par in range(2):
                sink = sink_ref[2 * hp + par] * LOG2E
                s = lax.dot_general(qp, k_ops[kh][par][kv_rows], (((1,), (1,)), ((), ())),
                                    preferred_element_type=F32)
                s = jnp.where(mask, s, -jnp.inf)
                m = jnp.maximum(jnp.max(s, axis=-1, keepdims=True), sink)
                e = jnp.exp2(s - m)
                den = jnp.sum(e, axis=-1, keepdims=True) + jnp.exp2(sink - m)
                pv = jnp.dot(e.astype(BF16), v_ops[kh][par][kv_rows],
                             preferred_element_type=F32)
                pv = pv * (1.0 / den)
                acc = pv if acc is None else acc + pv
            o_ref[q_rows, hp * pair_w:(hp + 1) * pair_w] = acc


def _attn_prompt(q, k, v, sinks, *, batch, seq):
    nq = ATTN_Q_BLOCKS
    nb = seq // WINDOW
    steps = nb // nq
    cur = lambda w: pl.BlockSpec((nq * WINDOW, w), lambda b, n: (b * steps + n, 0))
    prev = lambda w: pl.BlockSpec(
        (WINDOW, w), lambda b, n: (b * nb + jnp.maximum(n * nq - 1, 0), 0))
    return pl.pallas_call(
        functools.partial(_attn_prompt_body, nq=nq),
        grid=(batch, steps),
        in_specs=[pl.BlockSpec(memory_space=pltpu.SMEM),
                  cur(ATTN_W), cur(KV_W), prev(KV_W), cur(KV_W), prev(KV_W)],
        out_specs=cur(ATTN_W),
        out_shape=jax.ShapeDtypeStruct((batch * seq, ATTN_W), F32),
        compiler_params=_params(2),
        name="attn_prompt",
    )(sinks, q, k, k, v, v)


def _attn_sample_body(sink_ref, q_ref, k_ref, v_ref, kc_ref, vc_ref, kin_ref, vin_ref,
                      o_ref, ko_ref, vo_ref, *, bb, t):
    del kin_ref, vin_ref
    w = kc_ref.shape[-1]
    pair_w = 2 * HEAD_DIM
    lane = lax.broadcasted_iota(jnp.int32, (1, w), 1)
    new_col = lane < t
    k_new_t = k_ref[...].T
    v_new_t = v_ref[...].T
    qf = q_ref[...].astype(F32)
    unit = 2 * t
    upper = lax.broadcasted_iota(jnp.int32, (unit, 1), 0) < t
    zeros_half = jnp.zeros((HEAD_DIM, 2 * w), BF16)
    placed = lambda a, par: jnp.concatenate((a, zeros_half) if par == 0 else (zeros_half, a), axis=0)
    scores, sinks, values = [], [], []
    for b in range(bb):
        shift = (w - b * t) % w
        k_cols = jnp.where(new_col, pltpu.roll(k_new_t, shift, axis=1), 0.0)
        v_cols = jnp.where(new_col, pltpu.roll(v_new_t, shift, axis=1), 0.0)
        k_old = kc_ref[b]
        v_old = vc_ref[b]
        ko_ref[b] = pltpu.roll(jnp.where(new_col, k_cols, k_old), w - t, axis=1)
        vo_ref[b] = pltpu.roll(jnp.where(new_col, v_cols, v_old), w - t, axis=1)
        k_all = jnp.concatenate([k_old, k_cols], axis=1).astype(BF16)
        v_all = jnp.concatenate([v_old, v_cols], axis=1).astype(BF16)
        for kh in range(N_KV_HEADS):
            kv_sl = slice(kh * HEAD_DIM, (kh + 1) * HEAD_DIM)
            q2 = jnp.concatenate(
                [qf[b * t:(b + 1) * t, (2 * kh + c) * pair_w:(2 * kh + c + 1) * pair_w]
                 for c in range(2)], axis=0).astype(BF16)
            for par in range(2):
                h_a = GQA_GROUP * kh + par
                sinks.append(jnp.where(upper, sink_ref[h_a], sink_ref[h_a + 2]) * LOG2E)
                scores.append(jnp.dot(q2, placed(k_all[kv_sl], par),
                                      preferred_element_type=F32))
                values.append(placed(v_all[kv_sl], par))
    s = jnp.concatenate(scores, axis=0)
    ti = lax.broadcasted_iota(jnp.int32, s.shape, 0) % t
    si = lax.broadcasted_iota(jnp.int32, s.shape, 1)
    dist = w + ti - si
    p = _sink_softmax(s, (dist >= 0) & (dist < WINDOW), jnp.concatenate(sinks, axis=0)).astype(BF16)
    for b in range(bb):
        for kh in range(N_KV_HEADS):
            acc = None
            for par in range(2):
                u = (b * N_KV_HEADS + kh) * 2 + par
                pv = lax.dot_general(p[u * unit:(u + 1) * unit], values[u],
                                     (((1,), (1,)), ((), ())), preferred_element_type=F32)
                acc = pv if acc is None else acc + pv
            for c in range(2):
                o_ref[b * t:(b + 1) * t, (2 * kh + c) * pair_w:(2 * kh + c + 1) * pair_w] = (
                    acc[c * t:(c + 1) * t])


def _attn_sample(q, k, v, kc_t, vc_t, k_out, v_out, sinks, layer, *, batch, t):
    w = kc_t.shape[-1]
    bb = w // t
    row = lambda c: pl.BlockSpec((bb * t, c), lambda i: (i, 0))
    buf = pl.BlockSpec((None, bb, KV_W, w), lambda i: (layer, i, 0, 0))
    anywhere = pl.BlockSpec(memory_space=pl.ANY)
    return pl.pallas_call(
        functools.partial(_attn_sample_body, bb=bb, t=t),
        grid=(batch // bb,),
        in_specs=[pl.BlockSpec(memory_space=pltpu.SMEM),
                  row(ATTN_W), row(KV_W), row(KV_W), buf, buf, anywhere, anywhere],
        out_specs=[row(ATTN_W), buf, buf],
        out_shape=[jax.ShapeDtypeStruct((batch * t, ATTN_W), F32),
                   jax.ShapeDtypeStruct(k_out.shape, F32),
                   jax.ShapeDtypeStruct(v_out.shape, F32)],
        input_output_aliases={6: 1, 7: 2},
        compiler_params=_params(1),
        name="attn_sample",
    )(sinks, q, k, v, kc_t, vc_t, k_out, v_out)


def _ssm_body(u_ref, *refs, r, tc, has_h0):
    h0_refs, refs = refs[:2 * has_h0], refs[2 * has_h0:]
    (are_ref, aim_ref, ldt_ref, bre_ref, bim_ref, cre_ref, cim_ref, d_ref, wglu_ref, bglu_ref,
     o_ref, s_ref, abar_scr, bbar_scr, h_scr, *x_scrs) = refs
    c = pl.program_id(0)

    @pl.when(c == 0)
    def _():
        ar = are_ref[...]
        ai = aim_ref[...]
        dt = jnp.exp(ldt_ref[...])
        decay = jnp.exp(dt * ar)
        abr = decay * jnp.cos(dt * ai)
        abi = decay * jnp.sin(dt * ai)
        den = ar * ar + ai * ai
        nr = abr - 1.0
        fr = (nr * ar + abi * ai) / den
        fi = (abi * ar - nr * ai) / den
        abar_scr[0:1, :] = abr
        abar_scr[1:2, :] = abi
        br = bre_ref[...]
        bi = bim_ref[...]
        bbar_scr[:, :SSM_FLAT] = (fr * br - fi * bi).astype(BF16)
        bbar_scr[:, SSM_FLAT:] = (fr * bi + fi * br).astype(BF16)
        if has_h0:
            h_scr[:, :SSM_FLAT] = h0_refs[0][...].T
            h_scr[:, SSM_FLAT:] = h0_refs[1][...].T
        else:
            h_scr[...] = jnp.zeros_like(h_scr)

    u = jnp.swapaxes(u_ref[...], 0, 1).reshape(tc * r, SSM_W)
    ub = u.astype(BF16)

    y = None
    for lc in range(SSM_FLAT // SCAN_LANES):
        lanes = slice(lc * SCAN_LANES, (lc + 1) * SCAN_LANES)
        re_sl = lanes
        im_sl = slice(SSM_FLAT + lc * SCAN_LANES, SSM_FLAT + (lc + 1) * SCAN_LANES)
        xr_scr, xi_scr = x_scrs[2 * lc], x_scrs[2 * lc + 1]
        xr_scr[...] = jnp.dot(ub, bbar_scr[:, re_sl], preferred_element_type=F32)
        xi_scr[...] = jnp.dot(ub, bbar_scr[:, im_sl], preferred_element_type=F32)
        ar = jnp.broadcast_to(abar_scr[0:1, lanes], (8, SCAN_LANES))
        ai = jnp.broadcast_to(abar_scr[1:2, lanes], (8, SCAN_LANES))
        for rg in range(r // 8):
            rg_sl = slice(rg * 8, (rg + 1) * 8)
            hr, hi = h_scr[rg_sl, re_sl], h_scr[rg_sl, im_sl]
            for t in range(tc):
                rows = slice(t * r + rg * 8, t * r + rg * 8 + 8)
                hr, hi = (ar * hr - ai * hi + xr_scr[rows, :],
                          ar * hi + ai * hr + xi_scr[rows, :])
                xr_scr[rows, :] = hr
                xi_scr[rows, :] = hi
            h_scr[rg_sl, re_sl] = hr
            h_scr[rg_sl, im_sl] = hi
        part = (jnp.dot(xr_scr[...].astype(BF16), cre_ref[lanes, :], preferred_element_type=F32)
                - jnp.dot(xi_scr[...].astype(BF16), cim_ref[lanes, :],
                          preferred_element_type=F32))
        y = part if y is None else y + part
    y = jax.nn.gelu(y + d_ref[...] * u)
    gl = jnp.dot(y.astype(BF16), wglu_ref[...], preferred_element_type=F32) + bglu_ref[...]
    o_ref[...] = jnp.swapaxes((y * jax.nn.sigmoid(gl)).reshape(tc, r, SSM_W), 0, 1)

    @pl.when(c == pl.num_programs(0) - 1)
    def _():
        if has_h0:
            s_ref[0] = h_scr[:, :SSM_FLAT].T
            s_ref[1] = h_scr[:, SSM_FLAT:].T
        else:
            s_ref[...] = h_scr[...]


def _ssm(u, h0_t, sp, layer, *, tc):
    r, steps, _ = u.shape
    blk = tc * r
    has_h0 = h0_t is not None
    seq_blk = pl.BlockSpec((r, tc, SSM_W), lambda c: (0, c, 0))
    state_shape = (2, SSM_FLAT, r) if has_h0 else (r, 2 * SSM_FLAT)
    return pl.pallas_call(
        functools.partial(_ssm_body, r=r, tc=tc, has_h0=has_h0),
        grid=(steps // tc,),
        in_specs=[seq_blk]
        + [_layer((SSM_FLAT, r), layer)] * (2 * has_h0)
        + [_layer((1, SSM_FLAT), layer)] * 3
        + [_layer((SSM_W, SSM_FLAT), layer)] * 2
        + [_layer((SSM_FLAT, SSM_W), layer)] * 2
        + [_layer((1, SSM_W), layer), _layer((SSM_W, SSM_W), layer), _layer((1, SSM_W), layer)],
        out_specs=[seq_blk, _full(state_shape)],
        out_shape=[jax.ShapeDtypeStruct((r, steps, SSM_W), F32),
                   jax.ShapeDtypeStruct(state_shape, F32)],
        scratch_shapes=[pltpu.VMEM((2, SSM_FLAT), F32),
                        pltpu.VMEM((SSM_W, 2 * SSM_FLAT), BF16),
                        pltpu.VMEM((r, 2 * SSM_FLAT), F32)]
        + [pltpu.VMEM((blk, SCAN_LANES), F32)] * (2 * SSM_FLAT // SCAN_LANES),
        compiler_params=_params(1),
        name="ssm",
    )(u, *(h0_t if has_h0 else ()), sp['a_re'], sp['a_im'], sp['log_dt'], sp['b_re'], sp['b_im'],
      sp['c_re'], sp['c_im'], sp['d'], sp['w_glu'], sp['b_glu'])


def _ssm_params(p):
    eye = jnp.eye(SSM_GROUPS, dtype=F32)
    depth = p['ssm_a_re'].shape[0]

    def b_blockdiag(b):
        return jnp.einsum('lgpc,gh->lgchp', b, eye).reshape(depth, SSM_W, SSM_FLAT)

    def c_blockdiag(c):
        return jnp.einsum('lgcp,gh->lgphc', c, eye).reshape(depth, SSM_FLAT, SSM_W)

    return {
        'a_re': p['ssm_a_re'].reshape(depth, 1, SSM_FLAT),
        'a_im': p['ssm_a_im'].reshape(depth, 1, SSM_FLAT),
        'log_dt': jnp.repeat(p['ssm_log_dt'], SSM_STATE, axis=1).reshape(depth, 1, SSM_FLAT),
        'b_re': b_blockdiag(p['ssm_b_re']),
        'b_im': b_blockdiag(p['ssm_b_im']),
        'c_re': c_blockdiag(p['ssm_c_re']).astype(BF16),
        'c_im': c_blockdiag(p['ssm_c_im']).astype(BF16),
        'd': p['ssm_d'].reshape(depth, 1, SSM_W),
        'w_glu': p['ssm_w_glu'].astype(BF16),
        'b_glu': p['ssm_b_glu'].reshape(depth, 1, SSM_W),
    }


def _outproj_body(x_ref, oa_ref, os_ref, gu_ref, gvn_ref, ws_ref, bs_ref, go_ref, w_ref, o_ref,
                  *, chunk):
    tm = x_ref.shape[0]
    ri = lax.broadcasted_iota(jnp.int32, (CHUNK, CHUNK), 0)
    ci = lax.broadcasted_iota(jnp.int32, (CHUNK, CHUNK), 1)
    causal = (ri // chunk == ci // chunk) & (ci <= ri)
    lane = lax.broadcasted_iota(jnp.int32, (1, GM_W), 1)
    w_heads = [jnp.where(causal, ws_ref[h], 0.0).astype(BF16) for h in range(GM_HEADS)]
    go = go_ref[...]
    per_pass = OUTPROJ_PASS_ROWS // CHUNK
    for r0 in range(0, tm, OUTPROJ_PASS_ROWS):
        rs = slice(r0, r0 + OUTPROJ_PASS_ROWS)
        zs = []
        for cblk in range(per_pass):
            vn = gvn_ref[r0 + cblk * CHUNK:r0 + (cblk + 1) * CHUNK, :].astype(BF16)
            z = bs_ref[...]
            for h in range(GM_HEADS):
                z_h = jnp.dot(w_heads[h], vn, preferred_element_type=F32)
                head = (lane >= h * GM_HEAD_DIM) & (lane < (h + 1) * GM_HEAD_DIM)
                z = z + jnp.where(head, z_h, 0.0)
            zs.append(z)
        o = jnp.concatenate([
            _rms(oa_ref[rs, :], go[:, :ATTN_W]),
            _rms(os_ref[rs, :], go[:, ATTN_W:ATTN_W + SSM_W]),
            _rms(gu_ref[rs, :] * jnp.concatenate(zs, axis=0), go[:, ATTN_W + SSM_W:])], axis=-1)
        o_ref[rs, :] = x_ref[rs, :] + jnp.dot(o.astype(BF16), w_ref[...],
                                              preferred_element_type=F32)


def _outproj(x, o_attn, o_ssm, gu, gvn, ws, bs, g_out, w_bf16, j, *, chunk):
    rows, d = x.shape
    tm = min(ROW_TILE, rows)
    row = lambda w: pl.BlockSpec((tm, w), lambda i: (i, 0))
    return pl.pallas_call(
        functools.partial(_outproj_body, chunk=chunk),
        grid=(rows // tm,),
        in_specs=[row(d), row(ATTN_W), row(SSM_W), row(GM_W), row(GM_W),
                  _layer((GM_HEADS, CHUNK, CHUNK), j), _layer((CHUNK, GM_W), j),
                  _full((1, d)), _layer((d, d), j)],
        out_specs=row(d),
        out_shape=jax.ShapeDtypeStruct((rows, d), F32),
        compiler_params=_params(1),
        name="outproj",
    )(x, o_attn, o_ssm, gu, gvn, ws, bs, g_out, w_bf16)


def _gmlp_params(p, chunk):
    rep = CHUNK // chunk
    ws = jnp.tile(p['gmlp_w_s'][:, :, :chunk, :chunk], (1, 1, rep, rep))
    bs = jnp.tile(p['gmlp_b_s'][:, :, :chunk], (1, 1, rep))
    bs = jnp.repeat(bs.transpose(0, 2, 1), GM_HEAD_DIM, axis=2)
    return ws, bs


def _swiglu_cols(h, wg_ref, wu_ref, wd_ref, n_chunks):
    ff = wg_ref.shape[-1]
    fc = ff // n_chunks
    acc = None
    for j in range(n_chunks):
        sl = slice(j * fc, (j + 1) * fc)
        a = jnp.dot(h, wg_ref[:, sl], preferred_element_type=F32)
        b = jnp.dot(h, wu_ref[:, sl], preferred_element_type=F32)
        m = (jax.nn.silu(a) * b).astype(BF16)
        y = jnp.dot(m, wd_ref[sl, :], preferred_element_type=F32)
        acc = y if acc is None else acc + y
    return acc


def _ffn_body(x_ref, g_ref, wg_ref, wu_ref, wd_ref, *rest, final_norm, n_cast):
    gf_ref = rest[0] if final_norm else None
    rest = rest[int(final_norm):]
    cast_in, o_ref, cast_out = rest[:n_cast], rest[n_cast], rest[n_cast + 1:]
    x = x_ref[...]
    h = _rms(x, g_ref[...]).astype(BF16)
    y = x + _swiglu_cols(h, wg_ref, wu_ref, wd_ref, 2)
    o_ref[...] = _rms(y, gf_ref[...]) if final_norm else y
    if n_cast:
        eg_ref, eu_ref, ed_ref = cast_in
        egu_ref, edo_ref = cast_out
        ffe = eg_ref.shape[-1]
        egu_ref[:, :ffe] = eg_ref[...].astype(BF16)
        egu_ref[:, ffe:] = eu_ref[...].astype(BF16)
        edo_ref[...] = ed_ref[...].astype(BF16)


def _ffn(x, g, wg, wu, wd, j, final_g, experts=None):
    rows, d = x.shape
    ff = wg.shape[-1]
    tm = min(ROW_TILE, rows)
    steps = rows // tm
    row = pl.BlockSpec((tm, d), lambda i: (i, 0))
    once = lambda shape: pl.BlockSpec((None,) + shape, lambda i: (j, 0, 0),
                                      pipeline_mode=pl.Buffered(1))
    final_norm = final_g is not None
    cast_args, cast_in, cast_out, cast_shapes = [], [], [], []
    if experts is not None:
        eg, eu, ed, le = experts
        n_l, n_e, _, ffe = eg.shape
        gu_rows, d_rows = n_e * d // steps, n_e * ffe // steps
        slab = lambda r, c: pl.BlockSpec((r, c), lambda i: (le * steps + i, 0))
        cast_args = [eg.reshape(n_l * n_e * d, ffe), eu.reshape(n_l * n_e * d, ffe),
                     ed.reshape(n_l * n_e * ffe, d)]
        cast_in = [slab(gu_rows, ffe), slab(gu_rows, ffe), slab(d_rows, d)]
        cast_out = [pl.BlockSpec((gu_rows, 2 * ffe), lambda i: (i, 0)),
                    pl.BlockSpec((d_rows, d), lambda i: (i, 0))]
        cast_shapes = [jax.ShapeDtypeStruct((n_e * d, 2 * ffe), BF16),
                       jax.ShapeDtypeStruct((n_e * ffe, d), BF16)]
    out = pl.pallas_call(
        functools.partial(_ffn_body, final_norm=final_norm, n_cast=len(cast_in)),
        grid=(steps,),
        in_specs=[row, _full((1, d)), once((d, ff)), once((d, ff)), once((ff, d))]
        + [_full((1, d))] * final_norm + cast_in,
        out_specs=[row] + cast_out,
        out_shape=[jax.ShapeDtypeStruct((rows, d), F32)] + cast_shapes,
        compiler_params=_params(1),
        name="ffn",
    )(x, g, wg, wu, wd, *([final_g] * final_norm), *cast_args)
    if experts is None:
        return out[0], None
    return out[0], (out[1].reshape(n_e, d, 2 * ffe), out[2].reshape(n_e, ffe, d))


def _split_bf16(x):
    hi = x.astype(BF16)
    lo = (x - hi.astype(F32)).astype(BF16)
    return hi, lo


def _router_comb(hf, wr_ref, br_ref):
    h_hi, h_lo = _split_bf16(hf)
    w_hi, w_lo = _split_bf16(wr_ref[...])
    dot = lambda a, b: jnp.dot(a, b, preferred_element_type=F32)
    logits = dot(h_hi, w_hi) + (dot(h_hi, w_lo) + dot(h_lo, w_hi)) + br_ref[...]
    lane = lax.broadcasted_iota(jnp.int32, logits.shape, 1).astype(F32)
    far = float(ROUTER_LANES)
    m1 = jnp.max(logits, axis=-1, keepdims=True)
    i1 = jnp.min(jnp.where(logits == m1, lane, far), axis=-1, keepdims=True)
    rest = jnp.where(lane == i1, -jnp.inf, logits)
    m2 = jnp.max(rest, axis=-1, keepdims=True)
    i2 = jnp.min(jnp.where(rest == m2, lane, far), axis=-1, keepdims=True)
    e2 = jnp.exp(m2 - m1)
    den = 1.0 + e2
    comb = jnp.where(lane == i1, 1.0 / den, 0.0) + jnp.where(lane == i2, e2 / den, 0.0)
    sel = jnp.where((lane == i1) | (lane == i2), 1.0, 0.0)
    return comb, sel


def _moe_body(x_ref, g_ref, wr_ref, br_ref, wgu_ref, wd_ref, *rest, final_norm):
    gf_ref = rest[0] if final_norm else None
    o_ref, h_scr, gate_scr, rank_scr, rank_t_scr, lower_scr, cnt_smem = rest[int(final_norm):]
    i = pl.program_id(0)
    e = pl.program_id(1)
    tm = x_ref.shape[0]
    ff = wd_ref.shape[0]

    @pl.when((i == 0) & (e == 0))
    def _():
        ri = lax.broadcasted_iota(jnp.int32, (tm, tm), 0)
        ci = lax.broadcasted_iota(jnp.int32, (tm, tm), 1)
        lower_scr[...] = jnp.where(ci < ri, 1.0, 0.0).astype(BF16)

    @pl.when(e == 0)
    def _():
        x = x_ref[...]
        hf = _rms(x, g_ref[...])
        h_scr[...] = hf.astype(BF16)
        comb, sel = _router_comb(hf, wr_ref, br_ref)
        gate_scr[...] = comb
        rank = jnp.dot(lower_scr[...], sel.astype(BF16), preferred_element_type=F32)
        rank = jnp.where(sel > 0.0, rank, -1.0)
        rank_scr[...] = rank
        rank_t_scr[...] = rank.T
        cnt = jnp.sum(sel, axis=0, keepdims=True)
        for k in range(N_EXPERTS):
            cnt_smem[k] = cnt[0, k].astype(jnp.int32)
        o_ref[...] = x

    lane = lax.broadcasted_iota(jnp.int32, (1, ROUTER_LANES), 1)
    pick = lane == e
    rank_col = jnp.sum(jnp.where(pick, rank_scr[...], 0.0), axis=-1, keepdims=True)
    gate_col = jnp.sum(jnp.where(pick, gate_scr[...], 0.0), axis=-1, keepdims=True)
    rank_row = rank_t_scr[pl.ds(e, 1), :]
    n_chunks = (cnt_smem[e] + (MOE_CHUNK - 1)) // MOE_CHUNK
    n_pairs = n_chunks // 2

    def run_chunk(first_slot, size):
        base = first_slot.astype(F32)
        slot_col = lax.broadcasted_iota(jnp.int32, (size, 1), 0).astype(F32) + base
        slot_row = lax.broadcasted_iota(jnp.int32, (1, size), 1).astype(F32) + base
        gather = jnp.where(rank_row == slot_col, 1.0, 0.0).astype(BF16)
        xe = jnp.dot(gather, h_scr[...], preferred_element_type=F32).astype(BF16)
        ab = jnp.dot(xe, wgu_ref[...], preferred_element_type=F32)
        m = (jax.nn.silu(ab[:, :ff]) * ab[:, ff:]).astype(BF16)
        ye = jnp.dot(m, wd_ref[...], preferred_element_type=F32).astype(BF16)
        sr = min(MOE_SCATTER_ROWS, tm)
        for rb in range(tm // sr):
            sl = slice(rb * sr, (rb + 1) * sr)
            scatter = jnp.where(rank_col[sl] == slot_row, 1.0, 0.0).astype(BF16)
            o_ref[sl, :] += gate_col[sl] * jnp.dot(scatter, ye, preferred_element_type=F32)

    def pair(c, carry):
        run_chunk(c * (2 * MOE_CHUNK), 2 * MOE_CHUNK)
        return carry

    lax.fori_loop(0, n_pairs, pair, 0)

    @pl.when(n_chunks % 2 == 1)
    def _():
        run_chunk(n_pairs * (2 * MOE_CHUNK), MOE_CHUNK)

    if final_norm:
        @pl.when(e == pl.num_programs(1) - 1)
        def _():
            o_ref[...] = _rms(o_ref[...], gf_ref[...])


def _moe(x, g, wr_pad, br_pad, wgu, wd, final_g):
    rows, d = x.shape
    n_e, ff, _ = wd.shape
    tm = min(MOE_ROW_TILE, rows)
    row = pl.BlockSpec((tm, d), lambda i, e: (i, 0))
    final_norm = final_g is not None
    return pl.pallas_call(
        functools.partial(_moe_body, final_norm=final_norm),
        grid=(rows // tm, n_e),
        in_specs=[row, _full((1, d)), _full((d, ROUTER_LANES)), _full((1, ROUTER_LANES)),
                  pl.BlockSpec((None, d, 2 * ff), lambda i, e: (e, 0, 0)),
                  pl.BlockSpec((None, ff, d), lambda i, e: (e, 0, 0))]
        + [_full((1, d))] * final_norm,
        out_specs=row,
        out_shape=jax.ShapeDtypeStruct((rows, d), F32),
        scratch_shapes=[pltpu.VMEM((tm, d), BF16),
                        pltpu.VMEM((tm, ROUTER_LANES), F32),
                        pltpu.VMEM((tm, ROUTER_LANES), F32),
                        pltpu.VMEM((ROUTER_LANES, tm), F32),
                        pltpu.VMEM((tm, tm), BF16),
                        pltpu.SMEM((N_EXPERTS,), jnp.int32)],
        compiler_params=_params(2),
        name="moe",
    )(x, g, wr_pad, br_pad, wgu, wd, *([final_g] * final_norm))


def _trunk(x, p, past, expert_w):
    b, l, d = x.shape
    depth = p['w_in'].shape[0]
    rows = b * l
    xr = x.reshape(rows, d)
    prompt = past is None
    new = {'k': [], 'v': [], 're': [], 'im': [], 'gv': []}
    sp = p['ssm']
    ws, bs = _gmlp_params(p, min(l, CHUNK))
    if not prompt:
        w = past[0].shape[2]
        kc_t = past[0].transpose(0, 1, 3, 4, 2).reshape(depth, b, KV_W, w)
        vc_t = past[1].transpose(0, 1, 3, 4, 2).reshape(depth, b, KV_W, w)
        k_out = jnp.zeros_like(kc_t)
        v_out = jnp.zeros_like(vc_t)
        h0_t = tuple(s.transpose(0, 2, 3, 1).reshape(depth, SSM_FLAT, b) for s in past[2:])
    for i in range(depth):
        g_mix = p['norm_mix_g'][i].reshape(1, d)
        gv = p['gmlp_v_norm_g'][i].reshape(1, GM_W)
        q, k, v, u, gu, gvn = _inproj(xr, g_mix, p['w_in_bf16'], gv, i)
        u = u.reshape(b, l, SSM_W)
        sinks = p['attn_sinks'][i]
        if prompt:
            o_attn = _attn_prompt(q, k, v, sinks, batch=b, seq=l)
            nw = min(WINDOW, l)
            k_win = k.reshape(b, l, N_KV_HEADS, HEAD_DIM)[:, l - nw:]
            v_win = v.reshape(b, l, N_KV_HEADS, HEAD_DIM)[:, l - nw:]
            o_ssm, s_fin = _ssm(u, None, sp, i, tc=CHUNK)
            s_re = s_fin[:, :SSM_FLAT].reshape(b, SSM_GROUPS, SSM_STATE)
            s_im = s_fin[:, SSM_FLAT:].reshape(b, SSM_GROUPS, SSM_STATE)
        else:
            o_attn, k_out, v_out = _attn_sample(q, k, v, kc_t, vc_t, k_out, v_out, sinks, i,
                                                batch=b, t=l)
            o_ssm, s_fin = _ssm(u, h0_t, sp, i, tc=l)
            s_re, s_im = (s.reshape(SSM_GROUPS, SSM_STATE, b).transpose(2, 0, 1) for s in s_fin)
            new['gv'].append(gvn.reshape(b, l, GM_HEADS, GM_HEAD_DIM))
        xr = _outproj(xr, o_attn, o_ssm.reshape(rows, SSM_W), gu, gvn, ws, bs,
                      p['mix_out_norm_g'][i].reshape(1, d), p['w_out_bf16'], i,
                      chunk=min(l, CHUNK))
        g_ffn = p['norm_ffn_g'][i].reshape(1, d)
        final_g = p['final_norm_g'].reshape(1, d) if i == depth - 1 else None
        j = i // 2
        if i % 2 == 0:
            todo = prompt and i + 1 < depth and j not in expert_w
            xr, cast = _ffn(xr, g_ffn, p['ffn_w_gate_bf16'], p['ffn_w_up_bf16'],
                            p['ffn_w_down_bf16'], j, final_g,
                            (p['moe_w_gate'], p['moe_w_up'], p['moe_w_down'], j) if todo else None)
            if todo:
                expert_w[j] = cast
        else:
            wr = jnp.pad(p['moe_w_router'][j], ((0, 0), (0, ROUTER_LANES - N_EXPERTS)))
            br = jnp.pad(p['moe_b_router'][j], (0, ROUTER_LANES - N_EXPERTS),
                         constant_values=NEG_BIG).reshape(1, ROUTER_LANES)
            if j not in expert_w:
                expert_w[j] = (jnp.concatenate([p['moe_w_gate'][j].astype(BF16),
                                                p['moe_w_up'][j].astype(BF16)], axis=-1),
                               p['moe_w_down'][j].astype(BF16))
            xr = _moe(xr, g_ffn, wr, br, *expert_w[j], final_g)
        if prompt:
            new['k'].append(k_win)
            new['v'].append(v_win)
        new['re'].append(s_re)
        new['im'].append(s_im)
    for name in ('re', 'im', 'gv') + (('k', 'v') if prompt else ()):
        new[name] = jnp.stack(new[name]) if new[name] else None
    if not prompt:
        back = lambda c: c.reshape(depth, b, N_KV_HEADS, HEAD_DIM, w).transpose(0, 1, 4, 2, 3)
        new['k'], new['v'] = back(k_out), back(v_out)
    return xr.reshape(b, l, d), new


def kernel(x_prompt, x_sample, cache_k_win, cache_v_win, state_ssm_re, state_ssm_im,
           norm_mix_g, w_in, attn_sinks, ssm_a_re, ssm_a_im, ssm_log_dt, ssm_b_re, ssm_b_im,
           ssm_c_re, ssm_c_im, ssm_d, ssm_w_glu, ssm_b_glu, gmlp_v_norm_g, gmlp_w_s, gmlp_b_s,
           mix_out_norm_g, w_out, norm_ffn_g, ffn_w_gate, ffn_w_up, ffn_w_down,
           moe_w_router, moe_b_router, moe_w_gate, moe_w_up, moe_w_down, final_norm_g):
    params = {
        'norm_mix_g': norm_mix_g, 'w_in': w_in, 'attn_sinks': attn_sinks,
        'ssm_a_re': ssm_a_re, 'ssm_a_im': ssm_a_im, 'ssm_log_dt': ssm_log_dt,
        'ssm_b_re': ssm_b_re, 'ssm_b_im': ssm_b_im, 'ssm_c_re': ssm_c_re, 'ssm_c_im': ssm_c_im,
        'ssm_d': ssm_d, 'ssm_w_glu': ssm_w_glu, 'ssm_b_glu': ssm_b_glu,
        'gmlp_v_norm_g': gmlp_v_norm_g, 'gmlp_w_s': gmlp_w_s, 'gmlp_b_s': gmlp_b_s,
        'mix_out_norm_g': mix_out_norm_g, 'w_out': w_out, 'norm_ffn_g': norm_ffn_g,
        'ffn_w_gate': ffn_w_gate, 'ffn_w_up': ffn_w_up, 'ffn_w_down': ffn_w_down,
        'moe_w_router': moe_w_router, 'moe_b_router': moe_b_router,
        'moe_w_gate': moe_w_gate, 'moe_w_up': moe_w_up, 'moe_w_down': moe_w_down,
        'final_norm_g': final_norm_g,
    }
    for name in ('w_in', 'w_out', 'ffn_w_gate', 'ffn_w_up', 'ffn_w_down'):
        params[name + '_bf16'] = params[name].astype(BF16)
    params['ssm'] = _ssm_params(params)
    expert_w = {}
    y_p, st_p = _trunk(x_prompt, params, None, expert_w)
    y_s, st_s = _trunk(x_sample, params,
                       (cache_k_win, cache_v_win, state_ssm_re, state_ssm_im), expert_w)
    return (y_p, y_s,
            st_p['k'], st_p['v'], st_p['re'], st_p['im'],
            st_s['k'], st_s['v'], st_s['re'], st_s['im'], st_s['gv'])
```

```python
import functools
import math

import jax
import jax.numpy as jnp
from jax import lax
from jax.experimental import pallas as pl
from jax.experimental.pallas import tpu as pltpu

F32 = jnp.float32
BF16 = jnp.bfloat16

EPS = 1e-6
HEAD_DIM = 64
N_HEADS = 8
N_KV_HEADS = 2
GQA_GROUP = N_HEADS // N_KV_HEADS
ATTN_W = N_HEADS * HEAD_DIM
KV_W = N_KV_HEADS * HEAD_DIM
WINDOW = 128
ATTN_SCALE = 1.0 / math.sqrt(HEAD_DIM)
LOG2E = math.log2(math.e)
SSM_W = 256
SSM_GROUP_CH = 16
SSM_GROUPS = 16
SSM_STATE = 64
SSM_FLAT = SSM_GROUPS * SSM_STATE
GM_W = 256
GM_HEADS = 4
GM_HEAD_DIM = 64
CHUNK = 128
Q_END = ATTN_W
K_END = Q_END + KV_W
V_END = K_END + KV_W
S_END = V_END + SSM_W
IN_COLS = S_END + 2 * GM_W
N_EXPERTS = 8
ROUTER_LANES = 128
NEG_BIG = -1e30

VMEM_LIMIT_BYTES = 56 * 1024 * 1024
ROW_TILE = 512
SUB_ROWS = 128
OUTPROJ_PASS_ROWS = 256
ATTN_Q_BLOCKS = 4
MOE_ROW_TILE = 1024
MOE_CHUNK = 128
MOE_SCATTER_ROWS = 256
SCAN_LANES = 512


def _params(n_axes):
    return pltpu.CompilerParams(
        dimension_semantics=("arbitrary",) * n_axes,
        vmem_limit_bytes=VMEM_LIMIT_BYTES)


def _rms(x, g):
    return x * lax.rsqrt(jnp.mean(x * x, axis=-1, keepdims=True) + EPS) * g


def _bdot(a, b):
    return jnp.dot(a.astype(BF16), b.astype(BF16), preferred_element_type=F32)


def _full(shape):
    return pl.BlockSpec(shape, lambda *_: (0,) * len(shape))


def _group_mean(a, n_groups, width):
    lane = lax.broadcasted_iota(jnp.int32, (1, n_groups * width), 1)
    out = jnp.zeros_like(a)
    for h in range(n_groups):
        m = (lane >= h * width) & (lane < (h + 1) * width)
        s = jnp.sum(jnp.where(m, a, 0.0), axis=-1, keepdims=True) * (1.0 / width)
        out = jnp.where(m, s, out)
    return out


def _inproj_body(x_ref, g_ref, w_ref, gv_ref, q_ref, k_ref, v_ref, u_ref, gu_ref, gvn_ref):
    for r0 in range(0, x_ref.shape[0], SUB_ROWS):
        rs = slice(r0, r0 + SUB_ROWS)
        h = _rms(x_ref[rs, :], g_ref[...])
        z = jnp.dot(h.astype(BF16), w_ref[...], preferred_element_type=F32)
        q_ref[rs, :] = (z[:, :Q_END] * (ATTN_SCALE * LOG2E)).astype(BF16)
        k_ref[rs, :] = z[:, Q_END:K_END]
        v_ref[rs, :] = z[:, K_END:V_END]
        u_ref[rs, :] = z[:, V_END:S_END]
        g = jax.nn.gelu(z[:, S_END:])
        gu_ref[rs, :] = g[:, :GM_W]
        vv = g[:, GM_W:]
        mu = _group_mean(vv, GM_HEADS, GM_HEAD_DIM)
        var = _group_mean(jnp.square(vv - mu), GM_HEADS, GM_HEAD_DIM)
        gvn_ref[rs, :] = (vv - mu) * lax.rsqrt(var + EPS) * gv_ref[...]


def _layer(shape, j):
    return pl.BlockSpec((None,) + tuple(shape), lambda *_: (j,) + (0,) * len(shape))


def _inproj(x, g, w_bf16, gv, j):
    rows, d = x.shape
    tm = min(ROW_TILE, rows)
    nt = rows // tm
    row = lambda w: pl.BlockSpec((tm, w), lambda i: (i, 0))
    return pl.pallas_call(
        _inproj_body,
        grid=(nt,),
        in_specs=[row(d), _full((1, d)), _layer((d, IN_COLS), j), _full((1, GM_W))],
        out_specs=[row(ATTN_W), row(KV_W), row(KV_W), row(SSM_W), row(GM_W), row(GM_W)],
        out_shape=[jax.ShapeDtypeStruct((rows, ATTN_W), BF16),
                   jax.ShapeDtypeStruct((rows, KV_W), F32),
                   jax.ShapeDtypeStruct((rows, KV_W), F32),
                   jax.ShapeDtypeStruct((rows, SSM_W), F32),
                   jax.ShapeDtypeStruct((rows, GM_W), F32),
                   jax.ShapeDtypeStruct((rows, GM_W), F32)],
        compiler_params=_params(1),
        name="inproj",
    )(x, g, w_bf16, gv)


def _sink_softmax(s, mask, sink):
    s = jnp.where(mask, s, -jnp.inf)
    m = jnp.maximum(jnp.max(s, axis=-1, keepdims=True), sink)
    e = jnp.exp2(s - m)
    return e / (jnp.sum(e, axis=-1, keepdims=True) + jnp.exp2(sink - m))


def _head_pair_operands(x):
    low = lax.broadcasted_iota(jnp.int32, (1, KV_W), 1) < HEAD_DIM
    swapped = pltpu.roll(x, HEAD_DIM, axis=1)
    keep = lambda a, in_low: jnp.where(low == in_low, a, 0.0).astype(BF16)
    return ((keep(x, True), keep(swapped, False)),
            (keep(swapped, True), keep(x, False)))


def _attn_prompt_body(sink_ref, q_ref, kc_ref, kp_ref, vc_ref, vp_ref, o_ref, *, nq):
    n = pl.program_id(1)
    qi = lax.broadcasted_iota(jnp.int32, (WINDOW, 2 * WINDOW), 0)
    si = lax.broadcasted_iota(jnp.int32, (WINDOW, 2 * WINDOW), 1)
    dist = WINDOW + qi - si
    band = (dist >= 0) & (dist < WINDOW)
    first = band & ((n > 0) | (si >= WINDOW))
    k_ops = _head_pair_operands(jnp.concatenate([kp_ref[...], kc_ref[...]], axis=0))
    v_ops = _head_pair_operands(jnp.concatenate([vp_ref[...], vc_ref[...]], axis=0))
    pair_w = 2 * HEAD_DIM
    for j in range(nq):
        mask = first if j == 0 else band
        q_rows = slice(j * WINDOW, (j + 1) * WINDOW)
        kv_rows = slice(j * WINDOW, (j + 2) * WINDOW)
        for hp in range(N_HEADS // 2):
            kh = (2 * hp) // GQA_GROUP
            qp = q_ref[q_rows, hp * pair_w:(hp + 1) * pair_w]
            acc = None
            for par in range(2):
                sink = sink_ref[2 * hp + par] * LOG2E
                s = lax.dot_general(qp, k_ops[kh][par][kv_rows], (((1,), (1,)), ((), ())),
                                    preferred_element_type=F32)
                s = jnp.where(mask, s, -jnp.inf)
                m = jnp.maximum(jnp.max(s, axis=-1, keepdims=True), sink)
                e = jnp.exp2(s - m)
                den = jnp.sum(e, axis=-1, keepdims=True) + jnp.exp2(sink - m)
                pv = jnp.dot(e.astype(BF16), v_ops[kh][par][kv_rows],
                             preferred_element_type=F32)
                pv = pv * (1.0 / den)
                acc = pv if acc is None else acc + pv
            o_ref[q_rows, hp * pair_w:(hp + 1) * pair_w] = acc


def _attn_prompt(q, k, v, sinks, *, batch, seq):
    nq = ATTN_Q_BLOCKS
    nb = seq // WINDOW
    steps = nb // nq
    cur = lambda w: pl.BlockSpec((nq * WINDOW, w), lambda b, n: (b * steps + n, 0))
    prev = lambda w: pl.BlockSpec(
        (WINDOW, w), lambda b, n: (b * nb + jnp.maximum(n * nq - 1, 0), 0))
    return pl.pallas_call(
        functools.partial(_attn_prompt_body, nq=nq),
        grid=(batch, steps),
        in_specs=[pl.BlockSpec(memory_space=pltpu.SMEM),
                  cur(ATTN_W), cur(KV_W), prev(KV_W), cur(KV_W), prev(KV_W)],
        out_specs=cur(ATTN_W),
        out_shape=jax.ShapeDtypeStruct((batch * seq, ATTN_W), F32),
        compiler_params=_params(2),
        name="attn_prompt",
    )(sinks, q, k, k, v, v)


def _attn_sample_body(sink_ref, q_ref, k_ref, v_ref, kc_ref, vc_ref, kin_ref, vin_ref,
                      o_ref, ko_ref, vo_ref, *, bb, t):
    del kin_ref, vin_ref
    w = kc_ref.shape[-1]
    pair_w = 2 * HEAD_DIM
    lane = lax.broadcasted_iota(jnp.int32, (1, w), 1)
    new_col = lane < t
    k_new_t = k_ref[...].T
    v_new_t = v_ref[...].T
    qf = q_ref[...].astype(F32)
    unit = 2 * t
    upper = lax.broadcasted_iota(jnp.int32, (unit, 1), 0) < t
    zeros_half = jnp.zeros((HEAD_DIM, 2 * w), BF16)
    placed = lambda a, par: jnp.concatenate((a, zeros_half) if par == 0 else (zeros_half, a), axis=0)
    scores, sinks, values = [], [], []
    for b in range(bb):
        shift = (w - b * t) % w
        k_cols = jnp.where(new_col, pltpu.roll(k_new_t, shift, axis=1), 0.0)
        v_cols = jnp.where(new_col, pltpu.roll(v_new_t, shift, axis=1), 0.0)
        k_old = kc_ref[b]
        v_old = vc_ref[b]
        ko_ref[b] = pltpu.roll(jnp.where(new_col, k_cols, k_old), w - t, axis=1)
        vo_ref[b] = pltpu.roll(jnp.where(new_col, v_cols, v_old), w - t, axis=1)
        k_all = jnp.concatenate([k_old, k_cols], axis=1).astype(BF16)
        v_all = jnp.concatenate([v_old, v_cols], axis=1).astype(BF16)
        for kh in range(N_KV_HEADS):
            kv_sl = slice(kh * HEAD_DIM, (kh + 1) * HEAD_DIM)
            q2 = jnp.concatenate(
                [qf[b * t:(b + 1) * t, (2 * kh + c) * pair_w:(2 * kh + c + 1) * pair_w]
                 for c in range(2)], axis=0).astype(BF16)
            for par in range(2):
                h_a = GQA_GROUP * kh + par
                sinks.append(jnp.where(upper, sink_ref[h_a], sink_ref[h_a + 2]) * LOG2E)
                scores.append(jnp.dot(q2, placed(k_all[kv_sl], par),
                                      preferred_element_type=F32))
                values.append(placed(v_all[kv_sl], par))
    s = jnp.concatenate(scores, axis=0)
    ti = lax.broadcasted_iota(jnp.int32, s.shape, 0) % t
    si = lax.broadcasted_iota(jnp.int32, s.shape, 1)
    dist = w + ti - si
    p = _sink_softmax(s, (dist >= 0) & (dist < WINDOW), jnp.concatenate(sinks, axis=0)).astype(BF16)
    for b in range(bb):
        for kh in range(N_KV_HEADS):
            acc = None
            for par in range(2):
                u = (b * N_KV_HEADS + kh) * 2 + par
                pv = lax.dot_general(p[u * unit:(u + 1) * unit], values[u],
                                     (((1,), (1,)), ((), ())), preferred_element_type=F32)
                acc = pv if acc is None else acc + pv
            for c in range(2):
                o_ref[b * t:(b + 1) * t, (2 * kh + c) * pair_w:(2 * kh + c + 1) * pair_w] = (
                    acc[c * t:(c + 1) * t])


def _attn_sample(q, k, v, kc_t, vc_t, k_out, v_out, sinks, layer, *, batch, t):
    w = kc_t.shape[-1]
    bb = w // t
    row = lambda c: pl.BlockSpec((bb * t, c), lambda i: (i, 0))
    buf = pl.BlockSpec((None, bb, KV_W, w), lambda i: (layer, i, 0, 0))
    anywhere = pl.BlockSpec(memory_space=pl.ANY)
    return pl.pallas_call(
        functools.partial(_attn_sample_body, bb=bb, t=t),
        grid=(batch // bb,),
        in_specs=[pl.BlockSpec(memory_space=pltpu.SMEM),
                  row(ATTN_W), row(KV_W), row(KV_W), buf, buf, anywhere, anywhere],
        out_specs=[row(ATTN_W), buf, buf],
        out_shape=[jax.ShapeDtypeStruct((batch * t, ATTN_W), F32),
                   jax.ShapeDtypeStruct(k_out.shape, F32),
                   jax.ShapeDtypeStruct(v_out.shape, F32)],
        input_output_aliases={6: 1, 7: 2},
        compiler_params=_params(1),
        name="attn_sample",
    )(sinks, q, k, v, kc_t, vc_t, k_out, v_out)


def _ssm_body(u_ref, *refs, r, tc, has_h0, n_cast):
    h0_refs, refs = refs[:2 * has_h0], refs[2 * has_h0:]
    (are_ref, aim_ref, ldt_ref, bre_ref, bim_ref, cre_ref, cim_ref, d_ref, wglu_ref,
     bglu_ref), refs = refs[:10], refs[10:]
    cast_in, (o_ref, s_ref), refs = refs[:n_cast], refs[n_cast:n_cast + 2], refs[n_cast + 2:]
    cast_out, (abar_scr, bbar_scr, h_scr, *x_scrs) = refs[:n_cast], refs[n_cast:]
    c = pl.program_id(0)
    for src, dst in zip(cast_in, cast_out):
        dst[...] = src[...].astype(BF16)

    @pl.when(c == 0)
    def _():
        ar = are_ref[...]
        ai = aim_ref[...]
        dt = jnp.exp(ldt_ref[...])
        decay = jnp.exp(dt * ar)
        abr = decay * jnp.cos(dt * ai)
        abi = decay * jnp.sin(dt * ai)
        den = ar * ar + ai * ai
        nr = abr - 1.0
        fr = (nr * ar + abi * ai) / den
        fi = (abi * ar - nr * ai) / den
        abar_scr[0:1, :] = abr
        abar_scr[1:2, :] = abi
        br = bre_ref[...]
        bi = bim_ref[...]
        bbar_scr[:, :SSM_FLAT] = (fr * br - fi * bi).astype(BF16)
        bbar_scr[:, SSM_FLAT:] = (fr * bi + fi * br).astype(BF16)
        if has_h0:
            h_scr[:, :SSM_FLAT] = h0_refs[0][...].T
            h_scr[:, SSM_FLAT:] = h0_refs[1][...].T
        else:
            h_scr[...] = jnp.zeros_like(h_scr)

    u = jnp.swapaxes(u_ref[...], 0, 1).reshape(tc * r, SSM_W)
    ub = u.astype(BF16)

    y = None
    for lc in range(SSM_FLAT // SCAN_LANES):
        lanes = slice(lc * SCAN_LANES, (lc + 1) * SCAN_LANES)
        re_sl = lanes
        im_sl = slice(SSM_FLAT + lc * SCAN_LANES, SSM_FLAT + (lc + 1) * SCAN_LANES)
        xr_scr, xi_scr = x_scrs[2 * lc], x_scrs[2 * lc + 1]
        xr_scr[...] = jnp.dot(ub, bbar_scr[:, re_sl], preferred_element_type=F32)
        xi_scr[...] = jnp.dot(ub, bbar_scr[:, im_sl], preferred_element_type=F32)
        ar = jnp.broadcast_to(abar_scr[0:1, lanes], (8, SCAN_LANES))
        ai = jnp.broadcast_to(abar_scr[1:2, lanes], (8, SCAN_LANES))
        for rg in range(r // 8):
            rg_sl = slice(rg * 8, (rg + 1) * 8)
            hr, hi = h_scr[rg_sl, re_sl], h_scr[rg_sl, im_sl]
            for t in range(tc):
                rows = slice(t * r + rg * 8, t * r + rg * 8 + 8)
                hr, hi = (ar * hr - ai * hi + xr_scr[rows, :],
                          ar * hi + ai * hr + xi_scr[rows, :])
                xr_scr[rows, :] = hr
                xi_scr[rows, :] = hi
            h_scr[rg_sl, re_sl] = hr
            h_scr[rg_sl, im_sl] = hi
        part = (jnp.dot(xr_scr[...].astype(BF16), cre_ref[lanes, :], preferred_element_type=F32)
                - jnp.dot(xi_scr[...].astype(BF16), cim_ref[lanes, :],
                          preferred_element_type=F32))
        y = part if y is None else y + part
    y = jax.nn.gelu(y + d_ref[...] * u)
    gl = jnp.dot(y.astype(BF16), wglu_ref[...], preferred_element_type=F32) + bglu_ref[...]
    o_ref[...] = jnp.swapaxes((y * jax.nn.sigmoid(gl)).reshape(tc, r, SSM_W), 0, 1)

    @pl.when(c == pl.num_programs(0) - 1)
    def _():
        if has_h0:
            s_ref[0] = h_scr[:, :SSM_FLAT].T
            s_ref[1] = h_scr[:, SSM_FLAT:].T
        else:
            s_ref[...] = h_scr[...]


def _cast_slabs(stacks, layer, steps):
    args, in_specs, out_specs, out_shapes = [], [], [], []
    for w in stacks:
        n_l, rows, cols = w.shape
        slab = rows // steps
        args.append(w.reshape(n_l * rows, cols))
        in_specs.append(pl.BlockSpec((slab, cols), lambda i: (layer * steps + i, 0)))
        out_specs.append(pl.BlockSpec((slab, cols), lambda i: (i, 0)))
        out_shapes.append(jax.ShapeDtypeStruct((rows, cols), BF16))
    return args, in_specs, out_specs, out_shapes


def _ssm(u, h0_t, sp, layer, *, tc, cast=None):
    r, steps, _ = u.shape
    blk = tc * r
    has_h0 = h0_t is not None
    seq_blk = pl.BlockSpec((r, tc, SSM_W), lambda c: (0, c, 0))
    state_shape = (2, SSM_FLAT, r) if has_h0 else (r, 2 * SSM_FLAT)
    c_args, c_in, c_out, c_shapes = _cast_slabs(*cast, steps // tc) if cast else ([], [], [], [])
    out = pl.pallas_call(
        functools.partial(_ssm_body, r=r, tc=tc, has_h0=has_h0, n_cast=len(c_args)),
        grid=(steps // tc,),
        in_specs=[seq_blk]
        + [_layer((SSM_FLAT, r), layer)] * (2 * has_h0)
        + [_layer((1, SSM_FLAT), layer)] * 3
        + [_layer((SSM_W, SSM_FLAT), layer)] * 2
        + [_layer((SSM_FLAT, SSM_W), layer)] * 2
        + [_layer((1, SSM_W), layer), _layer((SSM_W, SSM_W), layer), _layer((1, SSM_W), layer)]
        + c_in,
        out_specs=[seq_blk, _full(state_shape)] + c_out,
        out_shape=[jax.ShapeDtypeStruct((r, steps, SSM_W), F32),
                   jax.ShapeDtypeStruct(state_shape, F32)] + c_shapes,
        scratch_shapes=[pltpu.VMEM((2, SSM_FLAT), F32),
                        pltpu.VMEM((SSM_W, 2 * SSM_FLAT), BF16),
                        pltpu.VMEM((r, 2 * SSM_FLAT), F32)]
        + [pltpu.VMEM((blk, SCAN_LANES), F32)] * (2 * SSM_FLAT // SCAN_LANES),
        compiler_params=_params(1),
        name="ssm",
    )(u, *(h0_t if has_h0 else ()), sp['a_re'], sp['a_im'], sp['log_dt'], sp['b_re'], sp['b_im'],
      sp['c_re'], sp['c_im'], sp['d'], sp['w_glu'], sp['b_glu'], *c_args)
    return out[0], out[1], tuple(out[2:])


def _ssm_params(p):
    eye = jnp.eye(SSM_GROUPS, dtype=F32)
    depth = p['ssm_a_re'].shape[0]

    def b_blockdiag(b):
        return jnp.einsum('lgpc,gh->lgchp', b, eye).reshape(depth, SSM_W, SSM_FLAT)

    def c_blockdiag(c):
        return jnp.einsum('lgcp,gh->lgphc', c, eye).reshape(depth, SSM_FLAT, SSM_W)

    return {
        'a_re': p['ssm_a_re'].reshape(depth, 1, SSM_FLAT),
        'a_im': p['ssm_a_im'].reshape(depth, 1, SSM_FLAT),
        'log_dt': jnp.repeat(p['ssm_log_dt'], SSM_STATE, axis=1).reshape(depth, 1, SSM_FLAT),
        'b_re': b_blockdiag(p['ssm_b_re']),
        'b_im': b_blockdiag(p['ssm_b_im']),
        'c_re': c_blockdiag(p['ssm_c_re']).astype(BF16),
        'c_im': c_blockdiag(p['ssm_c_im']).astype(BF16),
        'd': p['ssm_d'].reshape(depth, 1, SSM_W),
        'w_glu': p['ssm_w_glu'].astype(BF16),
        'b_glu': p['ssm_b_glu'].reshape(depth, 1, SSM_W),
    }


def _outproj_body(x_ref, oa_ref, os_ref, gu_ref, gvn_ref, ws_ref, bs_ref, go_ref, w_ref, o_ref,
                  *, chunk):
    tm = x_ref.shape[0]
    ri = lax.broadcasted_iota(jnp.int32, (CHUNK, CHUNK), 0)
    ci = lax.broadcasted_iota(jnp.int32, (CHUNK, CHUNK), 1)
    causal = (ri // chunk == ci // chunk) & (ci <= ri)
    lane = lax.broadcasted_iota(jnp.int32, (1, GM_W), 1)
    w_heads = [jnp.where(causal, ws_ref[h], 0.0).astype(BF16) for h in range(GM_HEADS)]
    go = go_ref[...]
    per_pass = OUTPROJ_PASS_ROWS // CHUNK
    for r0 in range(0, tm, OUTPROJ_PASS_ROWS):
        rs = slice(r0, r0 + OUTPROJ_PASS_ROWS)
        zs = []
        for cblk in range(per_pass):
            vn = gvn_ref[r0 + cblk * CHUNK:r0 + (cblk + 1) * CHUNK, :].astype(BF16)
            z = bs_ref[...]
            for h in range(GM_HEADS):
                z_h = jnp.dot(w_heads[h], vn, preferred_element_type=F32)
                head = (lane >= h * GM_HEAD_DIM) & (lane < (h + 1) * GM_HEAD_DIM)
                z = z + jnp.where(head, z_h, 0.0)
            zs.append(z)
        o = jnp.concatenate([
            _rms(oa_ref[rs, :], go[:, :ATTN_W]),
            _rms(os_ref[rs, :], go[:, ATTN_W:ATTN_W + SSM_W]),
            _rms(gu_ref[rs, :] * jnp.concatenate(zs, axis=0), go[:, ATTN_W + SSM_W:])], axis=-1)
        o_ref[rs, :] = x_ref[rs, :] + jnp.dot(o.astype(BF16), w_ref[...],
                                              preferred_element_type=F32)


def _outproj(x, o_attn, o_ssm, gu, gvn, ws, bs, g_out, w_bf16, j, *, chunk):
    rows, d = x.shape
    tm = min(ROW_TILE, rows)
    row = lambda w: pl.BlockSpec((tm, w), lambda i: (i, 0))
    return pl.pallas_call(
        functools.partial(_outproj_body, chunk=chunk),
        grid=(rows // tm,),
        in_specs=[row(d), row(ATTN_W), row(SSM_W), row(GM_W), row(GM_W),
                  _layer((GM_HEADS, CHUNK, CHUNK), j), _layer((CHUNK, GM_W), j),
                  _full((1, d)), _layer((d, d), j)],
        out_specs=row(d),
        out_shape=jax.ShapeDtypeStruct((rows, d), F32),
        compiler_params=_params(1),
        name="outproj",
    )(x, o_attn, o_ssm, gu, gvn, ws, bs, g_out, w_bf16)


def _gmlp_params(p, chunk):
    ws = p['gmlp_w_s'][:, :, :chunk, :chunk]
    bs = p['gmlp_b_s'][:, :, :chunk]
    if chunk < CHUNK:
        pick = (jnp.arange(CHUNK)[:, None] % chunk == jnp.arange(chunk)[None, :]).astype(F32)
        ws = jnp.einsum('rc,lhcd,sd->lhrs', pick, ws, pick, precision=lax.Precision.HIGHEST)
        bs = jnp.einsum('rc,lhc->lhr', pick, bs, precision=lax.Precision.HIGHEST)
    bs = jnp.repeat(bs.transpose(0, 2, 1), GM_HEAD_DIM, axis=2)
    return ws, bs


def _swiglu_cols(h, wg_ref, wu_ref, wd_ref, n_chunks):
    ff = wg_ref.shape[-1]
    fc = ff // n_chunks
    acc = None
    for j in range(n_chunks):
        sl = slice(j * fc, (j + 1) * fc)
        a = jnp.dot(h, wg_ref[:, sl], preferred_element_type=F32)
        b = jnp.dot(h, wu_ref[:, sl], preferred_element_type=F32)
        m = (jax.nn.silu(a) * b).astype(BF16)
        y = jnp.dot(m, wd_ref[sl, :], preferred_element_type=F32)
        acc = y if acc is None else acc + y
    return acc


def _ffn_body(x_ref, g_ref, wg_ref, wu_ref, wd_ref, *rest, final_norm, n_cast):
    gf_ref = rest[0] if final_norm else None
    rest = rest[int(final_norm):]
    cast_in, o_ref, cast_out = rest[:n_cast], rest[n_cast], rest[n_cast + 1:]
    x = x_ref[...]
    h = _rms(x, g_ref[...]).astype(BF16)
    y = x + _swiglu_cols(h, wg_ref, wu_ref, wd_ref, 2)
    o_ref[...] = _rms(y, gf_ref[...]) if final_norm else y
    if n_cast:
        eg_ref, eu_ref, ed_ref = cast_in
        egu_ref, edo_ref = cast_out
        ffe = eg_ref.shape[-1]
        egu_ref[:, :ffe] = eg_ref[...].astype(BF16)
        egu_ref[:, ffe:] = eu_ref[...].astype(BF16)
        edo_ref[...] = ed_ref[...].astype(BF16)


def _ffn(x, g, wg, wu, wd, final_g, experts=None):
    rows, d = x.shape
    ff = wg.shape[-1]
    tm = min(ROW_TILE, rows)
    steps = rows // tm
    row = pl.BlockSpec((tm, d), lambda i: (i, 0))
    once = lambda shape: pl.BlockSpec(shape, lambda i: (0, 0), pipeline_mode=pl.Buffered(1))
    final_norm = final_g is not None
    cast_args, cast_in, cast_out, cast_shapes = [], [], [], []
    if experts is not None:
        eg, eu, ed, le = experts
        n_l, n_e, _, ffe = eg.shape
        gu_rows, d_rows = n_e * d // steps, n_e * ffe // steps
        slab = lambda r, c: pl.BlockSpec((r, c), lambda i: (le * steps + i, 0))
        cast_args = [eg.reshape(n_l * n_e * d, ffe), eu.reshape(n_l * n_e * d, ffe),
                     ed.reshape(n_l * n_e * ffe, d)]
        cast_in = [slab(gu_rows, ffe), slab(gu_rows, ffe), slab(d_rows, d)]
        cast_out = [pl.BlockSpec((gu_rows, 2 * ffe), lambda i: (i, 0)),
                    pl.BlockSpec((d_rows, d), lambda i: (i, 0))]
        cast_shapes = [jax.ShapeDtypeStruct((n_e * d, 2 * ffe), BF16),
                       jax.ShapeDtypeStruct((n_e * ffe, d), BF16)]
    out = pl.pallas_call(
        functools.partial(_ffn_body, final_norm=final_norm, n_cast=len(cast_in)),
        grid=(steps,),
        in_specs=[row, _full((1, d)), once((d, ff)), once((d, ff)), once((ff, d))]
        + [_full((1, d))] * final_norm + cast_in,
        out_specs=[row] + cast_out,
        out_shape=[jax.ShapeDtypeStruct((rows, d), F32)] + cast_shapes,
        compiler_params=_params(1),
        name="ffn",
    )(x, g, wg, wu, wd, *([final_g] * final_norm), *cast_args)
    if experts is None:
        return out[0], None
    return out[0], (out[1].reshape(n_e, d, 2 * ffe), out[2].reshape(n_e, ffe, d))


def _split_bf16(x):
    hi = x.astype(BF16)
    lo = (x - hi.astype(F32)).astype(BF16)
    return hi, lo


def _router_comb(hf, wr_ref, br_ref):
    h_hi, h_lo = _split_bf16(hf)
    w_hi, w_lo = _split_bf16(wr_ref[...])
    dot = lambda a, b: jnp.dot(a, b, preferred_element_type=F32)
    logits = dot(h_hi, w_hi) + (dot(h_hi, w_lo) + dot(h_lo, w_hi)) + br_ref[...]
    lane = lax.broadcasted_iota(jnp.int32, logits.shape, 1).astype(F32)
    far = float(ROUTER_LANES)
    m1 = jnp.max(logits, axis=-1, keepdims=True)
    i1 = jnp.min(jnp.where(logits == m1, lane, far), axis=-1, keepdims=True)
    rest = jnp.where(lane == i1, -jnp.inf, logits)
    m2 = jnp.max(rest, axis=-1, keepdims=True)
    i2 = jnp.min(jnp.where(rest == m2, lane, far), axis=-1, keepdims=True)
    e2 = jnp.exp(m2 - m1)
    den = 1.0 + e2
    comb = jnp.where(lane == i1, 1.0 / den, 0.0) + jnp.where(lane == i2, e2 / den, 0.0)
    sel = jnp.where((lane == i1) | (lane == i2), 1.0, 0.0)
    return comb, sel


def _moe_body(x_ref, g_ref, wr_ref, br_ref, wgu_ref, wd_ref, *rest, final_norm):
    gf_ref = rest[0] if final_norm else None
    o_ref, h_scr, gate_scr, rank_scr, rank_t_scr, lower_scr, cnt_smem = rest[int(final_norm):]
    i = pl.program_id(0)
    e = pl.program_id(1)
    tm = x_ref.shape[0]
    ff = wd_ref.shape[0]

    @pl.when((i == 0) & (e == 0))
    def _():
        ri = lax.broadcasted_iota(jnp.int32, (tm, tm), 0)
        ci = lax.broadcasted_iota(jnp.int32, (tm, tm), 1)
        lower_scr[...] = jnp.where(ci < ri, 1.0, 0.0).astype(BF16)

    @pl.when(e == 0)
    def _():
        x = x_ref[...]
        hf = _rms(x, g_ref[...])
        h_scr[...] = hf.astype(BF16)
        comb, sel = _router_comb(hf, wr_ref, br_ref)
        gate_scr[...] = comb
        rank = jnp.dot(lower_scr[...], sel.astype(BF16), preferred_element_type=F32)
        rank = jnp.where(sel > 0.0, rank, -1.0)
        rank_scr[...] = rank
        rank_t_scr[...] = rank.T
        cnt = jnp.sum(sel, axis=0, keepdims=True)
        for k in range(N_EXPERTS):
            cnt_smem[k] = cnt[0, k].astype(jnp.int32)
        o_ref[...] = x

    lane = lax.broadcasted_iota(jnp.int32, (1, ROUTER_LANES), 1)
    pick = lane == e
    rank_col = jnp.sum(jnp.where(pick, rank_scr[...], 0.0), axis=-1, keepdims=True)
    gate_col = jnp.sum(jnp.where(pick, gate_scr[...], 0.0), axis=-1, keepdims=True)
    rank_row = rank_t_scr[pl.ds(e, 1), :]
    n_chunks = (cnt_smem[e] + (MOE_CHUNK - 1)) // MOE_CHUNK
    n_pairs = n_chunks // 2

    def run_chunk(first_slot, size):
        base = first_slot.astype(F32)
        slot_col = lax.broadcasted_iota(jnp.int32, (size, 1), 0).astype(F32) + base
        slot_row = lax.broadcasted_iota(jnp.int32, (1, size), 1).astype(F32) + base
        gather = jnp.where(rank_row == slot_col, 1.0, 0.0).astype(BF16)
        xe = jnp.dot(gather, h_scr[...], preferred_element_type=F32).astype(BF16)
        ab = jnp.dot(xe, wgu_ref[...], preferred_element_type=F32)
        m = (jax.nn.silu(ab[:, :ff]) * ab[:, ff:]).astype(BF16)
        ye = jnp.dot(m, wd_ref[...], preferred_element_type=F32).astype(BF16)
        sr = min(MOE_SCATTER_ROWS, tm)
        for rb in range(tm // sr):
            sl = slice(rb * sr, (rb + 1) * sr)
            scatter = jnp.where(rank_col[sl] == slot_row, 1.0, 0.0).astype(BF16)
            o_ref[sl, :] += gate_col[sl] * jnp.dot(scatter, ye, preferred_element_type=F32)

    def pair(c, carry):
        run_chunk(c * (2 * MOE_CHUNK), 2 * MOE_CHUNK)
        return carry

    lax.fori_loop(0, n_pairs, pair, 0)

    @pl.when(n_chunks % 2 == 1)
    def _():
        run_chunk(n_pairs * (2 * MOE_CHUNK), MOE_CHUNK)

    if final_norm:
        @pl.when(e == pl.num_programs(1) - 1)
        def _():
            o_ref[...] = _rms(o_ref[...], gf_ref[...])


def _moe(x, g, wr_pad, br_pad, wgu, wd, final_g):
    rows, d = x.shape
    n_e, ff, _ = wd.shape
    tm = min(MOE_ROW_TILE, rows)
    row = pl.BlockSpec((tm, d), lambda i, e: (i, 0))
    final_norm = final_g is not None
    return pl.pallas_call(
        functools.partial(_moe_body, final_norm=final_norm),
        grid=(rows // tm, n_e),
        in_specs=[row, _full((1, d)), _full((d, ROUTER_LANES)), _full((1, ROUTER_LANES)),
                  pl.BlockSpec((None, d, 2 * ff), lambda i, e: (e, 0, 0)),
                  pl.BlockSpec((None, ff, d), lambda i, e: (e, 0, 0))]
        + [_full((1, d))] * final_norm,
        out_specs=row,
        out_shape=jax.ShapeDtypeStruct((rows, d), F32),
        scratch_shapes=[pltpu.VMEM((tm, d), BF16),
                        pltpu.VMEM((tm, ROUTER_LANES), F32),
                        pltpu.VMEM((tm, ROUTER_LANES), F32),
                        pltpu.VMEM((ROUTER_LANES, tm), F32),
                        pltpu.VMEM((tm, tm), BF16),
                        pltpu.SMEM((N_EXPERTS,), jnp.int32)],
        compiler_params=_params(2),
        name="moe",
    )(x, g, wr_pad, br_pad, wgu, wd, *([final_g] * final_norm))


def _trunk(x, p, past, dense_w, expert_w):
    b, l, d = x.shape
    depth = p['w_in'].shape[0]
    rows = b * l
    xr = x.reshape(rows, d)
    prompt = past is None
    new = {'k': [], 'v': [], 're': [], 'im': [], 'gv': []}
    sp = p['ssm']
    ws, bs = _gmlp_params(p, min(l, CHUNK))
    if not prompt:
        w = past[0].shape[2]
        kc_t = past[0].transpose(0, 1, 3, 4, 2).reshape(depth, b, KV_W, w)
        vc_t = past[1].transpose(0, 1, 3, 4, 2).reshape(depth, b, KV_W, w)
        k_out = jnp.zeros_like(kc_t)
        v_out = jnp.zeros_like(vc_t)
        h0_t = tuple(s.transpose(0, 2, 3, 1).reshape(depth, SSM_FLAT, b) for s in past[2:])
    for i in range(depth):
        g_mix = p['norm_mix_g'][i].reshape(1, d)
        gv = p['gmlp_v_norm_g'][i].reshape(1, GM_W)
        q, k, v, u, gu, gvn = _inproj(xr, g_mix, p['w_in_bf16'], gv, i)
        u = u.reshape(b, l, SSM_W)
        sinks = p['attn_sinks'][i]
        if prompt:
            o_attn = _attn_prompt(q, k, v, sinks, batch=b, seq=l)
            nw = min(WINDOW, l)
            k_win = k.reshape(b, l, N_KV_HEADS, HEAD_DIM)[:, l - nw:]
            v_win = v.reshape(b, l, N_KV_HEADS, HEAD_DIM)[:, l - nw:]
            dense_todo = i % 2 == 0 and i // 2 not in dense_w
            o_ssm, s_fin, cast = _ssm(
                u, None, sp, i, tc=CHUNK,
                cast=((p['ffn_w_gate'], p['ffn_w_up'], p['ffn_w_down']), i // 2)
                if dense_todo else None)
            if dense_todo:
                dense_w[i // 2] = cast
            s_re = s_fin[:, :SSM_FLAT].reshape(b, SSM_GROUPS, SSM_STATE)
            s_im = s_fin[:, SSM_FLAT:].reshape(b, SSM_GROUPS, SSM_STATE)
        else:
            o_attn, k_out, v_out = _attn_sample(q, k, v, kc_t, vc_t, k_out, v_out, sinks, i,
                                                batch=b, t=l)
            o_ssm, s_fin, _ = _ssm(u, h0_t, sp, i, tc=l)
            s_re, s_im = (s.reshape(SSM_GROUPS, SSM_STATE, b).transpose(2, 0, 1) for s in s_fin)
            new['gv'].append(gvn.reshape(b, l, GM_HEADS, GM_HEAD_DIM))
        xr = _outproj(xr, o_attn, o_ssm.reshape(rows, SSM_W), gu, gvn, ws, bs,
                      p['mix_out_norm_g'][i].reshape(1, d), p['w_out_bf16'], i,
                      chunk=min(l, CHUNK))
        g_ffn = p['norm_ffn_g'][i].reshape(1, d)
        final_g = p['final_norm_g'].reshape(1, d) if i == depth - 1 else None
        j = i // 2
        if i % 2 == 0:
            todo = prompt and i + 1 < depth and j not in expert_w
            if j not in dense_w:
                dense_w[j] = tuple(p[n][j].astype(BF16)
                                   for n in ('ffn_w_gate', 'ffn_w_up', 'ffn_w_down'))
            xr, cast = _ffn(xr, g_ffn, *dense_w[j], final_g,
                            (p['moe_w_gate'], p['moe_w_up'], p['moe_w_down'], j) if todo else None)
            if todo:
                expert_w[j] = cast
        else:
            wr = jnp.pad(p['moe_w_router'][j], ((0, 0), (0, ROUTER_LANES - N_EXPERTS)))
            br = jnp.pad(p['moe_b_router'][j], (0, ROUTER_LANES - N_EXPERTS),
                         constant_values=NEG_BIG).reshape(1, ROUTER_LANES)
            if j not in expert_w:
                expert_w[j] = (jnp.concatenate([p['moe_w_gate'][j].astype(BF16),
                                                p['moe_w_up'][j].astype(BF16)], axis=-1),
                               p['moe_w_down'][j].astype(BF16))
            xr = _moe(xr, g_ffn, wr, br, *expert_w[j], final_g)
        if prompt:
            new['k'].append(k_win)
            new['v'].append(v_win)
        new['re'].append(s_re)
        new['im'].append(s_im)
    for name in ('re', 'im', 'gv') + (('k', 'v') if prompt else ()):
        new[name] = jnp.stack(new[name]) if new[name] else None
    if not prompt:
        back = lambda c: c.reshape(depth, b, N_KV_HEADS, HEAD_DIM, w).transpose(0, 1, 4, 2, 3)
        new['k'], new['v'] = back(k_out), back(v_out)
    return xr.reshape(b, l, d), new


def kernel(x_prompt, x_sample, cache_k_win, cache_v_win, state_ssm_re, state_ssm_im,
           norm_mix_g, w_in, attn_sinks, ssm_a_re, ssm_a_im, ssm_log_dt, ssm_b_re, ssm_b_im,
           ssm_c_re, ssm_c_im, ssm_d, ssm_w_glu, ssm_b_glu, gmlp_v_norm_g, gmlp_w_s, gmlp_b_s,
           mix_out_norm_g, w_out, norm_ffn_g, ffn_w_gate, ffn_w_up, ffn_w_down,
           moe_w_router, moe_b_router, moe_w_gate, moe_w_up, moe_w_down, final_norm_g):
    params = {
        'norm_mix_g': norm_mix_g, 'w_in': w_in, 'attn_sinks': attn_sinks,
        'ssm_a_re': ssm_a_re, 'ssm_a_im': ssm_a_im, 'ssm_log_dt': ssm_log_dt,
        'ssm_b_re': ssm_b_re, 'ssm_b_im': ssm_b_im, 'ssm_c_re': ssm_c_re, 'ssm_c_im': ssm_c_im,
        'ssm_d': ssm_d, 'ssm_w_glu': ssm_w_glu, 'ssm_b_glu': ssm_b_glu,
        'gmlp_v_norm_g': gmlp_v_norm_g, 'gmlp_w_s': gmlp_w_s, 'gmlp_b_s': gmlp_b_s,
        'mix_out_norm_g': mix_out_norm_g, 'w_out': w_out, 'norm_ffn_g': norm_ffn_g,
        'ffn_w_gate': ffn_w_gate, 'ffn_w_up': ffn_w_up, 'ffn_w_down': ffn_w_down,
        'moe_w_router': moe_w_router, 'moe_b_router': moe_b_router,
        'moe_w_gate': moe_w_gate, 'moe_w_up': moe_w_up, 'moe_w_down': moe_w_down,
        'final_norm_g': final_norm_g,
    }
    for name in ('w_in', 'w_out'):
        params[name + '_bf16'] = params[name].astype(BF16)
    params['ssm'] = _ssm_params(params)
    dense_w, expert_w = {}, {}
    y_p, st_p = _trunk(x_prompt, params, None, dense_w, expert_w)
    y_s, st_s = _trunk(x_sample, params,
                       (cache_k_win, cache_v_win, state_ssm_re, state_ssm_im), dense_w, expert_w)
    return (y_p, y_s,
            st_p['k'], st_p['v'], st_p['re'], st_p['im'],
            st_s['k'], st_s['v'], st_s['re'], st_s['im'], st_s['gv'])
```

```python
import functools
import math

import jax
import jax.numpy as jnp
from jax import lax
from jax.experimental import pallas as pl
from jax.experimental.pallas import tpu as pltpu

F32 = jnp.float32
BF16 = jnp.bfloat16

EPS = 1e-6
HEAD_DIM = 64
N_HEADS = 8
N_KV_HEADS = 2
GQA_GROUP = N_HEADS // N_KV_HEADS
ATTN_W = N_HEADS * HEAD_DIM
KV_W = N_KV_HEADS * HEAD_DIM
WINDOW = 128
ATTN_SCALE = 1.0 / math.sqrt(HEAD_DIM)
LOG2E = math.log2(math.e)
SSM_W = 256
SSM_GROUP_CH = 16
SSM_GROUPS = 16
SSM_STATE = 64
SSM_FLAT = SSM_GROUPS * SSM_STATE
GM_W = 256
GM_HEADS = 4
GM_HEAD_DIM = 64
CHUNK = 128
Q_END = ATTN_W
K_END = Q_END + KV_W
V_END = K_END + KV_W
S_END = V_END + SSM_W
IN_COLS = S_END + 2 * GM_W
N_EXPERTS = 8

VMEM_LIMIT_BYTES = 56 * 1024 * 1024
ROW_TILE = 512
SUB_ROWS = 128
OUTPROJ_PASS_ROWS = 256
ATTN_Q_BLOCKS = 4
MOE_ROW_TILE = 1024
MOE_CHUNK = 128
MOE_SCATTER_ROWS = 256
SCAN_LANES = 512


def _params(n_axes):
    return pltpu.CompilerParams(
        dimension_semantics=("arbitrary",) * n_axes,
        vmem_limit_bytes=VMEM_LIMIT_BYTES)


def _rms(x, g):
    return x * lax.rsqrt(jnp.mean(x * x, axis=-1, keepdims=True) + EPS) * g


def _bdot(a, b):
    return jnp.dot(a.astype(BF16), b.astype(BF16), preferred_element_type=F32)


def _full(shape):
    return pl.BlockSpec(shape, lambda *_: (0,) * len(shape))


def _group_mean(a, n_groups, width):
    lane = lax.broadcasted_iota(jnp.int32, (1, n_groups * width), 1)
    out = jnp.zeros_like(a)
    for h in range(n_groups):
        m = (lane >= h * width) & (lane < (h + 1) * width)
        s = jnp.sum(jnp.where(m, a, 0.0), axis=-1, keepdims=True) * (1.0 / width)
        out = jnp.where(m, s, out)
    return out


def _inproj_body(x_ref, g_ref, w_ref, gv_ref, q_ref, k_ref, v_ref, u_ref, gu_ref, gvn_ref):
    for r0 in range(0, x_ref.shape[0], SUB_ROWS):
        rs = slice(r0, r0 + SUB_ROWS)
        h = _rms(x_ref[rs, :], g_ref[...])
        z = jnp.dot(h.astype(BF16), w_ref[...], preferred_element_type=F32)
        q_ref[rs, :] = (z[:, :Q_END] * (ATTN_SCALE * LOG2E)).astype(BF16)
        k_ref[rs, :] = z[:, Q_END:K_END]
        v_ref[rs, :] = z[:, K_END:V_END]
        u_ref[rs, :] = z[:, V_END:S_END]
        g = jax.nn.gelu(z[:, S_END:])
        gu_ref[rs, :] = g[:, :GM_W]
        vv = g[:, GM_W:]
        mu = _group_mean(vv, GM_HEADS, GM_HEAD_DIM)
        var = _group_mean(jnp.square(vv - mu), GM_HEADS, GM_HEAD_DIM)
        gvn_ref[rs, :] = (vv - mu) * lax.rsqrt(var + EPS) * gv_ref[...]


def _layer(shape, j):
    return pl.BlockSpec((None,) + tuple(shape), lambda *_: (j,) + (0,) * len(shape))


def _inproj(x, g, w_bf16, gv, j):
    rows, d = x.shape
    tm = min(ROW_TILE, rows)
    nt = rows // tm
    row = lambda w: pl.BlockSpec((tm, w), lambda i: (i, 0))
    return pl.pallas_call(
        _inproj_body,
        grid=(nt,),
        in_specs=[row(d), _full((1, d)), _layer((d, IN_COLS), j), _full((1, GM_W))],
        out_specs=[row(ATTN_W), row(KV_W), row(KV_W), row(SSM_W), row(GM_W), row(GM_W)],
        out_shape=[jax.ShapeDtypeStruct((rows, ATTN_W), BF16),
                   jax.ShapeDtypeStruct((rows, KV_W), F32),
                   jax.ShapeDtypeStruct((rows, KV_W), F32),
                   jax.ShapeDtypeStruct((rows, SSM_W), F32),
                   jax.ShapeDtypeStruct((rows, GM_W), F32),
                   jax.ShapeDtypeStruct((rows, GM_W), F32)],
        compiler_params=_params(1),
        name="inproj",
    )(x, g, w_bf16, gv)


def _sink_softmax(s, mask, sink):
    s = jnp.where(mask, s, -jnp.inf)
    m = jnp.maximum(jnp.max(s, axis=-1, keepdims=True), sink)
    e = jnp.exp2(s - m)
    return e / (jnp.sum(e, axis=-1, keepdims=True) + jnp.exp2(sink - m))


def _head_pair_operands(x):
    low = lax.broadcasted_iota(jnp.int32, (1, KV_W), 1) < HEAD_DIM
    swapped = pltpu.roll(x, HEAD_DIM, axis=1)
    keep = lambda a, in_low: jnp.where(low == in_low, a, 0.0).astype(BF16)
    return ((keep(x, True), keep(swapped, False)),
            (keep(swapped, True), keep(x, False)))


def _attn_prompt_body(sink_ref, q_ref, kc_ref, kp_ref, vc_ref, vp_ref, o_ref, *, nq):
    n = pl.program_id(1)
    qi = lax.broadcasted_iota(jnp.int32, (WINDOW, 2 * WINDOW), 0)
    si = lax.broadcasted_iota(jnp.int32, (WINDOW, 2 * WINDOW), 1)
    dist = WINDOW + qi - si
    band = (dist >= 0) & (dist < WINDOW)
    first = band & ((n > 0) | (si >= WINDOW))
    k_ops = _head_pair_operands(jnp.concatenate([kp_ref[...], kc_ref[...]], axis=0))
    v_ops = _head_pair_operands(jnp.concatenate([vp_ref[...], vc_ref[...]], axis=0))
    pair_w = 2 * HEAD_DIM
    for j in range(nq):
        mask = first if j == 0 else band
        q_rows = slice(j * WINDOW, (j + 1) * WINDOW)
        kv_rows = slice(j * WINDOW, (j + 2) * WINDOW)
        for hp in range(N_HEADS // 2):
            kh = (2 * hp) // GQA_GROUP
            qp = q_ref[q_rows, hp * pair_w:(hp + 1) * pair_w]
            acc = None
            for par in range(2):
                sink = sink_ref[2 * hp + par] * LOG2E
                s = lax.dot_general(qp, k_ops[kh][par][kv_rows], (((1,), (1,)), ((), ())),
                                    preferred_element_type=F32)
                s = jnp.where(mask, s, -jnp.inf)
                m = jnp.maximum(jnp.max(s, axis=-1, keepdims=True), sink)
                e = jnp.exp2(s - m)
                den = jnp.sum(e, axis=-1, keepdims=True) + jnp.exp2(sink - m)
                pv = jnp.dot(e.astype(BF16), v_ops[kh][par][kv_rows],
                             preferred_element_type=F32)
                pv = pv * (1.0 / den)
                acc = pv if acc is None else acc + pv
            o_ref[q_rows, hp * pair_w:(hp + 1) * pair_w] = acc


def _attn_prompt(q, k, v, sinks, *, batch, seq):
    nq = ATTN_Q_BLOCKS
    nb = seq // WINDOW
    steps = nb // nq
    cur = lambda w: pl.BlockSpec((nq * WINDOW, w), lambda b, n: (b * steps + n, 0))
    prev = lambda w: pl.BlockSpec(
        (WINDOW, w), lambda b, n: (b * nb + jnp.maximum(n * nq - 1, 0), 0))
    return pl.pallas_call(
        functools.partial(_attn_prompt_body, nq=nq),
        grid=(batch, steps),
        in_specs=[pl.BlockSpec(memory_space=pltpu.SMEM),
                  cur(ATTN_W), cur(KV_W), prev(KV_W), cur(KV_W), prev(KV_W)],
        out_specs=cur(ATTN_W),
        out_shape=jax.ShapeDtypeStruct((batch * seq, ATTN_W), F32),
        compiler_params=_params(2),
        name="attn_prompt",
    )(sinks, q, k, k, v, v)


def _attn_sample_body(sink_ref, q_ref, k_ref, v_ref, kc_ref, vc_ref, kin_ref, vin_ref,
                      o_ref, ko_ref, vo_ref, *, bb, t):
    del kin_ref, vin_ref
    w = kc_ref.shape[-1]
    pair_w = 2 * HEAD_DIM
    lane = lax.broadcasted_iota(jnp.int32, (1, w), 1)
    new_col = lane < t
    k_new_t = k_ref[...].T
    v_new_t = v_ref[...].T
    qf = q_ref[...].astype(F32)
    unit = 2 * t
    upper = lax.broadcasted_iota(jnp.int32, (unit, 1), 0) < t
    zeros_half = jnp.zeros((HEAD_DIM, 2 * w), BF16)
    placed = lambda a, par: jnp.concatenate((a, zeros_half) if par == 0 else (zeros_half, a), axis=0)
    scores, sinks, values = [], [], []
    for b in range(bb):
        shift = (w - b * t) % w
        k_cols = jnp.where(new_col, pltpu.roll(k_new_t, shift, axis=1), 0.0)
        v_cols = jnp.where(new_col, pltpu.roll(v_new_t, shift, axis=1), 0.0)
        k_old = kc_ref[b]
        v_old = vc_ref[b]
        ko_ref[b] = pltpu.roll(jnp.where(new_col, k_cols, k_old), w - t, axis=1)
        vo_ref[b] = pltpu.roll(jnp.where(new_col, v_cols, v_old), w - t, axis=1)
        k_all = jnp.concatenate([k_old, k_cols], axis=1).astype(BF16)
        v_all = jnp.concatenate([v_old, v_cols], axis=1).astype(BF16)
        for kh in range(N_KV_HEADS):
            kv_sl = slice(kh * HEAD_DIM, (kh + 1) * HEAD_DIM)
            q2 = jnp.concatenate(
                [qf[b * t:(b + 1) * t, (2 * kh + c) * pair_w:(2 * kh + c + 1) * pair_w]
                 for c in range(2)], axis=0).astype(BF16)
            for par in range(2):
                h_a = GQA_GROUP * kh + par
                sinks.append(jnp.where(upper, sink_ref[h_a], sink_ref[h_a + 2]) * LOG2E)
                scores.append(jnp.dot(q2, placed(k_all[kv_sl], par),
                                      preferred_element_type=F32))
                values.append(placed(v_all[kv_sl], par))
    s = jnp.concatenate(scores, axis=0)
    ti = lax.broadcasted_iota(jnp.int32, s.shape, 0) % t
    si = lax.broadcasted_iota(jnp.int32, s.shape, 1)
    dist = w + ti - si
    p = _sink_softmax(s, (dist >= 0) & (dist < WINDOW), jnp.concatenate(sinks, axis=0)).astype(BF16)
    for b in range(bb):
        for kh in range(N_KV_HEADS):
            acc = None
            for par in range(2):
                u = (b * N_KV_HEADS + kh) * 2 + par
                pv = lax.dot_general(p[u * unit:(u + 1) * unit], values[u],
                                     (((1,), (1,)), ((), ())), preferred_element_type=F32)
                acc = pv if acc is None else acc + pv
            for c in range(2):
                o_ref[b * t:(b + 1) * t, (2 * kh + c) * pair_w:(2 * kh + c + 1) * pair_w] = (
                    acc[c * t:(c + 1) * t])


def _attn_sample(q, k, v, kc_t, vc_t, k_out, v_out, sinks, layer, *, batch, t):
    w = kc_t.shape[-1]
    bb = w // t
    row = lambda c: pl.BlockSpec((bb * t, c), lambda i: (i, 0))
    buf = pl.BlockSpec((None, bb, KV_W, w), lambda i: (layer, i, 0, 0))
    anywhere = pl.BlockSpec(memory_space=pl.ANY)
    return pl.pallas_call(
        functools.partial(_attn_sample_body, bb=bb, t=t),
        grid=(batch // bb,),
        in_specs=[pl.BlockSpec(memory_space=pltpu.SMEM),
                  row(ATTN_W), row(KV_W), row(KV_W), buf, buf, anywhere, anywhere],
        out_specs=[row(ATTN_W), buf, buf],
        out_shape=[jax.ShapeDtypeStruct((batch * t, ATTN_W), F32),
                   jax.ShapeDtypeStruct(k_out.shape, F32),
                   jax.ShapeDtypeStruct(v_out.shape, F32)],
        input_output_aliases={6: 1, 7: 2},
        compiler_params=_params(1),
        name="attn_sample",
    )(sinks, q, k, v, kc_t, vc_t, k_out, v_out)


def _ssm_body(u_ref, *refs, r, tc, has_h0, n_cast):
    h0_refs, refs = refs[:2 * has_h0], refs[2 * has_h0:]
    (are_ref, aim_ref, ldt_ref, bre_ref, bim_ref, cre_ref, cim_ref, d_ref, wglu_ref,
     bglu_ref), refs = refs[:10], refs[10:]
    cast_in, (o_ref, s_ref), refs = refs[:n_cast], refs[n_cast:n_cast + 2], refs[n_cast + 2:]
    cast_out, (abar_scr, bbar_scr, h_scr, *x_scrs) = refs[:n_cast], refs[n_cast:]
    c = pl.program_id(0)
    for src, dst in zip(cast_in, cast_out):
        dst[...] = src[...].astype(BF16)

    @pl.when(c == 0)
    def _():
        ar = are_ref[...]
        ai = aim_ref[...]
        dt = jnp.exp(ldt_ref[...])
        decay = jnp.exp(dt * ar)
        abr = decay * jnp.cos(dt * ai)
        abi = decay * jnp.sin(dt * ai)
        den = ar * ar + ai * ai
        nr = abr - 1.0
        fr = (nr * ar + abi * ai) / den
        fi = (abi * ar - nr * ai) / den
        abar_scr[0:1, :] = abr
        abar_scr[1:2, :] = abi
        br = bre_ref[...]
        bi = bim_ref[...]
        bbar_scr[:, :SSM_FLAT] = (fr * br - fi * bi).astype(BF16)
        bbar_scr[:, SSM_FLAT:] = (fr * bi + fi * br).astype(BF16)
        if has_h0:
            h_scr[:, :SSM_FLAT] = h0_refs[0][...].T
            h_scr[:, SSM_FLAT:] = h0_refs[1][...].T
        else:
            h_scr[...] = jnp.zeros_like(h_scr)

    u = jnp.swapaxes(u_ref[...], 0, 1).reshape(tc * r, SSM_W)
    ub = u.astype(BF16)

    y = None
    for lc in range(SSM_FLAT // SCAN_LANES):
        lanes = slice(lc * SCAN_LANES, (lc + 1) * SCAN_LANES)
        re_sl = lanes
        im_sl = slice(SSM_FLAT + lc * SCAN_LANES, SSM_FLAT + (lc + 1) * SCAN_LANES)
        xr_scr, xi_scr = x_scrs[2 * lc], x_scrs[2 * lc + 1]
        xr_scr[...] = jnp.dot(ub, bbar_scr[:, re_sl], preferred_element_type=F32)
        xi_scr[...] = jnp.dot(ub, bbar_scr[:, im_sl], preferred_element_type=F32)
        ar = jnp.broadcast_to(abar_scr[0:1, lanes], (8, SCAN_LANES))
        ai = jnp.broadcast_to(abar_scr[1:2, lanes], (8, SCAN_LANES))
        for rg in range(r // 8):
            rg_sl = slice(rg * 8, (rg + 1) * 8)
            hr, hi = h_scr[rg_sl, re_sl], h_scr[rg_sl, im_sl]
            for t in range(tc):
                rows = slice(t * r + rg * 8, t * r + rg * 8 + 8)
                hr, hi = (ar * hr - ai * hi + xr_scr[rows, :],
                          ar * hi + ai * hr + xi_scr[rows, :])
                xr_scr[rows, :] = hr
                xi_scr[rows, :] = hi
            h_scr[rg_sl, re_sl] = hr
            h_scr[rg_sl, im_sl] = hi
        part = (jnp.dot(xr_scr[...].astype(BF16), cre_ref[lanes, :], preferred_element_type=F32)
                - jnp.dot(xi_scr[...].astype(BF16), cim_ref[lanes, :],
                          preferred_element_type=F32))
        y = part if y is None else y + part
    y = jax.nn.gelu(y + d_ref[...] * u)
    gl = jnp.dot(y.astype(BF16), wglu_ref[...], preferred_element_type=F32) + bglu_ref[...]
    o_ref[...] = jnp.swapaxes((y * jax.nn.sigmoid(gl)).reshape(tc, r, SSM_W), 0, 1)

    @pl.when(c == pl.num_programs(0) - 1)
    def _():
        if has_h0:
            s_ref[0] = h_scr[:, :SSM_FLAT].T
            s_ref[1] = h_scr[:, SSM_FLAT:].T
        else:
            s_ref[...] = h_scr[...]


def _cast_slabs(stacks, layer, steps):
    args, in_specs, out_specs, out_shapes = [], [], [], []
    for w in stacks:
        n_l, rows, cols = w.shape
        slab = rows // steps
        args.append(w.reshape(n_l * rows, cols))
        in_specs.append(pl.BlockSpec((slab, cols), lambda i: (layer * steps + i, 0)))
        out_specs.append(pl.BlockSpec((slab, cols), lambda i: (i, 0)))
        out_shapes.append(jax.ShapeDtypeStruct((rows, cols), BF16))
    return args, in_specs, out_specs, out_shapes


def _ssm(u, h0_t, sp, layer, *, tc, cast=None):
    r, steps, _ = u.shape
    blk = tc * r
    has_h0 = h0_t is not None
    seq_blk = pl.BlockSpec((r, tc, SSM_W), lambda c: (0, c, 0))
    state_shape = (2, SSM_FLAT, r) if has_h0 else (r, 2 * SSM_FLAT)
    c_args, c_in, c_out, c_shapes = _cast_slabs(*cast, steps // tc) if cast else ([], [], [], [])
    out = pl.pallas_call(
        functools.partial(_ssm_body, r=r, tc=tc, has_h0=has_h0, n_cast=len(c_args)),
        grid=(steps // tc,),
        in_specs=[seq_blk]
        + [_layer((SSM_FLAT, r), layer)] * (2 * has_h0)
        + [_layer((1, SSM_FLAT), layer)] * 3
        + [_layer((SSM_W, SSM_FLAT), layer)] * 2
        + [_layer((SSM_FLAT, SSM_W), layer)] * 2
        + [_layer((1, SSM_W), layer), _layer((SSM_W, SSM_W), layer), _layer((1, SSM_W), layer)]
        + c_in,
        out_specs=[seq_blk, _full(state_shape)] + c_out,
        out_shape=[jax.ShapeDtypeStruct((r, steps, SSM_W), F32),
                   jax.ShapeDtypeStruct(state_shape, F32)] + c_shapes,
        scratch_shapes=[pltpu.VMEM((2, SSM_FLAT), F32),
                        pltpu.VMEM((SSM_W, 2 * SSM_FLAT), BF16),
                        pltpu.VMEM((r, 2 * SSM_FLAT), F32)]
        + [pltpu.VMEM((blk, SCAN_LANES), F32)] * (2 * SSM_FLAT // SCAN_LANES),
        compiler_params=_params(1),
        name="ssm",
    )(u, *(h0_t if has_h0 else ()), sp['a_re'], sp['a_im'], sp['log_dt'], sp['b_re'], sp['b_im'],
      sp['c_re'], sp['c_im'], sp['d'], sp['w_glu'], sp['b_glu'], *c_args)
    return out[0], out[1], tuple(out[2:])


def _ssm_params(p):
    eye = jnp.eye(SSM_GROUPS, dtype=F32)
    depth = p['ssm_a_re'].shape[0]

    def b_blockdiag(b):
        return jnp.einsum('lgpc,gh->lgchp', b, eye).reshape(depth, SSM_W, SSM_FLAT)

    def c_blockdiag(c):
        return jnp.einsum('lgcp,gh->lgphc', c, eye).reshape(depth, SSM_FLAT, SSM_W)

    return {
        'a_re': p['ssm_a_re'].reshape(depth, 1, SSM_FLAT),
        'a_im': p['ssm_a_im'].reshape(depth, 1, SSM_FLAT),
        'log_dt': jnp.repeat(p['ssm_log_dt'], SSM_STATE, axis=1).reshape(depth, 1, SSM_FLAT),
        'b_re': b_blockdiag(p['ssm_b_re']),
        'b_im': b_blockdiag(p['ssm_b_im']),
        'c_re': c_blockdiag(p['ssm_c_re']).astype(BF16),
        'c_im': c_blockdiag(p['ssm_c_im']).astype(BF16),
        'd': p['ssm_d'].reshape(depth, 1, SSM_W),
        'w_glu': p['ssm_w_glu'].astype(BF16),
        'b_glu': p['ssm_b_glu'].reshape(depth, 1, SSM_W),
    }


def _outproj_body(x_ref, oa_ref, os_ref, gu_ref, gvn_ref, ws_ref, bs_ref, go_ref, w_ref, o_ref,
                  *, chunk):
    tm = x_ref.shape[0]
    ri = lax.broadcasted_iota(jnp.int32, (CHUNK, CHUNK), 0)
    ci = lax.broadcasted_iota(jnp.int32, (CHUNK, CHUNK), 1)
    causal = (ri // chunk == ci // chunk) & (ci <= ri)
    lane = lax.broadcasted_iota(jnp.int32, (1, GM_W), 1)
    w_heads = [jnp.where(causal, ws_ref[h], 0.0).astype(BF16) for h in range(GM_HEADS)]
    go = go_ref[...]
    per_pass = OUTPROJ_PASS_ROWS // CHUNK
    for r0 in range(0, tm, OUTPROJ_PASS_ROWS):
        rs = slice(r0, r0 + OUTPROJ_PASS_ROWS)
        zs = []
        for cblk in range(per_pass):
            vn = gvn_ref[r0 + cblk * CHUNK:r0 + (cblk + 1) * CHUNK, :].astype(BF16)
            z = bs_ref[...]
            for h in range(GM_HEADS):
                z_h = jnp.dot(w_heads[h], vn, preferred_element_type=F32)
                head = (lane >= h * GM_HEAD_DIM) & (lane < (h + 1) * GM_HEAD_DIM)
                z = z + jnp.where(head, z_h, 0.0)
            zs.append(z)
        o = jnp.concatenate([
            _rms(oa_ref[rs, :], go[:, :ATTN_W]),
            _rms(os_ref[rs, :], go[:, ATTN_W:ATTN_W + SSM_W]),
            _rms(gu_ref[rs, :] * jnp.concatenate(zs, axis=0), go[:, ATTN_W + SSM_W:])], axis=-1)
        o_ref[rs, :] = x_ref[rs, :] + jnp.dot(o.astype(BF16), w_ref[...],
                                              preferred_element_type=F32)


def _outproj(x, o_attn, o_ssm, gu, gvn, ws, bs, g_out, w_bf16, j, *, chunk):
    rows, d = x.shape
    tm = min(ROW_TILE, rows)
    row = lambda w: pl.BlockSpec((tm, w), lambda i: (i, 0))
    return pl.pallas_call(
        functools.partial(_outproj_body, chunk=chunk),
        grid=(rows // tm,),
        in_specs=[row(d), row(ATTN_W), row(SSM_W), row(GM_W), row(GM_W),
                  _layer((GM_HEADS, CHUNK, CHUNK), j), _layer((CHUNK, GM_W), j),
                  _full((1, d)), _layer((d, d), j)],
        out_specs=row(d),
        out_shape=jax.ShapeDtypeStruct((rows, d), F32),
        compiler_params=_params(1),
        name="outproj",
    )(x, o_attn, o_ssm, gu, gvn, ws, bs, g_out, w_bf16)


def _gmlp_params(p, chunk):
    ws = p['gmlp_w_s'][:, :, :chunk, :chunk]
    bs = p['gmlp_b_s'][:, :, :chunk]
    if chunk < CHUNK:
        pick = (jnp.arange(CHUNK)[:, None] % chunk == jnp.arange(chunk)[None, :]).astype(F32)
        ws = jnp.einsum('rc,lhcd,sd->lhrs', pick, ws, pick, precision=lax.Precision.HIGHEST)
        bs = jnp.einsum('rc,lhc->lhr', pick, bs, precision=lax.Precision.HIGHEST)
    bs = jnp.repeat(bs.transpose(0, 2, 1), GM_HEAD_DIM, axis=2)
    return ws, bs


def _swiglu_cols(h, wg_ref, wu_ref, wd_ref, n_chunks):
    ff = wg_ref.shape[-1]
    fc = ff // n_chunks
    acc = None
    for j in range(n_chunks):
        sl = slice(j * fc, (j + 1) * fc)
        a = jnp.dot(h, wg_ref[:, sl], preferred_element_type=F32)
        b = jnp.dot(h, wu_ref[:, sl], preferred_element_type=F32)
        m = (jax.nn.silu(a) * b).astype(BF16)
        y = jnp.dot(m, wd_ref[sl, :], preferred_element_type=F32)
        acc = y if acc is None else acc + y
    return acc


def _ffn_body(x_ref, g_ref, wg_ref, wu_ref, wd_ref, *rest, final_norm, n_cast):
    gf_ref = rest[0] if final_norm else None
    rest = rest[int(final_norm):]
    cast_in, o_ref, cast_out = rest[:n_cast], rest[n_cast], rest[n_cast + 1:]
    x = x_ref[...]
    h = _rms(x, g_ref[...]).astype(BF16)
    y = x + _swiglu_cols(h, wg_ref, wu_ref, wd_ref, 2)
    o_ref[...] = _rms(y, gf_ref[...]) if final_norm else y
    if n_cast:
        eg_ref, eu_ref, ed_ref = cast_in
        egu_ref, edo_ref = cast_out
        ffe = eg_ref.shape[-1]
        egu_ref[:, :ffe] = eg_ref[...].astype(BF16)
        egu_ref[:, ffe:] = eu_ref[...].astype(BF16)
        edo_ref[...] = ed_ref[...].astype(BF16)


def _ffn(x, g, wg, wu, wd, final_g, experts=None):
    rows, d = x.shape
    ff = wg.shape[-1]
    tm = min(ROW_TILE, rows)
    steps = rows // tm
    row = pl.BlockSpec((tm, d), lambda i: (i, 0))
    once = lambda shape: pl.BlockSpec(shape, lambda i: (0, 0), pipeline_mode=pl.Buffered(1))
    final_norm = final_g is not None
    cast_args, cast_in, cast_out, cast_shapes = [], [], [], []
    if experts is not None:
        eg, eu, ed, le = experts
        n_l, n_e, _, ffe = eg.shape
        gu_rows, d_rows = n_e * d // steps, n_e * ffe // steps
        slab = lambda r, c: pl.BlockSpec((r, c), lambda i: (le * steps + i, 0))
        cast_args = [eg.reshape(n_l * n_e * d, ffe), eu.reshape(n_l * n_e * d, ffe),
                     ed.reshape(n_l * n_e * ffe, d)]
        cast_in = [slab(gu_rows, ffe), slab(gu_rows, ffe), slab(d_rows, d)]
        cast_out = [pl.BlockSpec((gu_rows, 2 * ffe), lambda i: (i, 0)),
                    pl.BlockSpec((d_rows, d), lambda i: (i, 0))]
        cast_shapes = [jax.ShapeDtypeStruct((n_e * d, 2 * ffe), BF16),
                       jax.ShapeDtypeStruct((n_e * ffe, d), BF16)]
    out = pl.pallas_call(
        functools.partial(_ffn_body, final_norm=final_norm, n_cast=len(cast_in)),
        grid=(steps,),
        in_specs=[row, _full((1, d)), once((d, ff)), once((d, ff)), once((ff, d))]
        + [_full((1, d))] * final_norm + cast_in,
        out_specs=[row] + cast_out,
        out_shape=[jax.ShapeDtypeStruct((rows, d), F32)] + cast_shapes,
        compiler_params=_params(1),
        name="ffn",
    )(x, g, wg, wu, wd, *([final_g] * final_norm), *cast_args)
    if experts is None:
        return out[0], None
    return out[0], (out[1].reshape(n_e, d, 2 * ffe), out[2].reshape(n_e, ffe, d))


def _split_bf16(x):
    hi = x.astype(BF16)
    lo = (x - hi.astype(F32)).astype(BF16)
    return hi, lo


def _router_comb(hf, wrt_ref, brt_ref):
    h_hi, h_lo = _split_bf16(hf)
    w_hi, w_lo = _split_bf16(wrt_ref[...])
    nt = lambda a, b: lax.dot_general(a, b, (((1,), (1,)), ((), ())), preferred_element_type=F32)
    logits = nt(w_hi, h_hi) + (nt(w_lo, h_hi) + nt(w_hi, h_lo)) + brt_ref[...]
    n_e = logits.shape[0]
    row = lax.broadcasted_iota(jnp.int32, logits.shape, 0).astype(F32)
    far = float(n_e)
    m1 = jnp.max(logits, axis=0, keepdims=True)
    i1 = jnp.min(jnp.where(logits == m1, row, far), axis=0, keepdims=True)
    rest = jnp.where(row == i1, -jnp.inf, logits)
    m2 = jnp.max(rest, axis=0, keepdims=True)
    i2 = jnp.min(jnp.where(rest == m2, row, far), axis=0, keepdims=True)
    e2 = jnp.exp(m2 - m1)
    den = 1.0 + e2
    comb = jnp.where(row == i1, 1.0 / den, 0.0) + jnp.where(row == i2, e2 / den, 0.0)
    sel = jnp.where((row == i1) | (row == i2), 1.0, 0.0)
    return comb, sel


def _moe_body(x_ref, g_ref, wr_ref, br_ref, wgu_ref, wd_ref, *rest, final_norm):
    gf_ref = rest[0] if final_norm else None
    o_ref, h_scr, gate_t_scr, rank_t_scr, upper_scr, cnt_smem = rest[int(final_norm):]
    i = pl.program_id(0)
    e = pl.program_id(1)
    tm = x_ref.shape[0]
    ff = wd_ref.shape[0]

    @pl.when((i == 0) & (e == 0))
    def _():
        ri = lax.broadcasted_iota(jnp.int32, (tm, tm), 0)
        ci = lax.broadcasted_iota(jnp.int32, (tm, tm), 1)
        upper_scr[...] = jnp.where(ri < ci, 1.0, 0.0).astype(BF16)

    @pl.when(e == 0)
    def _():
        x = x_ref[...]
        hf = _rms(x, g_ref[...])
        h_scr[...] = hf.astype(BF16)
        comb, sel = _router_comb(hf, wr_ref, br_ref)
        rank = jnp.dot(sel.astype(BF16), upper_scr[...], preferred_element_type=F32)
        rank_t_scr[...] = jnp.where(sel > 0.0, rank, -1.0)
        gate_t_scr[...] = comb
        cnt = jnp.sum(sel, axis=1, keepdims=True)
        for k in range(N_EXPERTS):
            cnt_smem[k] = cnt[k, 0].astype(jnp.int32)
        o_ref[...] = x

    rank_row = rank_t_scr[pl.ds(e, 1), :]
    gate_row = gate_t_scr[pl.ds(e, 1), :]
    n_chunks = (cnt_smem[e] + (MOE_CHUNK - 1)) // MOE_CHUNK
    n_pairs = n_chunks // 2

    def run_chunk(first_slot, size):
        base = first_slot.astype(F32)
        slot_col = lax.broadcasted_iota(jnp.int32, (size, 1), 0).astype(F32) + base
        hit = rank_row == slot_col
        onehot = jnp.where(hit, 1.0, 0.0).astype(BF16)
        xe = jnp.dot(onehot, h_scr[...], preferred_element_type=F32).astype(BF16)
        ab = jnp.dot(xe, wgu_ref[...], preferred_element_type=F32)
        m = (jax.nn.silu(ab[:, :ff]) * ab[:, ff:]).astype(BF16)
        gate = jnp.sum(jnp.where(hit, gate_row, 0.0), axis=-1, keepdims=True)
        ye = (jnp.dot(m, wd_ref[...], preferred_element_type=F32) * gate).astype(BF16)
        sr = min(MOE_SCATTER_ROWS, tm)
        for rb in range(tm // sr):
            sl = slice(rb * sr, (rb + 1) * sr)
            o_ref[sl, :] += lax.dot_general(onehot[:, sl], ye, (((0,), (0,)), ((), ())),
                                            preferred_element_type=F32)

    def pair(c, carry):
        run_chunk(c * (2 * MOE_CHUNK), 2 * MOE_CHUNK)
        return carry

    lax.fori_loop(0, n_pairs, pair, 0)

    @pl.when(n_chunks % 2 == 1)
    def _():
        run_chunk(n_pairs * (2 * MOE_CHUNK), MOE_CHUNK)

    if final_norm:
        @pl.when(e == pl.num_programs(1) - 1)
        def _():
            o_ref[...] = _rms(o_ref[...], gf_ref[...])


def _moe(x, g, wr_pad, br_pad, wgu, wd, final_g):
    rows, d = x.shape
    n_e, ff, _ = wd.shape
    tm = min(MOE_ROW_TILE, rows)
    row = pl.BlockSpec((tm, d), lambda i, e: (i, 0))
    final_norm = final_g is not None
    return pl.pallas_call(
        functools.partial(_moe_body, final_norm=final_norm),
        grid=(rows // tm, n_e),
        in_specs=[row, _full((1, d)), _full((n_e, d)), _full((n_e, 1)),
                  pl.BlockSpec((None, d, 2 * ff), lambda i, e: (e, 0, 0)),
                  pl.BlockSpec((None, ff, d), lambda i, e: (e, 0, 0))]
        + [_full((1, d))] * final_norm,
        out_specs=row,
        out_shape=jax.ShapeDtypeStruct((rows, d), F32),
        scratch_shapes=[pltpu.VMEM((tm, d), BF16),
                        pltpu.VMEM((n_e, tm), F32),
                        pltpu.VMEM((n_e, tm), F32),
                        pltpu.VMEM((tm, tm), BF16),
                        pltpu.SMEM((N_EXPERTS,), jnp.int32)],
        compiler_params=_params(2),
        name="moe",
    )(x, g, wr_pad, br_pad, wgu, wd, *([final_g] * final_norm))


def _trunk(x, p, past, dense_w, expert_w):
    b, l, d = x.shape
    depth = p['w_in'].shape[0]
    rows = b * l
    xr = x.reshape(rows, d)
    prompt = past is None
    new = {'k': [], 'v': [], 're': [], 'im': [], 'gv': []}
    sp = p['ssm']
    ws, bs = _gmlp_params(p, min(l, CHUNK))
    if not prompt:
        w = past[0].shape[2]
        kc_t = past[0].transpose(0, 1, 3, 4, 2).reshape(depth, b, KV_W, w)
        vc_t = past[1].transpose(0, 1, 3, 4, 2).reshape(depth, b, KV_W, w)
        k_out = jnp.zeros_like(kc_t)
        v_out = jnp.zeros_like(vc_t)
        h0_t = tuple(s.transpose(0, 2, 3, 1).reshape(depth, SSM_FLAT, b) for s in past[2:])
    for i in range(depth):
        g_mix = p['norm_mix_g'][i].reshape(1, d)
        gv = p['gmlp_v_norm_g'][i].reshape(1, GM_W)
        q, k, v, u, gu, gvn = _inproj(xr, g_mix, p['w_in_bf16'], gv, i)
        u = u.reshape(b, l, SSM_W)
        sinks = p['attn_sinks'][i]
        if prompt:
            o_attn = _attn_prompt(q, k, v, sinks, batch=b, seq=l)
            nw = min(WINDOW, l)
            k_win = k.reshape(b, l, N_KV_HEADS, HEAD_DIM)[:, l - nw:]
            v_win = v.reshape(b, l, N_KV_HEADS, HEAD_DIM)[:, l - nw:]
            dense_todo = i % 2 == 0 and i // 2 not in dense_w
            o_ssm, s_fin, cast = _ssm(
                u, None, sp, i, tc=CHUNK,
                cast=((p['ffn_w_gate'], p['ffn_w_up'], p['ffn_w_down']), i // 2)
                if dense_todo else None)
            if dense_todo:
                dense_w[i // 2] = cast
            s_re = s_fin[:, :SSM_FLAT].reshape(b, SSM_GROUPS, SSM_STATE)
            s_im = s_fin[:, SSM_FLAT:].reshape(b, SSM_GROUPS, SSM_STATE)
        else:
            o_attn, k_out, v_out = _attn_sample(q, k, v, kc_t, vc_t, k_out, v_out, sinks, i,
                                                batch=b, t=l)
            o_ssm, s_fin, _ = _ssm(u, h0_t, sp, i, tc=l)
            s_re, s_im = (s.reshape(SSM_GROUPS, SSM_STATE, b).transpose(2, 0, 1) for s in s_fin)
            new['gv'].append(gvn.reshape(b, l, GM_HEADS, GM_HEAD_DIM))
        xr = _outproj(xr, o_attn, o_ssm.reshape(rows, SSM_W), gu, gvn, ws, bs,
                      p['mix_out_norm_g'][i].reshape(1, d), p['w_out_bf16'], i,
                      chunk=min(l, CHUNK))
        g_ffn = p['norm_ffn_g'][i].reshape(1, d)
        final_g = p['final_norm_g'].reshape(1, d) if i == depth - 1 else None
        j = i // 2
        if i % 2 == 0:
            todo = prompt and i + 1 < depth and j not in expert_w
            if j not in dense_w:
                dense_w[j] = tuple(p[n][j].astype(BF16)
                                   for n in ('ffn_w_gate', 'ffn_w_up', 'ffn_w_down'))
            xr, cast = _ffn(xr, g_ffn, *dense_w[j], final_g,
                            (p['moe_w_gate'], p['moe_w_up'], p['moe_w_down'], j) if todo else None)
            if todo:
                expert_w[j] = cast
        else:
            wr = p['moe_w_router'][j].T
            br = p['moe_b_router'][j].reshape(N_EXPERTS, 1)
            if j not in expert_w:
                expert_w[j] = (jnp.concatenate([p['moe_w_gate'][j].astype(BF16),
                                                p['moe_w_up'][j].astype(BF16)], axis=-1),
                               p['moe_w_down'][j].astype(BF16))
            xr = _moe(xr, g_ffn, wr, br, *expert_w[j], final_g)
        if prompt:
            new['k'].append(k_win)
            new['v'].append(v_win)
        new['re'].append(s_re)
        new['im'].append(s_im)
    for name in ('re', 'im', 'gv') + (('k', 'v') if prompt else ()):
        new[name] = jnp.stack(new[name]) if new[name] else None
    if not prompt:
        back = lambda c: c.reshape(depth, b, N_KV_HEADS, HEAD_DIM, w).transpose(0, 1, 4, 2, 3)
        new['k'], new['v'] = back(k_out), back(v_out)
    return xr.reshape(b, l, d), new


def kernel(x_prompt, x_sample, cache_k_win, cache_v_win, state_ssm_re, state_ssm_im,
           norm_mix_g, w_in, attn_sinks, ssm_a_re, ssm_a_im, ssm_log_dt, ssm_b_re, ssm_b_im,
           ssm_c_re, ssm_c_im, ssm_d, ssm_w_glu, ssm_b_glu, gmlp_v_norm_g, gmlp_w_s, gmlp_b_s,
           mix_out_norm_g, w_out, norm_ffn_g, ffn_w_gate, ffn_w_up, ffn_w_down,
           moe_w_router, moe_b_router, moe_w_gate, moe_w_up, moe_w_down, final_norm_g):
    params = {
        'norm_mix_g': norm_mix_g, 'w_in': w_in, 'attn_sinks': attn_sinks,
        'ssm_a_re': ssm_a_re, 'ssm_a_im': ssm_a_im, 'ssm_log_dt': ssm_log_dt,
        'ssm_b_re': ssm_b_re, 'ssm_b_im': ssm_b_im, 'ssm_c_re': ssm_c_re, 'ssm_c_im': ssm_c_im,
        'ssm_d': ssm_d, 'ssm_w_glu': ssm_w_glu, 'ssm_b_glu': ssm_b_glu,
        'gmlp_v_norm_g': gmlp_v_norm_g, 'gmlp_w_s': gmlp_w_s, 'gmlp_b_s': gmlp_b_s,
        'mix_out_norm_g': mix_out_norm_g, 'w_out': w_out, 'norm_ffn_g': norm_ffn_g,
        'ffn_w_gate': ffn_w_gate, 'ffn_w_up': ffn_w_up, 'ffn_w_down': ffn_w_down,
        'moe_w_router': moe_w_router, 'moe_b_router': moe_b_router,
        'moe_w_gate': moe_w_gate, 'moe_w_up': moe_w_up, 'moe_w_down': moe_w_down,
        'final_norm_g': final_norm_g,
    }
    for name in ('w_in', 'w_out'):
        params[name + '_bf16'] = params[name].astype(BF16)
    params['ssm'] = _ssm_params(params)
    dense_w, expert_w = {}, {}
    y_p, st_p = _trunk(x_prompt, params, None, dense_w, expert_w)
    y_s, st_s = _trunk(x_sample, params,
                       (cache_k_win, cache_v_win, state_ssm_re, state_ssm_im), dense_w, expert_w)
    return (y_p, y_s,
            st_p['k'], st_p['v'], st_p['re'], st_p['im'],
            st_s['k'], st_s['v'], st_s['re'], st_s['im'], st_s['gv'])
```

```python
import functools
import math

import jax
import jax.numpy as jnp
from jax import lax
from jax.experimental import pallas as pl
from jax.experimental.pallas import tpu as pltpu

F32 = jnp.float32
BF16 = jnp.bfloat16

EPS = 1e-6
HEAD_DIM = 64
N_HEADS = 8
N_KV_HEADS = 2
GQA_GROUP = N_HEADS // N_KV_HEADS
ATTN_W = N_HEADS * HEAD_DIM
KV_W = N_KV_HEADS * HEAD_DIM
WINDOW = 128
ATTN_SCALE = 1.0 / math.sqrt(HEAD_DIM)
LOG2E = math.log2(math.e)
SSM_W = 256
SSM_GROUP_CH = 16
SSM_GROUPS = 16
SSM_STATE = 64
SSM_FLAT = SSM_GROUPS * SSM_STATE
GM_W = 256
GM_HEADS = 4
GM_HEAD_DIM = 64
CHUNK = 128
Q_END = ATTN_W
K_END = Q_END + KV_W
V_END = K_END + KV_W
S_END = V_END + SSM_W
IN_COLS = S_END + 2 * GM_W
N_EXPERTS = 8

VMEM_LIMIT_BYTES = 56 * 1024 * 1024
ROW_TILE = 512
SUB_ROWS = 128
OUTPROJ_PASS_ROWS = 256
ATTN_Q_BLOCKS = 4
MOE_ROW_TILE = 1024
MOE_CHUNK = 128
MOE_SCATTER_ROWS = 256
SCAN_LANES = 512


def _params(n_axes):
    return pltpu.CompilerParams(
        dimension_semantics=("arbitrary",) * n_axes,
        vmem_limit_bytes=VMEM_LIMIT_BYTES)


def _rms(x, g):
    return x * lax.rsqrt(jnp.mean(x * x, axis=-1, keepdims=True) + EPS) * g


def _bdot(a, b):
    return jnp.dot(a.astype(BF16), b.astype(BF16), preferred_element_type=F32)


def _full(shape):
    return pl.BlockSpec(shape, lambda *_: (0,) * len(shape))


def _group_mean(a, n_groups, width):
    lane = lax.broadcasted_iota(jnp.int32, (1, n_groups * width), 1)
    out = jnp.zeros_like(a)
    for h in range(n_groups):
        m = (lane >= h * width) & (lane < (h + 1) * width)
        s = jnp.sum(jnp.where(m, a, 0.0), axis=-1, keepdims=True) * (1.0 / width)
        out = jnp.where(m, s, out)
    return out


def _inproj_body(x_ref, g_ref, w_ref, gv_ref, q_ref, k_ref, v_ref, u_ref, gu_ref, gvn_ref):
    for r0 in range(0, x_ref.shape[0], SUB_ROWS):
        rs = slice(r0, r0 + SUB_ROWS)
        h = _rms(x_ref[rs, :], g_ref[...])
        z = jnp.dot(h.astype(BF16), w_ref[...], preferred_element_type=F32)
        q_ref[rs, :] = (z[:, :Q_END] * (ATTN_SCALE * LOG2E)).astype(BF16)
        k_ref[rs, :] = z[:, Q_END:K_END]
        v_ref[rs, :] = z[:, K_END:V_END]
        u_ref[rs, :] = z[:, V_END:S_END]
        g = jax.nn.gelu(z[:, S_END:])
        gu_ref[rs, :] = g[:, :GM_W].astype(gu_ref.dtype)
        vv = g[:, GM_W:]
        mu = _group_mean(vv, GM_HEADS, GM_HEAD_DIM)
        var = _group_mean(jnp.square(vv - mu), GM_HEADS, GM_HEAD_DIM)
        gvn_ref[rs, :] = ((vv - mu) * lax.rsqrt(var + EPS) * gv_ref[...]).astype(gvn_ref.dtype)


def _layer(shape, j):
    return pl.BlockSpec((None,) + tuple(shape), lambda *_: (j,) + (0,) * len(shape))


def _inproj(x, g, w_bf16, gv, j, act_dtype):
    rows, d = x.shape
    tm = min(ROW_TILE, rows)
    nt = rows // tm
    row = lambda w: pl.BlockSpec((tm, w), lambda i: (i, 0))
    return pl.pallas_call(
        _inproj_body,
        grid=(nt,),
        in_specs=[row(d), _full((1, d)), _layer((d, IN_COLS), j), _full((1, GM_W))],
        out_specs=[row(ATTN_W), row(KV_W), row(KV_W), row(SSM_W), row(GM_W), row(GM_W)],
        out_shape=[jax.ShapeDtypeStruct((rows, ATTN_W), BF16),
                   jax.ShapeDtypeStruct((rows, KV_W), F32),
                   jax.ShapeDtypeStruct((rows, KV_W), F32),
                   jax.ShapeDtypeStruct((rows, SSM_W), F32),
                   jax.ShapeDtypeStruct((rows, GM_W), act_dtype),
                   jax.ShapeDtypeStruct((rows, GM_W), act_dtype)],
        compiler_params=_params(1),
        name="inproj",
    )(x, g, w_bf16, gv)


def _sink_softmax(s, mask, sink):
    s = jnp.where(mask, s, -jnp.inf)
    m = jnp.maximum(jnp.max(s, axis=-1, keepdims=True), sink)
    e = jnp.exp2(s - m)
    return e / (jnp.sum(e, axis=-1, keepdims=True) + jnp.exp2(sink - m))


def _head_pair_operands(x):
    low = lax.broadcasted_iota(jnp.int32, (1, KV_W), 1) < HEAD_DIM
    swapped = pltpu.roll(x, HEAD_DIM, axis=1)
    keep = lambda a, in_low: jnp.where(low == in_low, a, 0.0).astype(BF16)
    return ((keep(x, True), keep(swapped, False)),
            (keep(swapped, True), keep(x, False)))


def _attn_prompt_body(sink_ref, q_ref, kc_ref, kp_ref, vc_ref, vp_ref, o_ref, *, nq):
    n = pl.program_id(1)
    qi = lax.broadcasted_iota(jnp.int32, (WINDOW, 2 * WINDOW), 0)
    si = lax.broadcasted_iota(jnp.int32, (WINDOW, 2 * WINDOW), 1)
    dist = WINDOW + qi - si
    band = (dist >= 0) & (dist < WINDOW)
    first = band & ((n > 0) | (si >= WINDOW))
    k_ops = _head_pair_operands(jnp.concatenate([kp_ref[...], kc_ref[...]], axis=0))
    v_ops = _head_pair_operands(jnp.concatenate([vp_ref[...], vc_ref[...]], axis=0))
    pair_w = 2 * HEAD_DIM
    for j in range(nq):
        mask = first if j == 0 else band
        q_rows = slice(j * WINDOW, (j + 1) * WINDOW)
        kv_rows = slice(j * WINDOW, (j + 2) * WINDOW)
        for hp in range(N_HEADS // 2):
            kh = (2 * hp) // GQA_GROUP
            qp = q_ref[q_rows, hp * pair_w:(hp + 1) * pair_w]
            acc = None
            for par in range(2):
                sink = sink_ref[2 * hp + par] * LOG2E
                s = lax.dot_general(qp, k_ops[kh][par][kv_rows], (((1,), (1,)), ((), ())),
                                    preferred_element_type=F32)
                s = jnp.where(mask, s, -jnp.inf)
                m = jnp.maximum(jnp.max(s, axis=-1, keepdims=True), sink)
                e = jnp.exp2(s - m)
                den = jnp.sum(e, axis=-1, keepdims=True) + jnp.exp2(sink - m)
                pv = jnp.dot(e.astype(BF16), v_ops[kh][par][kv_rows],
                             preferred_element_type=F32)
                pv = pv * (1.0 / den)
                acc = pv if acc is None else acc + pv
            o_ref[q_rows, hp * pair_w:(hp + 1) * pair_w] = acc.astype(o_ref.dtype)


def _attn_prompt(q, k, v, sinks, *, batch, seq):
    nq = ATTN_Q_BLOCKS
    nb = seq // WINDOW
    steps = nb // nq
    cur = lambda w: pl.BlockSpec((nq * WINDOW, w), lambda b, n: (b * steps + n, 0))
    prev = lambda w: pl.BlockSpec(
        (WINDOW, w), lambda b, n: (b * nb + jnp.maximum(n * nq - 1, 0), 0))
    return pl.pallas_call(
        functools.partial(_attn_prompt_body, nq=nq),
        grid=(batch, steps),
        in_specs=[pl.BlockSpec(memory_space=pltpu.SMEM),
                  cur(ATTN_W), cur(KV_W), prev(KV_W), cur(KV_W), prev(KV_W)],
        out_specs=cur(ATTN_W),
        out_shape=jax.ShapeDtypeStruct((batch * seq, ATTN_W), BF16),
        compiler_params=_params(2),
        name="attn_prompt",
    )(sinks, q, k, k, v, v)


def _attn_sample_body(sink_ref, q_ref, k_ref, v_ref, kc_ref, vc_ref, kin_ref, vin_ref,
                      o_ref, ko_ref, vo_ref, *, bb, t):
    del kin_ref, vin_ref
    w = kc_ref.shape[-1]
    pair_w = 2 * HEAD_DIM
    lane = lax.broadcasted_iota(jnp.int32, (1, w), 1)
    new_col = lane < t
    k_new_t = k_ref[...].T
    v_new_t = v_ref[...].T
    qf = q_ref[...].astype(F32)
    unit = 2 * t
    upper = lax.broadcasted_iota(jnp.int32, (unit, 1), 0) < t
    zeros_half = jnp.zeros((HEAD_DIM, 2 * w), BF16)
    placed = lambda a, par: jnp.concatenate((a, zeros_half) if par == 0 else (zeros_half, a), axis=0)
    scores, sinks, values = [], [], []
    for b in range(bb):
        shift = (w - b * t) % w
        k_cols = jnp.where(new_col, pltpu.roll(k_new_t, shift, axis=1), 0.0)
        v_cols = jnp.where(new_col, pltpu.roll(v_new_t, shift, axis=1), 0.0)
        k_old = kc_ref[b]
        v_old = vc_ref[b]
        ko_ref[b] = pltpu.roll(jnp.where(new_col, k_cols, k_old), w - t, axis=1)
        vo_ref[b] = pltpu.roll(jnp.where(new_col, v_cols, v_old), w - t, axis=1)
        k_all = jnp.concatenate([k_old, k_cols], axis=1).astype(BF16)
        v_all = jnp.concatenate([v_old, v_cols], axis=1).astype(BF16)
        for kh in range(N_KV_HEADS):
            kv_sl = slice(kh * HEAD_DIM, (kh + 1) * HEAD_DIM)
            q2 = jnp.concatenate(
                [qf[b * t:(b + 1) * t, (2 * kh + c) * pair_w:(2 * kh + c + 1) * pair_w]
                 for c in range(2)], axis=0).astype(BF16)
            for par in range(2):
                h_a = GQA_GROUP * kh + par
                sinks.append(jnp.where(upper, sink_ref[h_a], sink_ref[h_a + 2]) * LOG2E)
                scores.append(jnp.dot(q2, placed(k_all[kv_sl], par),
                                      preferred_element_type=F32))
                values.append(placed(v_all[kv_sl], par))
    s = jnp.concatenate(scores, axis=0)
    ti = lax.broadcasted_iota(jnp.int32, s.shape, 0) % t
    si = lax.broadcasted_iota(jnp.int32, s.shape, 1)
    dist = w + ti - si
    p = _sink_softmax(s, (dist >= 0) & (dist < WINDOW), jnp.concatenate(sinks, axis=0)).astype(BF16)
    for b in range(bb):
        for kh in range(N_KV_HEADS):
            acc = None
            for par in range(2):
                u = (b * N_KV_HEADS + kh) * 2 + par
                pv = lax.dot_general(p[u * unit:(u + 1) * unit], values[u],
                                     (((1,), (1,)), ((), ())), preferred_element_type=F32)
                acc = pv if acc is None else acc + pv
            for c in range(2):
                o_ref[b * t:(b + 1) * t, (2 * kh + c) * pair_w:(2 * kh + c + 1) * pair_w] = (
                    acc[c * t:(c + 1) * t])


def _attn_sample(q, k, v, kc_t, vc_t, k_out, v_out, sinks, layer, *, batch, t):
    w = kc_t.shape[-1]
    bb = w // t
    row = lambda c: pl.BlockSpec((bb * t, c), lambda i: (i, 0))
    buf = pl.BlockSpec((None, bb, KV_W, w), lambda i: (layer, i, 0, 0))
    anywhere = pl.BlockSpec(memory_space=pl.ANY)
    return pl.pallas_call(
        functools.partial(_attn_sample_body, bb=bb, t=t),
        grid=(batch // bb,),
        in_specs=[pl.BlockSpec(memory_space=pltpu.SMEM),
                  row(ATTN_W), row(KV_W), row(KV_W), buf, buf, anywhere, anywhere],
        out_specs=[row(ATTN_W), buf, buf],
        out_shape=[jax.ShapeDtypeStruct((batch * t, ATTN_W), F32),
                   jax.ShapeDtypeStruct(k_out.shape, F32),
                   jax.ShapeDtypeStruct(v_out.shape, F32)],
        input_output_aliases={6: 1, 7: 2},
        compiler_params=_params(1),
        name="attn_sample",
    )(sinks, q, k, v, kc_t, vc_t, k_out, v_out)


def _ssm_body(u_ref, *refs, r, tc, has_h0, n_cast):
    h0_refs, refs = refs[:2 * has_h0], refs[2 * has_h0:]
    (are_ref, aim_ref, ldt_ref, bre_ref, bim_ref, cre_ref, cim_ref, d_ref, wglu_ref,
     bglu_ref), refs = refs[:10], refs[10:]
    cast_in, (o_ref, s_ref), refs = refs[:n_cast], refs[n_cast:n_cast + 2], refs[n_cast + 2:]
    cast_out, (abar_scr, bbar_scr, h_scr, *x_scrs) = refs[:n_cast], refs[n_cast:]
    c = pl.program_id(0)
    for src, dst in zip(cast_in, cast_out):
        dst[...] = src[...].astype(BF16)

    @pl.when(c == 0)
    def _():
        ar = are_ref[...]
        ai = aim_ref[...]
        dt = jnp.exp(ldt_ref[...])
        decay = jnp.exp(dt * ar)
        abr = decay * jnp.cos(dt * ai)
        abi = decay * jnp.sin(dt * ai)
        den = ar * ar + ai * ai
        nr = abr - 1.0
        fr = (nr * ar + abi * ai) / den
        fi = (abi * ar - nr * ai) / den
        abar_scr[0:1, :] = abr
        abar_scr[1:2, :] = abi
        br = bre_ref[...]
        bi = bim_ref[...]
        bbar_scr[:, :SSM_FLAT] = (fr * br - fi * bi).astype(BF16)
        bbar_scr[:, SSM_FLAT:] = (fr * bi + fi * br).astype(BF16)
        if has_h0:
            h_scr[:, :SSM_FLAT] = h0_refs[0][...].T
            h_scr[:, SSM_FLAT:] = h0_refs[1][...].T
        else:
            h_scr[...] = jnp.zeros_like(h_scr)

    u = jnp.swapaxes(u_ref[...], 0, 1).reshape(tc * r, SSM_W)
    ub = u.astype(BF16)

    y = None
    for lc in range(SSM_FLAT // SCAN_LANES):
        lanes = slice(lc * SCAN_LANES, (lc + 1) * SCAN_LANES)
        re_sl = lanes
        im_sl = slice(SSM_FLAT + lc * SCAN_LANES, SSM_FLAT + (lc + 1) * SCAN_LANES)
        xr_scr, xi_scr = x_scrs[2 * lc], x_scrs[2 * lc + 1]
        xr_scr[...] = jnp.dot(ub, bbar_scr[:, re_sl], preferred_element_type=F32)
        xi_scr[...] = jnp.dot(ub, bbar_scr[:, im_sl], preferred_element_type=F32)
        ar = jnp.broadcast_to(abar_scr[0:1, lanes], (8, SCAN_LANES))
        ai = jnp.broadcast_to(abar_scr[1:2, lanes], (8, SCAN_LANES))
        for rg in range(r // 8):
            rg_sl = slice(rg * 8, (rg + 1) * 8)
            hr, hi = h_scr[rg_sl, re_sl], h_scr[rg_sl, im_sl]
            for t in range(tc):
                rows = slice(t * r + rg * 8, t * r + rg * 8 + 8)
                hr, hi = (ar * hr - ai * hi + xr_scr[rows, :],
                          ar * hi + ai * hr + xi_scr[rows, :])
                xr_scr[rows, :] = hr
                xi_scr[rows, :] = hi
            h_scr[rg_sl, re_sl] = hr
            h_scr[rg_sl, im_sl] = hi
        part = (jnp.dot(xr_scr[...].astype(BF16), cre_ref[lanes, :], preferred_element_type=F32)
                - jnp.dot(xi_scr[...].astype(BF16), cim_ref[lanes, :],
                          preferred_element_type=F32))
        y = part if y is None else y + part
    y = jax.nn.gelu(y + d_ref[...] * u)
    gl = jnp.dot(y.astype(BF16), wglu_ref[...], preferred_element_type=F32) + bglu_ref[...]
    o_ref[...] = jnp.swapaxes((y * jax.nn.sigmoid(gl)).reshape(tc, r, SSM_W), 0, 1).astype(
        o_ref.dtype)

    @pl.when(c == pl.num_programs(0) - 1)
    def _():
        if has_h0:
            s_ref[0] = h_scr[:, :SSM_FLAT].T
            s_ref[1] = h_scr[:, SSM_FLAT:].T
        else:
            s_ref[...] = h_scr[...]


def _cast_slabs(stacks, layer, steps):
    args, in_specs, out_specs, out_shapes = [], [], [], []
    for w in stacks:
        n_l, rows, cols = w.shape
        slab = rows // steps
        args.append(w.reshape(n_l * rows, cols))
        in_specs.append(pl.BlockSpec((slab, cols), lambda i: (layer * steps + i, 0)))
        out_specs.append(pl.BlockSpec((slab, cols), lambda i: (i, 0)))
        out_shapes.append(jax.ShapeDtypeStruct((rows, cols), BF16))
    return args, in_specs, out_specs, out_shapes


def _ssm(u, h0_t, sp, layer, *, tc, out_dtype, cast=None):
    r, steps, _ = u.shape
    blk = tc * r
    has_h0 = h0_t is not None
    seq_blk = pl.BlockSpec((r, tc, SSM_W), lambda c: (0, c, 0))
    state_shape = (2, SSM_FLAT, r) if has_h0 else (r, 2 * SSM_FLAT)
    c_args, c_in, c_out, c_shapes = _cast_slabs(*cast, steps // tc) if cast else ([], [], [], [])
    out = pl.pallas_call(
        functools.partial(_ssm_body, r=r, tc=tc, has_h0=has_h0, n_cast=len(c_args)),
        grid=(steps // tc,),
        in_specs=[seq_blk]
        + [_layer((SSM_FLAT, r), layer)] * (2 * has_h0)
        + [_layer((1, SSM_FLAT), layer)] * 3
        + [_layer((SSM_W, SSM_FLAT), layer)] * 2
        + [_layer((SSM_FLAT, SSM_W), layer)] * 2
        + [_layer((1, SSM_W), layer), _layer((SSM_W, SSM_W), layer), _layer((1, SSM_W), layer)]
        + c_in,
        out_specs=[seq_blk, _full(state_shape)] + c_out,
        out_shape=[jax.ShapeDtypeStruct((r, steps, SSM_W), out_dtype),
                   jax.ShapeDtypeStruct(state_shape, F32)] + c_shapes,
        scratch_shapes=[pltpu.VMEM((2, SSM_FLAT), F32),
                        pltpu.VMEM((SSM_W, 2 * SSM_FLAT), BF16),
                        pltpu.VMEM((r, 2 * SSM_FLAT), F32)]
        + [pltpu.VMEM((blk, SCAN_LANES), F32)] * (2 * SSM_FLAT // SCAN_LANES),
        compiler_params=_params(1),
        name="ssm",
    )(u, *(h0_t if has_h0 else ()), sp['a_re'], sp['a_im'], sp['log_dt'], sp['b_re'], sp['b_im'],
      sp['c_re'], sp['c_im'], sp['d'], sp['w_glu'], sp['b_glu'], *c_args)
    return out[0], out[1], tuple(out[2:])


def _ssm_params(p):
    eye = jnp.eye(SSM_GROUPS, dtype=F32)
    depth = p['ssm_a_re'].shape[0]

    def b_blockdiag(b):
        return jnp.einsum('lgpc,gh->lgchp', b, eye).reshape(depth, SSM_W, SSM_FLAT)

    def c_blockdiag(c):
        return jnp.einsum('lgcp,gh->lgphc', c, eye).reshape(depth, SSM_FLAT, SSM_W)

    return {
        'a_re': p['ssm_a_re'].reshape(depth, 1, SSM_FLAT),
        'a_im': p['ssm_a_im'].reshape(depth, 1, SSM_FLAT),
        'log_dt': jnp.repeat(p['ssm_log_dt'], SSM_STATE, axis=1).reshape(depth, 1, SSM_FLAT),
        'b_re': b_blockdiag(p['ssm_b_re']),
        'b_im': b_blockdiag(p['ssm_b_im']),
        'c_re': c_blockdiag(p['ssm_c_re']).astype(BF16),
        'c_im': c_blockdiag(p['ssm_c_im']).astype(BF16),
        'd': p['ssm_d'].reshape(depth, 1, SSM_W),
        'w_glu': p['ssm_w_glu'].astype(BF16),
        'b_glu': p['ssm_b_glu'].reshape(depth, 1, SSM_W),
    }


def _outproj_body(x_ref, oa_ref, os_ref, gu_ref, gvn_ref, ws_ref, bs_ref, go_ref, w_ref, o_ref,
                  *, chunk):
    tm = x_ref.shape[0]
    ri = lax.broadcasted_iota(jnp.int32, (CHUNK, CHUNK), 0)
    ci = lax.broadcasted_iota(jnp.int32, (CHUNK, CHUNK), 1)
    causal = (ri // chunk == ci // chunk) & (ci <= ri)
    lane = lax.broadcasted_iota(jnp.int32, (1, GM_W), 1)
    w_heads = [jnp.where(causal, ws_ref[h], 0.0).astype(BF16) for h in range(GM_HEADS)]
    go = go_ref[...]
    per_pass = OUTPROJ_PASS_ROWS // CHUNK
    for r0 in range(0, tm, OUTPROJ_PASS_ROWS):
        rs = slice(r0, r0 + OUTPROJ_PASS_ROWS)
        zs = []
        for cblk in range(per_pass):
            vn = gvn_ref[r0 + cblk * CHUNK:r0 + (cblk + 1) * CHUNK, :].astype(BF16)
            z = bs_ref[...]
            for h in range(GM_HEADS):
                z_h = jnp.dot(w_heads[h], vn, preferred_element_type=F32)
                head = (lane >= h * GM_HEAD_DIM) & (lane < (h + 1) * GM_HEAD_DIM)
                z = z + jnp.where(head, z_h, 0.0)
            zs.append(z)
        o = jnp.concatenate([
            _rms(oa_ref[rs, :].astype(F32), go[:, :ATTN_W]),
            _rms(os_ref[rs, :].astype(F32), go[:, ATTN_W:ATTN_W + SSM_W]),
            _rms(gu_ref[rs, :].astype(F32) * jnp.concatenate(zs, axis=0),
                 go[:, ATTN_W + SSM_W:])], axis=-1)
        o_ref[rs, :] = x_ref[rs, :] + jnp.dot(o.astype(BF16), w_ref[...],
                                              preferred_element_type=F32)


def _outproj(x, o_attn, o_ssm, gu, gvn, ws, bs, g_out, w_bf16, j, *, chunk):
    rows, d = x.shape
    tm = min(ROW_TILE, rows)
    row = lambda w: pl.BlockSpec((tm, w), lambda i: (i, 0))
    return pl.pallas_call(
        functools.partial(_outproj_body, chunk=chunk),
        grid=(rows // tm,),
        in_specs=[row(d), row(ATTN_W), row(SSM_W), row(GM_W), row(GM_W),
                  _layer((GM_HEADS, CHUNK, CHUNK), j), _layer((CHUNK, GM_W), j),
                  _full((1, d)), _layer((d, d), j)],
        out_specs=row(d),
        out_shape=jax.ShapeDtypeStruct((rows, d), F32),
        compiler_params=_params(1),
        name="outproj",
    )(x, o_attn, o_ssm, gu, gvn, ws, bs, g_out, w_bf16)


def _gmlp_params(p, chunk):
    ws = p['gmlp_w_s'][:, :, :chunk, :chunk]
    bs = p['gmlp_b_s'][:, :, :chunk]
    if chunk < CHUNK:
        pick = (jnp.arange(CHUNK)[:, None] % chunk == jnp.arange(chunk)[None, :]).astype(F32)
        ws = jnp.einsum('rc,lhcd,sd->lhrs', pick, ws, pick, precision=lax.Precision.HIGHEST)
        bs = jnp.einsum('rc,lhc->lhr', pick, bs, precision=lax.Precision.HIGHEST)
    bs = jnp.repeat(bs.transpose(0, 2, 1), GM_HEAD_DIM, axis=2)
    return ws, bs


def _swiglu_cols(h, wg_ref, wu_ref, wd_ref, n_chunks):
    ff = wg_ref.shape[-1]
    fc = ff // n_chunks
    acc = None
    for j in range(n_chunks):
        sl = slice(j * fc, (j + 1) * fc)
        a = jnp.dot(h, wg_ref[:, sl], preferred_element_type=F32)
        b = jnp.dot(h, wu_ref[:, sl], preferred_element_type=F32)
        m = (jax.nn.silu(a) * b).astype(BF16)
        y = jnp.dot(m, wd_ref[sl, :], preferred_element_type=F32)
        acc = y if acc is None else acc + y
    return acc


def _ffn_body(x_ref, g_ref, wg_ref, wu_ref, wd_ref, *rest, final_norm, n_cast):
    gf_ref = rest[0] if final_norm else None
    rest = rest[int(final_norm):]
    cast_in, o_ref, cast_out = rest[:n_cast], rest[n_cast], rest[n_cast + 1:]
    x = x_ref[...]
    h = _rms(x, g_ref[...]).astype(BF16)
    y = x + _swiglu_cols(h, wg_ref, wu_ref, wd_ref, 2)
    o_ref[...] = _rms(y, gf_ref[...]) if final_norm else y
    if n_cast:
        eg_ref, eu_ref, ed_ref = cast_in
        egu_ref, edo_ref = cast_out
        ffe = eg_ref.shape[-1]
        egu_ref[:, :ffe] = eg_ref[...].astype(BF16)
        egu_ref[:, ffe:] = eu_ref[...].astype(BF16)
        edo_ref[...] = ed_ref[...].astype(BF16)


def _ffn(x, g, wg, wu, wd, final_g, experts=None):
    rows, d = x.shape
    ff = wg.shape[-1]
    tm = min(ROW_TILE, rows)
    steps = rows // tm
    row = pl.BlockSpec((tm, d), lambda i: (i, 0))
    once = lambda shape: pl.BlockSpec(shape, lambda i: (0, 0), pipeline_mode=pl.Buffered(1))
    final_norm = final_g is not None
    cast_args, cast_in, cast_out, cast_shapes = [], [], [], []
    if experts is not None:
        eg, eu, ed, le = experts
        n_l, n_e, _, ffe = eg.shape
        gu_rows, d_rows = n_e * d // steps, n_e * ffe // steps
        slab = lambda r, c: pl.BlockSpec((r, c), lambda i: (le * steps + i, 0))
        cast_args = [eg.reshape(n_l * n_e * d, ffe), eu.reshape(n_l * n_e * d, ffe),
                     ed.reshape(n_l * n_e * ffe, d)]
        cast_in = [slab(gu_rows, ffe), slab(gu_rows, ffe), slab(d_rows, d)]
        cast_out = [pl.BlockSpec((gu_rows, 2 * ffe), lambda i: (i, 0)),
                    pl.BlockSpec((d_rows, d), lambda i: (i, 0))]
        cast_shapes = [jax.ShapeDtypeStruct((n_e * d, 2 * ffe), BF16),
                       jax.ShapeDtypeStruct((n_e * ffe, d), BF16)]
    out = pl.pallas_call(
        functools.partial(_ffn_body, final_norm=final_norm, n_cast=len(cast_in)),
        grid=(steps,),
        in_specs=[row, _full((1, d)), once((d, ff)), once((d, ff)), once((ff, d))]
        + [_full((1, d))] * final_norm + cast_in,
        out_specs=[row] + cast_out,
        out_shape=[jax.ShapeDtypeStruct((rows, d), F32)] + cast_shapes,
        compiler_params=_params(1),
        name="ffn",
    )(x, g, wg, wu, wd, *([final_g] * final_norm), *cast_args)
    if experts is None:
        return out[0], None
    return out[0], (out[1].reshape(n_e, d, 2 * ffe), out[2].reshape(n_e, ffe, d))


def _split_bf16(x):
    hi = x.astype(BF16)
    lo = (x - hi.astype(F32)).astype(BF16)
    return hi, lo


def _router_comb(hf, wrt_ref, brt_ref):
    h_hi, h_lo = _split_bf16(hf)
    w_hi, w_lo = _split_bf16(wrt_ref[...])
    nt = lambda a, b: lax.dot_general(a, b, (((1,), (1,)), ((), ())), preferred_element_type=F32)
    logits = nt(w_hi, h_hi) + (nt(w_lo, h_hi) + nt(w_hi, h_lo)) + brt_ref[...]
    n_e = logits.shape[0]
    row = lax.broadcasted_iota(jnp.int32, logits.shape, 0).astype(F32)
    far = float(n_e)
    m1 = jnp.max(logits, axis=0, keepdims=True)
    i1 = jnp.min(jnp.where(logits == m1, row, far), axis=0, keepdims=True)
    rest = jnp.where(row == i1, -jnp.inf, logits)
    m2 = jnp.max(rest, axis=0, keepdims=True)
    i2 = jnp.min(jnp.where(rest == m2, row, far), axis=0, keepdims=True)
    e2 = jnp.exp(m2 - m1)
    den = 1.0 + e2
    comb = jnp.where(row == i1, 1.0 / den, 0.0) + jnp.where(row == i2, e2 / den, 0.0)
    sel = jnp.where((row == i1) | (row == i2), 1.0, 0.0)
    return comb, sel


def _moe_body(x_ref, g_ref, wr_ref, br_ref, wgu_ref, wd_ref, *rest, final_norm):
    gf_ref = rest[0] if final_norm else None
    o_ref, h_scr, gate_t_scr, rank_t_scr, upper_scr, cnt_smem = rest[int(final_norm):]
    i = pl.program_id(0)
    e = pl.program_id(1)
    tm = x_ref.shape[0]
    ff = wd_ref.shape[0]

    @pl.when((i == 0) & (e == 0))
    def _():
        ri = lax.broadcasted_iota(jnp.int32, (tm, tm), 0)
        ci = lax.broadcasted_iota(jnp.int32, (tm, tm), 1)
        upper_scr[...] = jnp.where(ri < ci, 1.0, 0.0).astype(BF16)

    @pl.when(e == 0)
    def _():
        x = x_ref[...]
        hf = _rms(x, g_ref[...])
        h_scr[...] = hf.astype(BF16)
        comb, sel = _router_comb(hf, wr_ref, br_ref)
        rank = jnp.dot(sel.astype(BF16), upper_scr[...], preferred_element_type=F32)
        rank_t_scr[...] = jnp.where(sel > 0.0, rank, -1.0)
        gate_t_scr[...] = comb
        cnt = jnp.sum(sel, axis=1, keepdims=True)
        for k in range(N_EXPERTS):
            cnt_smem[k] = cnt[k, 0].astype(jnp.int32)
        o_ref[...] = x

    rank_row = rank_t_scr[pl.ds(e, 1), :]
    gate_row = gate_t_scr[pl.ds(e, 1), :]
    n_chunks = (cnt_smem[e] + (MOE_CHUNK - 1)) // MOE_CHUNK
    n_pairs = n_chunks // 2

    def run_chunk(first_slot, size):
        base = first_slot.astype(F32)
        slot_col = lax.broadcasted_iota(jnp.int32, (size, 1), 0).astype(F32) + base
        hit = rank_row == slot_col
        onehot = jnp.where(hit, 1.0, 0.0).astype(BF16)
        xe = jnp.dot(onehot, h_scr[...], preferred_element_type=F32).astype(BF16)
        ab = jnp.dot(xe, wgu_ref[...], preferred_element_type=F32)
        m = (jax.nn.silu(ab[:, :ff]) * ab[:, ff:]).astype(BF16)
        gate = jnp.sum(jnp.where(hit, gate_row, 0.0), axis=-1, keepdims=True)
        ye = (jnp.dot(m, wd_ref[...], preferred_element_type=F32) * gate).astype(BF16)
        sr = min(MOE_SCATTER_ROWS, tm)
        for rb in range(tm // sr):
            sl = slice(rb * sr, (rb + 1) * sr)
            o_ref[sl, :] += lax.dot_general(onehot[:, sl], ye, (((0,), (0,)), ((), ())),
                                            preferred_element_type=F32)

    def pair(c, carry):
        run_chunk(c * (2 * MOE_CHUNK), 2 * MOE_CHUNK)
        return carry

    lax.fori_loop(0, n_pairs, pair, 0)

    @pl.when(n_chunks % 2 == 1)
    def _():
        run_chunk(n_pairs * (2 * MOE_CHUNK), MOE_CHUNK)

    if final_norm:
        @pl.when(e == pl.num_programs(1) - 1)
        def _():
            o_ref[...] = _rms(o_ref[...], gf_ref[...])


def _moe(x, g, wr_pad, br_pad, wgu, wd, final_g):
    rows, d = x.shape
    n_e, ff, _ = wd.shape
    tm = min(MOE_ROW_TILE, rows)
    row = pl.BlockSpec((tm, d), lambda i, e: (i, 0))
    final_norm = final_g is not None
    return pl.pallas_call(
        functools.partial(_moe_body, final_norm=final_norm),
        grid=(rows // tm, n_e),
        in_specs=[row, _full((1, d)), _full((n_e, d)), _full((n_e, 1)),
                  pl.BlockSpec((None, d, 2 * ff), lambda i, e: (e, 0, 0)),
                  pl.BlockSpec((None, ff, d), lambda i, e: (e, 0, 0))]
        + [_full((1, d))] * final_norm,
        out_specs=row,
        out_shape=jax.ShapeDtypeStruct((rows, d), F32),
        scratch_shapes=[pltpu.VMEM((tm, d), BF16),
                        pltpu.VMEM((n_e, tm), F32),
                        pltpu.VMEM((n_e, tm), F32),
                        pltpu.VMEM((tm, tm), BF16),
                        pltpu.SMEM((N_EXPERTS,), jnp.int32)],
        compiler_params=_params(2),
        name="moe",
    )(x, g, wr_pad, br_pad, wgu, wd, *([final_g] * final_norm))


def _trunk(x, p, past, dense_w, expert_w):
    b, l, d = x.shape
    depth = p['w_in'].shape[0]
    rows = b * l
    xr = x.reshape(rows, d)
    prompt = past is None
    act_dtype = BF16 if prompt else F32
    new = {'k': [], 'v': [], 're': [], 'im': [], 'gv': []}
    sp = p['ssm']
    ws, bs = _gmlp_params(p, min(l, CHUNK))
    if not prompt:
        w = past[0].shape[2]
        kc_t = past[0].transpose(0, 1, 3, 4, 2).reshape(depth, b, KV_W, w)
        vc_t = past[1].transpose(0, 1, 3, 4, 2).reshape(depth, b, KV_W, w)
        k_out = jnp.zeros_like(kc_t)
        v_out = jnp.zeros_like(vc_t)
        h0_t = tuple(s.transpose(0, 2, 3, 1).reshape(depth, SSM_FLAT, b) for s in past[2:])
    for i in range(depth):
        g_mix = p['norm_mix_g'][i].reshape(1, d)
        gv = p['gmlp_v_norm_g'][i].reshape(1, GM_W)
        q, k, v, u, gu, gvn = _inproj(xr, g_mix, p['w_in_bf16'], gv, i, act_dtype)
        u = u.reshape(b, l, SSM_W)
        sinks = p['attn_sinks'][i]
        if prompt:
            o_attn = _attn_prompt(q, k, v, sinks, batch=b, seq=l)
            nw = min(WINDOW, l)
            k_win = k.reshape(b, l, N_KV_HEADS, HEAD_DIM)[:, l - nw:]
            v_win = v.reshape(b, l, N_KV_HEADS, HEAD_DIM)[:, l - nw:]
            dense_todo = i % 2 == 0 and i // 2 not in dense_w
            o_ssm, s_fin, cast = _ssm(
                u, None, sp, i, tc=CHUNK, out_dtype=act_dtype,
                cast=((p['ffn_w_gate'], p['ffn_w_up'], p['ffn_w_down']), i // 2)
                if dense_todo else None)
            if dense_todo:
                dense_w[i // 2] = cast
            s_re = s_fin[:, :SSM_FLAT].reshape(b, SSM_GROUPS, SSM_STATE)
            s_im = s_fin[:, SSM_FLAT:].reshape(b, SSM_GROUPS, SSM_STATE)
        else:
            o_attn, k_out, v_out = _attn_sample(q, k, v, kc_t, vc_t, k_out, v_out, sinks, i,
                                                batch=b, t=l)
            o_ssm, s_fin, _ = _ssm(u, h0_t, sp, i, tc=l, out_dtype=act_dtype)
            s_re, s_im = (s.reshape(SSM_GROUPS, SSM_STATE, b).transpose(2, 0, 1) for s in s_fin)
            new['gv'].append(gvn.reshape(b, l, GM_HEADS, GM_HEAD_DIM))
        xr = _outproj(xr, o_attn, o_ssm.reshape(rows, SSM_W), gu, gvn, ws, bs,
                      p['mix_out_norm_g'][i].reshape(1, d), p['w_out_bf16'], i,
                      chunk=min(l, CHUNK))
        g_ffn = p['norm_ffn_g'][i].reshape(1, d)
        final_g = p['final_norm_g'].reshape(1, d) if i == depth - 1 else None
        j = i // 2
        if i % 2 == 0:
            todo = prompt and i + 1 < depth and j not in expert_w
            if j not in dense_w:
                dense_w[j] = tuple(p[n][j].astype(BF16)
                                   for n in ('ffn_w_gate', 'ffn_w_up', 'ffn_w_down'))
            xr, cast = _ffn(xr, g_ffn, *dense_w[j], final_g,
                            (p['moe_w_gate'], p['moe_w_up'], p['moe_w_down'], j) if todo else None)
            if todo:
                expert_w[j] = cast
        else:
            wr = p['moe_w_router'][j].T
            br = p['moe_b_router'][j].reshape(N_EXPERTS, 1)
            if j not in expert_w:
                expert_w[j] = (jnp.concatenate([p['moe_w_gate'][j].astype(BF16),
                                                p['moe_w_up'][j].astype(BF16)], axis=-1),
                               p['moe_w_down'][j].astype(BF16))
            xr = _moe(xr, g_ffn, wr, br, *expert_w[j], final_g)
        if prompt:
            new['k'].append(k_win)
            new['v'].append(v_win)
        new['re'].append(s_re)
        new['im'].append(s_im)
    for name in ('re', 'im', 'gv') + (('k', 'v') if prompt else ()):
        new[name] = jnp.stack(new[name]) if new[name] else None
    if not prompt:
        back = lambda c: c.reshape(depth, b, N_KV_HEADS, HEAD_DIM, w).transpose(0, 1, 4, 2, 3)
        new['k'], new['v'] = back(k_out), back(v_out)
    return xr.reshape(b, l, d), new


def kernel(x_prompt, x_sample, cache_k_win, cache_v_win, state_ssm_re, state_ssm_im,
           norm_mix_g, w_in, attn_sinks, ssm_a_re, ssm_a_im, ssm_log_dt, ssm_b_re, ssm_b_im,
           ssm_c_re, ssm_c_im, ssm_d, ssm_w_glu, ssm_b_glu, gmlp_v_norm_g, gmlp_w_s, gmlp_b_s,
           mix_out_norm_g, w_out, norm_ffn_g, ffn_w_gate, ffn_w_up, ffn_w_down,
           moe_w_router, moe_b_router, moe_w_gate, moe_w_up, moe_w_down, final_norm_g):
    params = {
        'norm_mix_g': norm_mix_g, 'w_in': w_in, 'attn_sinks': attn_sinks,
        'ssm_a_re': ssm_a_re, 'ssm_a_im': ssm_a_im, 'ssm_log_dt': ssm_log_dt,
        'ssm_b_re': ssm_b_re, 'ssm_b_im': ssm_b_im, 'ssm_c_re': ssm_c_re, 'ssm_c_im': ssm_c_im,
        'ssm_d': ssm_d, 'ssm_w_glu': ssm_w_glu, 'ssm_b_glu': ssm_b_glu,
        'gmlp_v_norm_g': gmlp_v_norm_g, 'gmlp_w_s': gmlp_w_s, 'gmlp_b_s': gmlp_b_s,
        'mix_out_norm_g': mix_out_norm_g, 'w_out': w_out, 'norm_ffn_g': norm_ffn_g,
        'ffn_w_gate': ffn_w_gate, 'ffn_w_up': ffn_w_up, 'ffn_w_down': ffn_w_down,
        'moe_w_router': moe_w_router, 'moe_b_router': moe_b_router,
        'moe_w_gate': moe_w_gate, 'moe_w_up': moe_w_up, 'moe_w_down': moe_w_down,
        'final_norm_g': final_norm_g,
    }
    for name in ('w_in', 'w_out'):
        params[name + '_bf16'] = params[name].astype(BF16)
    params['ssm'] = _ssm_params(params)
    dense_w, expert_w = {}, {}
    y_p, st_p = _trunk(x_prompt, params, None, dense_w, expert_w)
    y_s, st_s = _trunk(x_sample, params,
                       (cache_k_win, cache_v_win, state_ssm_re, state_ssm_im), dense_w, expert_w)
    return (y_p, y_s,
            st_p['k'], st_p['v'], st_p['re'], st_p['im'],
            st_s['k'], st_s['v'], st_s['re'], st_s['im'], st_s['gv'])
```

```python
import functools
import math

import jax
import jax.numpy as jnp
from jax import lax
from jax.experimental import pallas as pl
from jax.experimental.pallas import tpu as pltpu

F32 = jnp.float32
BF16 = jnp.bfloat16

EPS = 1e-6
HEAD_DIM = 64
N_HEADS = 8
N_KV_HEADS = 2
GQA_GROUP = N_HEADS // N_KV_HEADS
ATTN_W = N_HEADS * HEAD_DIM
KV_W = N_KV_HEADS * HEAD_DIM
WINDOW = 128
ATTN_SCALE = 1.0 / math.sqrt(HEAD_DIM)
LOG2E = math.log2(math.e)
SSM_W = 256
SSM_GROUP_CH = 16
SSM_GROUPS = 16
SSM_STATE = 64
SSM_FLAT = SSM_GROUPS * SSM_STATE
GM_W = 256
GM_HEADS = 4
GM_HEAD_DIM = 64
CHUNK = 128
Q_END = ATTN_W
K_END = Q_END + KV_W
V_END = K_END + KV_W
S_END = V_END + SSM_W
IN_COLS = S_END + 2 * GM_W
N_EXPERTS = 8

VMEM_LIMIT_BYTES = 56 * 1024 * 1024
ROW_TILE = 512
SUB_ROWS = 128
OUTPROJ_PASS_ROWS = 256
ATTN_Q_BLOCKS = 4
MOE_ROW_TILE = 1024
MOE_CHUNK = 128
MOE_SCATTER_ROWS = 256
MOE_TAIL_WINDOW = 256
SCAN_LANES = 512


def _params(n_axes):
    return pltpu.CompilerParams(
        dimension_semantics=("arbitrary",) * n_axes,
        vmem_limit_bytes=VMEM_LIMIT_BYTES)


def _rms(x, g):
    return x * lax.rsqrt(jnp.mean(x * x, axis=-1, keepdims=True) + EPS) * g


def _bdot(a, b):
    return jnp.dot(a.astype(BF16), b.astype(BF16), preferred_element_type=F32)


def _full(shape):
    return pl.BlockSpec(shape, lambda *_: (0,) * len(shape))


def _group_mean(a, n_groups, width):
    lane = lax.broadcasted_iota(jnp.int32, (1, n_groups * width), 1)
    out = jnp.zeros_like(a)
    for h in range(n_groups):
        m = (lane >= h * width) & (lane < (h + 1) * width)
        s = jnp.sum(jnp.where(m, a, 0.0), axis=-1, keepdims=True) * (1.0 / width)
        out = jnp.where(m, s, out)
    return out


def _inproj_body(x_ref, g_ref, w_ref, gv_ref, q_ref, k_ref, v_ref, u_ref, gu_ref, gvn_ref):
    for r0 in range(0, x_ref.shape[0], SUB_ROWS):
        rs = slice(r0, r0 + SUB_ROWS)
        h = _rms(x_ref[rs, :], g_ref[...])
        z = jnp.dot(h.astype(BF16), w_ref[...], preferred_element_type=F32)
        q_ref[rs, :] = (z[:, :Q_END] * (ATTN_SCALE * LOG2E)).astype(BF16)
        k_ref[rs, :] = z[:, Q_END:K_END]
        v_ref[rs, :] = z[:, K_END:V_END]
        u_ref[rs, :] = z[:, V_END:S_END]
        g = jax.nn.gelu(z[:, S_END:])
        gu_ref[rs, :] = g[:, :GM_W].astype(gu_ref.dtype)
        vv = g[:, GM_W:]
        mu = _group_mean(vv, GM_HEADS, GM_HEAD_DIM)
        var = _group_mean(jnp.square(vv - mu), GM_HEADS, GM_HEAD_DIM)
        gvn_ref[rs, :] = ((vv - mu) * lax.rsqrt(var + EPS) * gv_ref[...]).astype(gvn_ref.dtype)


def _layer(shape, j):
    return pl.BlockSpec((None,) + tuple(shape), lambda *_: (j,) + (0,) * len(shape))


def _inproj(x, g, w_bf16, gv, j, act_dtype):
    rows, d = x.shape
    tm = min(ROW_TILE, rows)
    nt = rows // tm
    row = lambda w: pl.BlockSpec((tm, w), lambda i: (i, 0))
    return pl.pallas_call(
        _inproj_body,
        grid=(nt,),
        in_specs=[row(d), _full((1, d)), _layer((d, IN_COLS), j), _full((1, GM_W))],
        out_specs=[row(ATTN_W), row(KV_W), row(KV_W), row(SSM_W), row(GM_W), row(GM_W)],
        out_shape=[jax.ShapeDtypeStruct((rows, ATTN_W), BF16),
                   jax.ShapeDtypeStruct((rows, KV_W), F32),
                   jax.ShapeDtypeStruct((rows, KV_W), F32),
                   jax.ShapeDtypeStruct((rows, SSM_W), F32),
                   jax.ShapeDtypeStruct((rows, GM_W), act_dtype),
                   jax.ShapeDtypeStruct((rows, GM_W), act_dtype)],
        compiler_params=_params(1),
        name="inproj",
    )(x, g, w_bf16, gv)


def _sink_softmax(s, mask, sink):
    s = jnp.where(mask, s, -jnp.inf)
    m = jnp.maximum(jnp.max(s, axis=-1, keepdims=True), sink)
    e = jnp.exp2(s - m)
    return e / (jnp.sum(e, axis=-1, keepdims=True) + jnp.exp2(sink - m))


def _head_pair_operands(x):
    low = lax.broadcasted_iota(jnp.int32, (1, KV_W), 1) < HEAD_DIM
    swapped = pltpu.roll(x, HEAD_DIM, axis=1)
    keep = lambda a, in_low: jnp.where(low == in_low, a, 0.0).astype(BF16)
    return ((keep(x, True), keep(swapped, False)),
            (keep(swapped, True), keep(x, False)))


def _attn_prompt_body(sink_ref, q_ref, kc_ref, kp_ref, vc_ref, vp_ref, o_ref, *, nq):
    n = pl.program_id(1)
    qi = lax.broadcasted_iota(jnp.int32, (WINDOW, 2 * WINDOW), 0)
    si = lax.broadcasted_iota(jnp.int32, (WINDOW, 2 * WINDOW), 1)
    dist = WINDOW + qi - si
    band = (dist >= 0) & (dist < WINDOW)
    first = band & ((n > 0) | (si >= WINDOW))
    k_ops = _head_pair_operands(jnp.concatenate([kp_ref[...], kc_ref[...]], axis=0))
    v_ops = _head_pair_operands(jnp.concatenate([vp_ref[...], vc_ref[...]], axis=0))
    pair_w = 2 * HEAD_DIM
    for j in range(nq):
        mask = first if j == 0 else band
        q_rows = slice(j * WINDOW, (j + 1) * WINDOW)
        kv_rows = slice(j * WINDOW, (j + 2) * WINDOW)
        for hp in range(N_HEADS // 2):
            kh = (2 * hp) // GQA_GROUP
            qp = q_ref[q_rows, hp * pair_w:(hp + 1) * pair_w]
            acc = None
            for par in range(2):
                sink = sink_ref[2 * hp + par] * LOG2E
                s = lax.dot_general(qp, k_ops[kh][par][kv_rows], (((1,), (1,)), ((), ())),
                                    preferred_element_type=F32)
                s = jnp.where(mask, s, -jnp.inf)
                m = jnp.maximum(jnp.max(s, axis=-1, keepdims=True), sink)
                e = jnp.exp2(s - m)
                den = jnp.sum(e, axis=-1, keepdims=True) + jnp.exp2(sink - m)
                pv = jnp.dot(e.astype(BF16), v_ops[kh][par][kv_rows],
                             preferred_element_type=F32)
                pv = pv * (1.0 / den)
                acc = pv if acc is None else acc + pv
            o_ref[q_rows, hp * pair_w:(hp + 1) * pair_w] = acc.astype(o_ref.dtype)


def _attn_prompt(q, k, v, sinks, *, batch, seq):
    nq = ATTN_Q_BLOCKS
    nb = seq // WINDOW
    steps = nb // nq
    cur = lambda w: pl.BlockSpec((nq * WINDOW, w), lambda b, n: (b * steps + n, 0))
    prev = lambda w: pl.BlockSpec(
        (WINDOW, w), lambda b, n: (b * nb + jnp.maximum(n * nq - 1, 0), 0))
    return pl.pallas_call(
        functools.partial(_attn_prompt_body, nq=nq),
        grid=(batch, steps),
        in_specs=[pl.BlockSpec(memory_space=pltpu.SMEM),
                  cur(ATTN_W), cur(KV_W), prev(KV_W), cur(KV_W), prev(KV_W)],
        out_specs=cur(ATTN_W),
        out_shape=jax.ShapeDtypeStruct((batch * seq, ATTN_W), BF16),
        compiler_params=_params(2),
        name="attn_prompt",
    )(sinks, q, k, k, v, v)


def _attn_sample_body(sink_ref, q_ref, k_ref, v_ref, kc_ref, vc_ref, kin_ref, vin_ref,
                      o_ref, ko_ref, vo_ref, *, bb, t):
    del kin_ref, vin_ref
    w = kc_ref.shape[-1]
    pair_w = 2 * HEAD_DIM
    lane = lax.broadcasted_iota(jnp.int32, (1, w), 1)
    new_col = lane < t
    k_new_t = k_ref[...].T
    v_new_t = v_ref[...].T
    qf = q_ref[...].astype(F32)
    unit = 2 * t
    upper = lax.broadcasted_iota(jnp.int32, (unit, 1), 0) < t
    zeros_half = jnp.zeros((HEAD_DIM, 2 * w), BF16)
    placed = lambda a, par: jnp.concatenate((a, zeros_half) if par == 0 else (zeros_half, a), axis=0)
    scores, sinks, values = [], [], []
    for b in range(bb):
        shift = (w - b * t) % w
        k_cols = jnp.where(new_col, pltpu.roll(k_new_t, shift, axis=1), 0.0)
        v_cols = jnp.where(new_col, pltpu.roll(v_new_t, shift, axis=1), 0.0)
        k_old = kc_ref[b]
        v_old = vc_ref[b]
        ko_ref[b] = pltpu.roll(jnp.where(new_col, k_cols, k_old), w - t, axis=1)
        vo_ref[b] = pltpu.roll(jnp.where(new_col, v_cols, v_old), w - t, axis=1)
        k_all = jnp.concatenate([k_old, k_cols], axis=1).astype(BF16)
        v_all = jnp.concatenate([v_old, v_cols], axis=1).astype(BF16)
        for kh in range(N_KV_HEADS):
            kv_sl = slice(kh * HEAD_DIM, (kh + 1) * HEAD_DIM)
            q2 = jnp.concatenate(
                [qf[b * t:(b + 1) * t, (2 * kh + c) * pair_w:(2 * kh + c + 1) * pair_w]
                 for c in range(2)], axis=0).astype(BF16)
            for par in range(2):
                h_a = GQA_GROUP * kh + par
                sinks.append(jnp.where(upper, sink_ref[h_a], sink_ref[h_a + 2]) * LOG2E)
                scores.append(jnp.dot(q2, placed(k_all[kv_sl], par),
                                      preferred_element_type=F32))
                values.append(placed(v_all[kv_sl], par))
    s = jnp.concatenate(scores, axis=0)
    ti = lax.broadcasted_iota(jnp.int32, s.shape, 0) % t
    si = lax.broadcasted_iota(jnp.int32, s.shape, 1)
    dist = w + ti - si
    p = _sink_softmax(s, (dist >= 0) & (dist < WINDOW), jnp.concatenate(sinks, axis=0)).astype(BF16)
    for b in range(bb):
        for kh in range(N_KV_HEADS):
            acc = None
            for par in range(2):
                u = (b * N_KV_HEADS + kh) * 2 + par
                pv = lax.dot_general(p[u * unit:(u + 1) * unit], values[u],
                                     (((1,), (1,)), ((), ())), preferred_element_type=F32)
                acc = pv if acc is None else acc + pv
            for c in range(2):
                o_ref[b * t:(b + 1) * t, (2 * kh + c) * pair_w:(2 * kh + c + 1) * pair_w] = (
                    acc[c * t:(c + 1) * t])


def _attn_sample(q, k, v, kc_t, vc_t, k_out, v_out, sinks, layer, *, batch, t):
    w = kc_t.shape[-1]
    bb = w // t
    row = lambda c: pl.BlockSpec((bb * t, c), lambda i: (i, 0))
    buf = pl.BlockSpec((None, bb, KV_W, w), lambda i: (layer, i, 0, 0))
    anywhere = pl.BlockSpec(memory_space=pl.ANY)
    return pl.pallas_call(
        functools.partial(_attn_sample_body, bb=bb, t=t),
        grid=(batch // bb,),
        in_specs=[pl.BlockSpec(memory_space=pltpu.SMEM),
                  row(ATTN_W), row(KV_W), row(KV_W), buf, buf, anywhere, anywhere],
        out_specs=[row(ATTN_W), buf, buf],
        out_shape=[jax.ShapeDtypeStruct((batch * t, ATTN_W), F32),
                   jax.ShapeDtypeStruct(k_out.shape, F32),
                   jax.ShapeDtypeStruct(v_out.shape, F32)],
        input_output_aliases={6: 1, 7: 2},
        compiler_params=_params(1),
        name="attn_sample",
    )(sinks, q, k, v, kc_t, vc_t, k_out, v_out)


def _ssm_body(u_ref, *refs, r, tc, has_h0, n_cast):
    h0_refs, refs = refs[:2 * has_h0], refs[2 * has_h0:]
    (are_ref, aim_ref, ldt_ref, bre_ref, bim_ref, cre_ref, cim_ref, d_ref, wglu_ref,
     bglu_ref), refs = refs[:10], refs[10:]
    cast_in, (o_ref, s_ref), refs = refs[:n_cast], refs[n_cast:n_cast + 2], refs[n_cast + 2:]
    cast_out, (abar_scr, bbar_scr, h_scr, *x_scrs) = refs[:n_cast], refs[n_cast:]
    c = pl.program_id(0)
    for src, dst in zip(cast_in, cast_out):
        dst[...] = src[...].astype(BF16)

    @pl.when(c == 0)
    def _():
        ar = are_ref[...]
        ai = aim_ref[...]
        dt = jnp.exp(ldt_ref[...])
        decay = jnp.exp(dt * ar)
        abr = decay * jnp.cos(dt * ai)
        abi = decay * jnp.sin(dt * ai)
        den = ar * ar + ai * ai
        nr = abr - 1.0
        fr = (nr * ar + abi * ai) / den
        fi = (abi * ar - nr * ai) / den
        abar_scr[0:1, :] = abr
        abar_scr[1:2, :] = abi
        br = bre_ref[...]
        bi = bim_ref[...]
        bbar_scr[:, :SSM_FLAT] = (fr * br - fi * bi).astype(BF16)
        bbar_scr[:, SSM_FLAT:] = (fr * bi + fi * br).astype(BF16)
        if has_h0:
            h_scr[:, :SSM_FLAT] = h0_refs[0][...].T
            h_scr[:, SSM_FLAT:] = h0_refs[1][...].T
        else:
            h_scr[...] = jnp.zeros_like(h_scr)

    u = jnp.swapaxes(u_ref[...], 0, 1).reshape(tc * r, SSM_W)
    ub = u.astype(BF16)

    y = None
    for lc in range(SSM_FLAT // SCAN_LANES):
        lanes = slice(lc * SCAN_LANES, (lc + 1) * SCAN_LANES)
        re_sl = lanes
        im_sl = slice(SSM_FLAT + lc * SCAN_LANES, SSM_FLAT + (lc + 1) * SCAN_LANES)
        xr_scr, xi_scr = x_scrs[2 * lc], x_scrs[2 * lc + 1]
        xr_scr[...] = jnp.dot(ub, bbar_scr[:, re_sl], preferred_element_type=F32)
        xi_scr[...] = jnp.dot(ub, bbar_scr[:, im_sl], preferred_element_type=F32)
        ar = jnp.broadcast_to(abar_scr[0:1, lanes], (8, SCAN_LANES))
        ai = jnp.broadcast_to(abar_scr[1:2, lanes], (8, SCAN_LANES))
        for rg in range(r // 8):
            rg_sl = slice(rg * 8, (rg + 1) * 8)
            hr, hi = h_scr[rg_sl, re_sl], h_scr[rg_sl, im_sl]
            for t in range(tc):
                rows = slice(t * r + rg * 8, t * r + rg * 8 + 8)
                hr, hi = (ar * hr - ai * hi + xr_scr[rows, :],
                          ar * hi + ai * hr + xi_scr[rows, :])
                xr_scr[rows, :] = hr
                xi_scr[rows, :] = hi
            h_scr[rg_sl, re_sl] = hr
            h_scr[rg_sl, im_sl] = hi
        part = (jnp.dot(xr_scr[...].astype(BF16), cre_ref[lanes, :], preferred_element_type=F32)
                - jnp.dot(xi_scr[...].astype(BF16), cim_ref[lanes, :],
                          preferred_element_type=F32))
        y = part if y is None else y + part
    y = jax.nn.gelu(y + d_ref[...] * u)
    gl = jnp.dot(y.astype(BF16), wglu_ref[...], preferred_element_type=F32) + bglu_ref[...]
    o_ref[...] = jnp.swapaxes((y * jax.nn.sigmoid(gl)).reshape(tc, r, SSM_W), 0, 1).astype(
        o_ref.dtype)

    @pl.when(c == pl.num_programs(0) - 1)
    def _():
        if has_h0:
            s_ref[0] = h_scr[:, :SSM_FLAT].T
            s_ref[1] = h_scr[:, SSM_FLAT:].T
        else:
            s_ref[...] = h_scr[...]


def _cast_slabs(stacks, layer, steps):
    args, in_specs, out_specs, out_shapes = [], [], [], []
    for w in stacks:
        n_l, rows, cols = w.shape
        slab = rows // steps
        args.append(w.reshape(n_l * rows, cols))
        in_specs.append(pl.BlockSpec((slab, cols), lambda i: (layer * steps + i, 0)))
        out_specs.append(pl.BlockSpec((slab, cols), lambda i: (i, 0)))
        out_shapes.append(jax.ShapeDtypeStruct((rows, cols), BF16))
    return args, in_specs, out_specs, out_shapes


def _ssm(u, h0_t, sp, layer, *, tc, out_dtype, cast=None):
    r, steps, _ = u.shape
    blk = tc * r
    has_h0 = h0_t is not None
    seq_blk = pl.BlockSpec((r, tc, SSM_W), lambda c: (0, c, 0))
    state_shape = (2, SSM_FLAT, r) if has_h0 else (r, 2 * SSM_FLAT)
    c_args, c_in, c_out, c_shapes = _cast_slabs(*cast, steps // tc) if cast else ([], [], [], [])
    out = pl.pallas_call(
        functools.partial(_ssm_body, r=r, tc=tc, has_h0=has_h0, n_cast=len(c_args)),
        grid=(steps // tc,),
        in_specs=[seq_blk]
        + [_layer((SSM_FLAT, r), layer)] * (2 * has_h0)
        + [_layer((1, SSM_FLAT), layer)] * 3
        + [_layer((SSM_W, SSM_FLAT), layer)] * 2
        + [_layer((SSM_FLAT, SSM_W), layer)] * 2
        + [_layer((1, SSM_W), layer), _layer((SSM_W, SSM_W), layer), _layer((1, SSM_W), layer)]
        + c_in,
        out_specs=[seq_blk, _full(state_shape)] + c_out,
        out_shape=[jax.ShapeDtypeStruct((r, steps, SSM_W), out_dtype),
                   jax.ShapeDtypeStruct(state_shape, F32)] + c_shapes,
        scratch_shapes=[pltpu.VMEM((2, SSM_FLAT), F32),
                        pltpu.VMEM((SSM_W, 2 * SSM_FLAT), BF16),
                        pltpu.VMEM((r, 2 * SSM_FLAT), F32)]
        + [pltpu.VMEM((blk, SCAN_LANES), F32)] * (2 * SSM_FLAT // SCAN_LANES),
        compiler_params=_params(1),
        name="ssm",
    )(u, *(h0_t if has_h0 else ()), sp['a_re'], sp['a_im'], sp['log_dt'], sp['b_re'], sp['b_im'],
      sp['c_re'], sp['c_im'], sp['d'], sp['w_glu'], sp['b_glu'], *c_args)
    return out[0], out[1], tuple(out[2:])


def _ssm_params(p):
    eye = jnp.eye(SSM_GROUPS, dtype=F32)
    depth = p['ssm_a_re'].shape[0]

    def b_blockdiag(b):
        return jnp.einsum('lgpc,gh->lgchp', b, eye).reshape(depth, SSM_W, SSM_FLAT)

    def c_blockdiag(c):
        return jnp.einsum('lgcp,gh->lgphc', c, eye).reshape(depth, SSM_FLAT, SSM_W)

    return {
        'a_re': p['ssm_a_re'].reshape(depth, 1, SSM_FLAT),
        'a_im': p['ssm_a_im'].reshape(depth, 1, SSM_FLAT),
        'log_dt': jnp.repeat(p['ssm_log_dt'], SSM_STATE, axis=1).reshape(depth, 1, SSM_FLAT),
        'b_re': b_blockdiag(p['ssm_b_re']),
        'b_im': b_blockdiag(p['ssm_b_im']),
        'c_re': c_blockdiag(p['ssm_c_re']).astype(BF16),
        'c_im': c_blockdiag(p['ssm_c_im']).astype(BF16),
        'd': p['ssm_d'].reshape(depth, 1, SSM_W),
        'w_glu': p['ssm_w_glu'].astype(BF16),
        'b_glu': p['ssm_b_glu'].reshape(depth, 1, SSM_W),
    }


def _outproj_body(x_ref, oa_ref, os_ref, gu_ref, gvn_ref, ws_ref, bs_ref, go_ref, w_ref, o_ref,
                  *, chunk):
    tm = x_ref.shape[0]
    ri = lax.broadcasted_iota(jnp.int32, (CHUNK, CHUNK), 0)
    ci = lax.broadcasted_iota(jnp.int32, (CHUNK, CHUNK), 1)
    causal = (ri // chunk == ci // chunk) & (ci <= ri)
    lane = lax.broadcasted_iota(jnp.int32, (1, GM_W), 1)
    w_heads = [jnp.where(causal, ws_ref[h], 0.0).astype(BF16) for h in range(GM_HEADS)]
    go = go_ref[...]
    per_pass = OUTPROJ_PASS_ROWS // CHUNK
    for r0 in range(0, tm, OUTPROJ_PASS_ROWS):
        rs = slice(r0, r0 + OUTPROJ_PASS_ROWS)
        zs = []
        for cblk in range(per_pass):
            vn = gvn_ref[r0 + cblk * CHUNK:r0 + (cblk + 1) * CHUNK, :].astype(BF16)
            z = bs_ref[...]
            for h in range(GM_HEADS):
                z_h = jnp.dot(w_heads[h], vn, preferred_element_type=F32)
                head = (lane >= h * GM_HEAD_DIM) & (lane < (h + 1) * GM_HEAD_DIM)
                z = z + jnp.where(head, z_h, 0.0)
            zs.append(z)
        o = jnp.concatenate([
            _rms(oa_ref[rs, :].astype(F32), go[:, :ATTN_W]),
            _rms(os_ref[rs, :].astype(F32), go[:, ATTN_W:ATTN_W + SSM_W]),
            _rms(gu_ref[rs, :].astype(F32) * jnp.concatenate(zs, axis=0),
                 go[:, ATTN_W + SSM_W:])], axis=-1)
        o_ref[rs, :] = x_ref[rs, :] + jnp.dot(o.astype(BF16), w_ref[...],
                                              preferred_element_type=F32)


def _outproj(x, o_attn, o_ssm, gu, gvn, ws, bs, g_out, w_bf16, j, *, chunk):
    rows, d = x.shape
    tm = min(ROW_TILE, rows)
    row = lambda w: pl.BlockSpec((tm, w), lambda i: (i, 0))
    return pl.pallas_call(
        functools.partial(_outproj_body, chunk=chunk),
        grid=(rows // tm,),
        in_specs=[row(d), row(ATTN_W), row(SSM_W), row(GM_W), row(GM_W),
                  _layer((GM_HEADS, CHUNK, CHUNK), j), _layer((CHUNK, GM_W), j),
                  _full((1, d)), _layer((d, d), j)],
        out_specs=row(d),
        out_shape=jax.ShapeDtypeStruct((rows, d), F32),
        compiler_params=_params(1),
        name="outproj",
    )(x, o_attn, o_ssm, gu, gvn, ws, bs, g_out, w_bf16)


def _gmlp_params(p, chunk):
    ws = p['gmlp_w_s'][:, :, :chunk, :chunk]
    bs = p['gmlp_b_s'][:, :, :chunk]
    if chunk < CHUNK:
        pick = (jnp.arange(CHUNK)[:, None] % chunk == jnp.arange(chunk)[None, :]).astype(F32)
        ws = jnp.einsum('rc,lhcd,sd->lhrs', pick, ws, pick, precision=lax.Precision.HIGHEST)
        bs = jnp.einsum('rc,lhc->lhr', pick, bs, precision=lax.Precision.HIGHEST)
    bs = jnp.repeat(bs.transpose(0, 2, 1), GM_HEAD_DIM, axis=2)
    return ws, bs


def _swiglu_cols(h, wg_ref, wu_ref, wd_ref, n_chunks):
    ff = wg_ref.shape[-1]
    fc = ff // n_chunks
    acc = None
    for j in range(n_chunks):
        sl = slice(j * fc, (j + 1) * fc)
        a = jnp.dot(h, wg_ref[:, sl], preferred_element_type=F32)
        b = jnp.dot(h, wu_ref[:, sl], preferred_element_type=F32)
        m = (jax.nn.silu(a) * b).astype(BF16)
        y = jnp.dot(m, wd_ref[sl, :], preferred_element_type=F32)
        acc = y if acc is None else acc + y
    return acc


def _ffn_body(x_ref, g_ref, wg_ref, wu_ref, wd_ref, *rest, final_norm, n_cast):
    gf_ref = rest[0] if final_norm else None
    rest = rest[int(final_norm):]
    cast_in, o_ref, cast_out = rest[:n_cast], rest[n_cast], rest[n_cast + 1:]
    x = x_ref[...]
    h = _rms(x, g_ref[...]).astype(BF16)
    y = x + _swiglu_cols(h, wg_ref, wu_ref, wd_ref, 2)
    o_ref[...] = _rms(y, gf_ref[...]) if final_norm else y
    if n_cast:
        eg_ref, eu_ref, ed_ref = cast_in
        egu_ref, edo_ref = cast_out
        ffe = eg_ref.shape[-1]
        egu_ref[:, :ffe] = eg_ref[...].astype(BF16)
        egu_ref[:, ffe:] = eu_ref[...].astype(BF16)
        edo_ref[...] = ed_ref[...].astype(BF16)


def _ffn(x, g, wg, wu, wd, final_g, experts=None):
    rows, d = x.shape
    ff = wg.shape[-1]
    tm = min(ROW_TILE, rows)
    steps = rows // tm
    row = pl.BlockSpec((tm, d), lambda i: (i, 0))
    once = lambda shape: pl.BlockSpec(shape, lambda i: (0, 0), pipeline_mode=pl.Buffered(1))
    final_norm = final_g is not None
    cast_args, cast_in, cast_out, cast_shapes = [], [], [], []
    if experts is not None:
        eg, eu, ed, le = experts
        n_l, n_e, _, ffe = eg.shape
        gu_rows, d_rows = n_e * d // steps, n_e * ffe // steps
        slab = lambda r, c: pl.BlockSpec((r, c), lambda i: (le * steps + i, 0))
        cast_args = [eg.reshape(n_l * n_e * d, ffe), eu.reshape(n_l * n_e * d, ffe),
                     ed.reshape(n_l * n_e * ffe, d)]
        cast_in = [slab(gu_rows, ffe), slab(gu_rows, ffe), slab(d_rows, d)]
        cast_out = [pl.BlockSpec((gu_rows, 2 * ffe), lambda i: (i, 0)),
                    pl.BlockSpec((d_rows, d), lambda i: (i, 0))]
        cast_shapes = [jax.ShapeDtypeStruct((n_e * d, 2 * ffe), BF16),
                       jax.ShapeDtypeStruct((n_e * ffe, d), BF16)]
    out = pl.pallas_call(
        functools.partial(_ffn_body, final_norm=final_norm, n_cast=len(cast_in)),
        grid=(steps,),
        in_specs=[row, _full((1, d)), once((d, ff)), once((d, ff)), once((ff, d))]
        + [_full((1, d))] * final_norm + cast_in,
        out_specs=[row] + cast_out,
        out_shape=[jax.ShapeDtypeStruct((rows, d), F32)] + cast_shapes,
        compiler_params=_params(1),
        name="ffn",
    )(x, g, wg, wu, wd, *([final_g] * final_norm), *cast_args)
    if experts is None:
        return out[0], None
    return out[0], (out[1].reshape(n_e, d, 2 * ffe), out[2].reshape(n_e, ffe, d))


def _split_bf16(x):
    hi = x.astype(BF16)
    lo = (x - hi.astype(F32)).astype(BF16)
    return hi, lo


def _router_comb(hf, wrt_ref, brt_ref):
    h_hi, h_lo = _split_bf16(hf)
    w_hi, w_lo = _split_bf16(wrt_ref[...])
    nt = lambda a, b: lax.dot_general(a, b, (((1,), (1,)), ((), ())), preferred_element_type=F32)
    logits = nt(w_hi, h_hi) + (nt(w_lo, h_hi) + nt(w_hi, h_lo)) + brt_ref[...]
    n_e = logits.shape[0]
    row = lax.broadcasted_iota(jnp.int32, logits.shape, 0).astype(F32)
    far = float(n_e)
    m1 = jnp.max(logits, axis=0, keepdims=True)
    i1 = jnp.min(jnp.where(logits == m1, row, far), axis=0, keepdims=True)
    rest = jnp.where(row == i1, -jnp.inf, logits)
    m2 = jnp.max(rest, axis=0, keepdims=True)
    i2 = jnp.min(jnp.where(rest == m2, row, far), axis=0, keepdims=True)
    e2 = jnp.exp(m2 - m1)
    den = 1.0 + e2
    comb = jnp.where(row == i1, 1.0 / den, 0.0) + jnp.where(row == i2, e2 / den, 0.0)
    sel = jnp.where((row == i1) | (row == i2), 1.0, 0.0)
    return comb, sel


def _moe_body(x_ref, g_ref, wr_ref, br_ref, wgu_ref, wd_ref, *rest, final_norm):
    gf_ref = rest[0] if final_norm else None
    o_ref, h_scr, gate_t_scr, rank_t_scr, upper_scr, cnt_smem = rest[int(final_norm):]
    i = pl.program_id(0)
    e = pl.program_id(1)
    tm = x_ref.shape[0]
    ff = wd_ref.shape[0]

    @pl.when((i == 0) & (e == 0))
    def _():
        ri = lax.broadcasted_iota(jnp.int32, (tm, tm), 0)
        ci = lax.broadcasted_iota(jnp.int32, (tm, tm), 1)
        upper_scr[...] = jnp.where(ri < ci, 1.0, 0.0).astype(BF16)

    @pl.when(e == 0)
    def _():
        x = x_ref[...]
        hf = _rms(x, g_ref[...])
        h_scr[...] = hf.astype(BF16)
        comb, sel = _router_comb(hf, wr_ref, br_ref)
        rank = jnp.dot(sel.astype(BF16), upper_scr[...], preferred_element_type=F32)
        rank_t_scr[...] = jnp.where(sel > 0.0, rank, -1.0)
        gate_t_scr[...] = comb
        cnt = jnp.sum(sel, axis=1, keepdims=True)
        tail0 = tm - min(MOE_TAIL_WINDOW, tm)
        for k in range(N_EXPERTS):
            cnt_smem[k] = cnt[k, 0].astype(jnp.int32)
            cnt_smem[N_EXPERTS + k] = rank[k, tail0].astype(jnp.int32)
        o_ref[...] = x

    n_chunks = (cnt_smem[e] + (MOE_CHUNK - 1)) // MOE_CHUNK
    n_pairs = n_chunks // 2

    def run_chunk(first_slot, size, t0):
        base = first_slot.astype(F32)
        slot_col = lax.broadcasted_iota(jnp.int32, (size, 1), 0).astype(F32) + base
        hit = rank_t_scr[pl.ds(e, 1), t0:] == slot_col
        onehot = jnp.where(hit, 1.0, 0.0).astype(BF16)
        xe = jnp.dot(onehot, h_scr[t0:, :], preferred_element_type=F32).astype(BF16)
        ab = jnp.dot(xe, wgu_ref[...], preferred_element_type=F32)
        m = (jax.nn.silu(ab[:, :ff]) * ab[:, ff:]).astype(BF16)
        gate = jnp.sum(jnp.where(hit, gate_t_scr[pl.ds(e, 1), t0:], 0.0), axis=-1, keepdims=True)
        ye = (jnp.dot(m, wd_ref[...], preferred_element_type=F32) * gate).astype(BF16)
        sr = min(MOE_SCATTER_ROWS, tm)
        for r0 in range(t0, tm, sr):
            o_ref[r0:r0 + sr, :] += lax.dot_general(
                onehot[:, r0 - t0:r0 - t0 + sr], ye, (((0,), (0,)), ((), ())),
                preferred_element_type=F32)

    def pair(c, carry):
        run_chunk(c * (2 * MOE_CHUNK), 2 * MOE_CHUNK, 0)
        return carry

    lax.fori_loop(0, n_pairs, pair, 0)

    tail_slot = n_pairs * (2 * MOE_CHUNK)
    has_tail = n_chunks % 2 == 1
    in_window = cnt_smem[N_EXPERTS + e] <= tail_slot

    @pl.when(has_tail & in_window)
    def _():
        run_chunk(tail_slot, MOE_CHUNK, tm - min(MOE_TAIL_WINDOW, tm))

    @pl.when(has_tail & jnp.logical_not(in_window))
    def _():
        run_chunk(tail_slot, MOE_CHUNK, 0)

    if final_norm:
        @pl.when(e == pl.num_programs(1) - 1)
        def _():
            o_ref[...] = _rms(o_ref[...], gf_ref[...])


def _moe(x, g, wr_pad, br_pad, wgu, wd, final_g):
    rows, d = x.shape
    n_e, ff, _ = wd.shape
    tm = min(MOE_ROW_TILE, rows)
    row = pl.BlockSpec((tm, d), lambda i, e: (i, 0))
    final_norm = final_g is not None
    return pl.pallas_call(
        functools.partial(_moe_body, final_norm=final_norm),
        grid=(rows // tm, n_e),
        in_specs=[row, _full((1, d)), _full((n_e, d)), _full((n_e, 1)),
                  pl.BlockSpec((None, d, 2 * ff), lambda i, e: (e, 0, 0)),
                  pl.BlockSpec((None, ff, d), lambda i, e: (e, 0, 0))]
        + [_full((1, d))] * final_norm,
        out_specs=row,
        out_shape=jax.ShapeDtypeStruct((rows, d), F32),
        scratch_shapes=[pltpu.VMEM((tm, d), BF16),
                        pltpu.VMEM((n_e, tm), F32),
                        pltpu.VMEM((n_e, tm), F32),
                        pltpu.VMEM((tm, tm), BF16),
                        pltpu.SMEM((2 * N_EXPERTS,), jnp.int32)],
        compiler_params=_params(2),
        name="moe",
    )(x, g, wr_pad, br_pad, wgu, wd, *([final_g] * final_norm))


def _trunk(x, p, past, dense_w, expert_w):
    b, l, d = x.shape
    depth = p['w_in'].shape[0]
    rows = b * l
    xr = x.reshape(rows, d)
    prompt = past is None
    act_dtype = BF16 if prompt else F32
    new = {'k': [], 'v': [], 're': [], 'im': [], 'gv': []}
    sp = p['ssm']
    ws, bs = _gmlp_params(p, min(l, CHUNK))
    if not prompt:
        w = past[0].shape[2]
        kc_t = past[0].transpose(0, 1, 3, 4, 2).reshape(depth, b, KV_W, w)
        vc_t = past[1].transpose(0, 1, 3, 4, 2).reshape(depth, b, KV_W, w)
        k_out = jnp.zeros_like(kc_t)
        v_out = jnp.zeros_like(vc_t)
        h0_t = tuple(s.transpose(0, 2, 3, 1).reshape(depth, SSM_FLAT, b) for s in past[2:])
    for i in range(depth):
        g_mix = p['norm_mix_g'][i].reshape(1, d)
        gv = p['gmlp_v_norm_g'][i].reshape(1, GM_W)
        q, k, v, u, gu, gvn = _inproj(xr, g_mix, p['w_in_bf16'], gv, i, act_dtype)
        u = u.reshape(b, l, SSM_W)
        sinks = p['attn_sinks'][i]
        if prompt:
            o_attn = _attn_prompt(q, k, v, sinks, batch=b, seq=l)
            nw = min(WINDOW, l)
            k_win = k.reshape(b, l, N_KV_HEADS, HEAD_DIM)[:, l - nw:]
            v_win = v.reshape(b, l, N_KV_HEADS, HEAD_DIM)[:, l - nw:]
            dense_todo = i % 2 == 0 and i // 2 not in dense_w
            o_ssm, s_fin, cast = _ssm(
                u, None, sp, i, tc=CHUNK, out_dtype=act_dtype,
                cast=((p['ffn_w_gate'], p['ffn_w_up'], p['ffn_w_down']), i // 2)
                if dense_todo else None)
            if dense_todo:
                dense_w[i // 2] = cast
            s_re = s_fin[:, :SSM_FLAT].reshape(b, SSM_GROUPS, SSM_STATE)
            s_im = s_fin[:, SSM_FLAT:].reshape(b, SSM_GROUPS, SSM_STATE)
        else:
            o_attn, k_out, v_out = _attn_sample(q, k, v, kc_t, vc_t, k_out, v_out, sinks, i,
                                                batch=b, t=l)
            o_ssm, s_fin, _ = _ssm(u, h0_t, sp, i, tc=l, out_dtype=act_dtype)
            s_re, s_im = (s.reshape(SSM_GROUPS, SSM_STATE, b).transpose(2, 0, 1) for s in s_fin)
            new['gv'].append(gvn.reshape(b, l, GM_HEADS, GM_HEAD_DIM))
        xr = _outproj(xr, o_attn, o_ssm.reshape(rows, SSM_W), gu, gvn, ws, bs,
                      p['mix_out_norm_g'][i].reshape(1, d), p['w_out_bf16'], i,
                      chunk=min(l, CHUNK))
        g_ffn = p['norm_ffn_g'][i].reshape(1, d)
        final_g = p['final_norm_g'].reshape(1, d) if i == depth - 1 else None
        j = i // 2
        if i % 2 == 0:
            todo = prompt and i + 1 < depth and j not in expert_w
            if j not in dense_w:
                dense_w[j] = tuple(p[n][j].astype(BF16)
                                   for n in ('ffn_w_gate', 'ffn_w_up', 'ffn_w_down'))
            xr, cast = _ffn(xr, g_ffn, *dense_w[j], final_g,
                            (p['moe_w_gate'], p['moe_w_up'], p['moe_w_down'], j) if todo else None)
            if todo:
                expert_w[j] = cast
        else:
            wr = p['moe_w_router'][j].T
            br = p['moe_b_router'][j].reshape(N_EXPERTS, 1)
            if j not in expert_w:
                expert_w[j] = (jnp.concatenate([p['moe_w_gate'][j].astype(BF16),
                                                p['moe_w_up'][j].astype(BF16)], axis=-1),
                               p['moe_w_down'][j].astype(BF16))
            xr = _moe(xr, g_ffn, wr, br, *expert_w[j], final_g)
        if prompt:
            new['k'].append(k_win)
            new['v'].append(v_win)
        new['re'].append(s_re)
        new['im'].append(s_im)
    for name in ('re', 'im', 'gv') + (('k', 'v') if prompt else ()):
        new[name] = jnp.stack(new[name]) if new[name] else None
    if not prompt:
        back = lambda c: c.reshape(depth, b, N_KV_HEADS, HEAD_DIM, w).transpose(0, 1, 4, 2, 3)
        new['k'], new['v'] = back(k_out), back(v_out)
    return xr.reshape(b, l, d), new


def kernel(x_prompt, x_sample, cache_k_win, cache_v_win, state_ssm_re, state_ssm_im,
           norm_mix_g, w_in, attn_sinks, ssm_a_re, ssm_a_im, ssm_log_dt, ssm_b_re, ssm_b_im,
           ssm_c_re, ssm_c_im, ssm_d, ssm_w_glu, ssm_b_glu, gmlp_v_norm_g, gmlp_w_s, gmlp_b_s,
           mix_out_norm_g, w_out, norm_ffn_g, ffn_w_gate, ffn_w_up, ffn_w_down,
           moe_w_router, moe_b_router, moe_w_gate, moe_w_up, moe_w_down, final_norm_g):
    params = {
        'norm_mix_g': norm_mix_g, 'w_in': w_in, 'attn_sinks': attn_sinks,
        'ssm_a_re': ssm_a_re, 'ssm_a_im': ssm_a_im, 'ssm_log_dt': ssm_log_dt,
        'ssm_b_re': ssm_b_re, 'ssm_b_im': ssm_b_im, 'ssm_c_re': ssm_c_re, 'ssm_c_im': ssm_c_im,
        'ssm_d': ssm_d, 'ssm_w_glu': ssm_w_glu, 'ssm_b_glu': ssm_b_glu,
        'gmlp_v_norm_g': gmlp_v_norm_g, 'gmlp_w_s': gmlp_w_s, 'gmlp_b_s': gmlp_b_s,
        'mix_out_norm_g': mix_out_norm_g, 'w_out': w_out, 'norm_ffn_g': norm_ffn_g,
        'ffn_w_gate': ffn_w_gate, 'ffn_w_up': ffn_w_up, 'ffn_w_down': ffn_w_down,
        'moe_w_router': moe_w_router, 'moe_b_router': moe_b_router,
        'moe_w_gate': moe_w_gate, 'moe_w_up': moe_w_up, 'moe_w_down': moe_w_down,
        'final_norm_g': final_norm_g,
    }
    for name in ('w_in', 'w_out'):
        params[name + '_bf16'] = params[name].astype(BF16)
    params['ssm'] = _ssm_params(params)
    dense_w, expert_w = {}, {}
    y_p, st_p = _trunk(x_prompt, params, None, dense_w, expert_w)
    y_s, st_s = _trunk(x_sample, params,
                       (cache_k_win, cache_v_win, state_ssm_re, state_ssm_im), dense_w, expert_w)
    return (y_p, y_s,
            st_p['k'], st_p['v'], st_p['re'], st_p['im'],
            st_s['k'], st_s['v'], st_s['re'], st_s['im'], st_s['gv'])
```

```python
import functools
import math

import jax
import jax.numpy as jnp
from jax import lax
from jax.experimental import pallas as pl
from jax.experimental.pallas import tpu as pltpu

F32 = jnp.float32
BF16 = jnp.bfloat16

EPS = 1e-6
HEAD_DIM = 64
N_HEADS = 8
N_KV_HEADS = 2
GQA_GROUP = N_HEADS // N_KV_HEADS
ATTN_W = N_HEADS * HEAD_DIM
KV_W = N_KV_HEADS * HEAD_DIM
WINDOW = 128
ATTN_SCALE = 1.0 / math.sqrt(HEAD_DIM)
LOG2E = math.log2(math.e)
SSM_W = 256
SSM_GROUPS = 16
SSM_STATE = 64
SSM_FLAT = SSM_GROUPS * SSM_STATE
GM_W = 256
GM_HEADS = 4
GM_HEAD_DIM = 64
CHUNK = 128
Q_END = ATTN_W
K_END = Q_END + KV_W
V_END = K_END + KV_W
S_END = V_END + SSM_W
IN_COLS = S_END + 2 * GM_W
N_EXPERTS = 8

VMEM_LIMIT_BYTES = 56 * 1024 * 1024
ROW_TILE = 512
SUB_ROWS = 128
OUTPROJ_PASS_ROWS = 256
ATTN_Q_BLOCKS = 4
MOE_ROW_TILE = 1024
MOE_CHUNK = 128
MOE_SCATTER_ROWS = 256
MOE_TAIL_WINDOW = 256
SCAN_LANES = 512


def _params(n_axes):
    return pltpu.CompilerParams(
        dimension_semantics=("arbitrary",) * n_axes,
        vmem_limit_bytes=VMEM_LIMIT_BYTES)


def _rms(x, g):
    return x * lax.rsqrt(jnp.mean(x * x, axis=-1, keepdims=True) + EPS) * g


def _full(shape):
    return pl.BlockSpec(shape, lambda *_: (0,) * len(shape))


def _group_mean(a, n_groups, width):
    lane = lax.broadcasted_iota(jnp.int32, (1, n_groups * width), 1)
    out = jnp.zeros_like(a)
    for h in range(n_groups):
        m = (lane >= h * width) & (lane < (h + 1) * width)
        s = jnp.sum(jnp.where(m, a, 0.0), axis=-1, keepdims=True) * (1.0 / width)
        out = jnp.where(m, s, out)
    return out


def _inproj_body(x_ref, g_ref, w_ref, gv_ref, q_ref, k_ref, v_ref, u_ref, gu_ref, gvn_ref):
    for r0 in range(0, x_ref.shape[0], SUB_ROWS):
        rs = slice(r0, r0 + SUB_ROWS)
        h = _rms(x_ref[rs, :], g_ref[...])
        z = jnp.dot(h.astype(BF16), w_ref[...], preferred_element_type=F32)
        q_ref[rs, :] = (z[:, :Q_END] * (ATTN_SCALE * LOG2E)).astype(BF16)
        k_ref[rs, :] = z[:, Q_END:K_END]
        v_ref[rs, :] = z[:, K_END:V_END]
        u_ref[rs, :] = z[:, V_END:S_END]
        g = jax.nn.gelu(z[:, S_END:])
        gu_ref[rs, :] = g[:, :GM_W].astype(gu_ref.dtype)
        vv = g[:, GM_W:]
        mu = _group_mean(vv, GM_HEADS, GM_HEAD_DIM)
        var = _group_mean(jnp.square(vv - mu), GM_HEADS, GM_HEAD_DIM)
        gvn_ref[rs, :] = ((vv - mu) * lax.rsqrt(var + EPS) * gv_ref[...]).astype(gvn_ref.dtype)


def _layer(shape, j):
    return pl.BlockSpec((None,) + tuple(shape), lambda *_: (j,) + (0,) * len(shape))


def _inproj(x, g, w_bf16, gv, j, act_dtype):
    rows, d = x.shape
    tm = min(ROW_TILE, rows)
    nt = rows // tm
    row = lambda w: pl.BlockSpec((tm, w), lambda i: (i, 0))
    return pl.pallas_call(
        _inproj_body,
        grid=(nt,),
        in_specs=[row(d), _full((1, d)), _layer((d, IN_COLS), j), _full((1, GM_W))],
        out_specs=[row(ATTN_W), row(KV_W), row(KV_W), row(SSM_W), row(GM_W), row(GM_W)],
        out_shape=[jax.ShapeDtypeStruct((rows, ATTN_W), BF16),
                   jax.ShapeDtypeStruct((rows, KV_W), F32),
                   jax.ShapeDtypeStruct((rows, KV_W), F32),
                   jax.ShapeDtypeStruct((rows, SSM_W), F32),
                   jax.ShapeDtypeStruct((rows, GM_W), act_dtype),
                   jax.ShapeDtypeStruct((rows, GM_W), act_dtype)],
        compiler_params=_params(1),
        name="inproj",
    )(x, g, w_bf16, gv)


def _sink_softmax(s, mask, sink):
    s = jnp.where(mask, s, -jnp.inf)
    m = jnp.maximum(jnp.max(s, axis=-1, keepdims=True), sink)
    e = jnp.exp2(s - m)
    return e / (jnp.sum(e, axis=-1, keepdims=True) + jnp.exp2(sink - m))


def _head_pair_operands(x):
    low = lax.broadcasted_iota(jnp.int32, (1, KV_W), 1) < HEAD_DIM
    swapped = pltpu.roll(x, HEAD_DIM, axis=1)
    keep = lambda a, in_low: jnp.where(low == in_low, a, 0.0).astype(BF16)
    return ((keep(x, True), keep(swapped, False)),
            (keep(swapped, True), keep(x, False)))


def _attn_prompt_body(sink_ref, q_ref, kc_ref, kp_ref, vc_ref, vp_ref, o_ref, *, nq):
    n = pl.program_id(1)
    qi = lax.broadcasted_iota(jnp.int32, (WINDOW, 2 * WINDOW), 0)
    si = lax.broadcasted_iota(jnp.int32, (WINDOW, 2 * WINDOW), 1)
    dist = WINDOW + qi - si
    band = (dist >= 0) & (dist < WINDOW)
    first = band & ((n > 0) | (si >= WINDOW))
    k_ops = _head_pair_operands(jnp.concatenate([kp_ref[...], kc_ref[...]], axis=0))
    v_ops = _head_pair_operands(jnp.concatenate([vp_ref[...], vc_ref[...]], axis=0))
    pair_w = 2 * HEAD_DIM
    for j in range(nq):
        mask = first if j == 0 else band
        q_rows = slice(j * WINDOW, (j + 1) * WINDOW)
        kv_rows = slice(j * WINDOW, (j + 2) * WINDOW)
        for hp in range(N_HEADS // 2):
            kh = (2 * hp) // GQA_GROUP
            qp = q_ref[q_rows, hp * pair_w:(hp + 1) * pair_w]
            acc = None
            for par in range(2):
                sink = sink_ref[2 * hp + par] * LOG2E
                s = lax.dot_general(qp, k_ops[kh][par][kv_rows], (((1,), (1,)), ((), ())),
                                    preferred_element_type=F32)
                s = jnp.where(mask, s, -jnp.inf)
                m = jnp.maximum(jnp.max(s, axis=-1, keepdims=True), sink)
                e = jnp.exp2(s - m)
                den = jnp.sum(e, axis=-1, keepdims=True) + jnp.exp2(sink - m)
                pv = jnp.dot(e.astype(BF16), v_ops[kh][par][kv_rows],
                             preferred_element_type=F32)
                pv = pv * (1.0 / den)
                acc = pv if acc is None else acc + pv
            o_ref[q_rows, hp * pair_w:(hp + 1) * pair_w] = acc.astype(o_ref.dtype)


def _attn_prompt(q, k, v, sinks, *, batch, seq):
    nq = ATTN_Q_BLOCKS
    nb = seq // WINDOW
    steps = nb // nq
    cur = lambda w: pl.BlockSpec((nq * WINDOW, w), lambda b, n: (b * steps + n, 0))
    prev = lambda w: pl.BlockSpec(
        (WINDOW, w), lambda b, n: (b * nb + jnp.maximum(n * nq - 1, 0), 0))
    return pl.pallas_call(
        functools.partial(_attn_prompt_body, nq=nq),
        grid=(batch, steps),
        in_specs=[pl.BlockSpec(memory_space=pltpu.SMEM),
                  cur(ATTN_W), cur(KV_W), prev(KV_W), cur(KV_W), prev(KV_W)],
        out_specs=cur(ATTN_W),
        out_shape=jax.ShapeDtypeStruct((batch * seq, ATTN_W), BF16),
        compiler_params=_params(2),
        name="attn_prompt",
    )(sinks, q, k, k, v, v)


def _attn_sample_body(sink_ref, q_ref, k_ref, v_ref, kc_ref, vc_ref, kin_ref, vin_ref,
                      o_ref, ko_ref, vo_ref, *, bb, t):
    del kin_ref, vin_ref
    w = kc_ref.shape[-1]
    pair_w = 2 * HEAD_DIM
    lane = lax.broadcasted_iota(jnp.int32, (1, w), 1)
    new_col = lane < t
    k_new_t = k_ref[...].T
    v_new_t = v_ref[...].T
    qf = q_ref[...].astype(F32)
    unit = 2 * t
    upper = lax.broadcasted_iota(jnp.int32, (unit, 1), 0) < t
    zeros_half = jnp.zeros((HEAD_DIM, 2 * w), BF16)
    placed = lambda a, par: jnp.concatenate((a, zeros_half) if par == 0 else (zeros_half, a), axis=0)
    scores, sinks, values = [], [], []
    for b in range(bb):
        shift = (w - b * t) % w
        k_cols = jnp.where(new_col, pltpu.roll(k_new_t, shift, axis=1), 0.0)
        v_cols = jnp.where(new_col, pltpu.roll(v_new_t, shift, axis=1), 0.0)
        k_old = kc_ref[b]
        v_old = vc_ref[b]
        ko_ref[b] = pltpu.roll(jnp.where(new_col, k_cols, k_old), w - t, axis=1)
        vo_ref[b] = pltpu.roll(jnp.where(new_col, v_cols, v_old), w - t, axis=1)
        k_all = jnp.concatenate([k_old, k_cols], axis=1).astype(BF16)
        v_all = jnp.concatenate([v_old, v_cols], axis=1).astype(BF16)
        for kh in range(N_KV_HEADS):
            kv_sl = slice(kh * HEAD_DIM, (kh + 1) * HEAD_DIM)
            q2 = jnp.concatenate(
                [qf[b * t:(b + 1) * t, (2 * kh + c) * pair_w:(2 * kh + c + 1) * pair_w]
                 for c in range(2)], axis=0).astype(BF16)
            for par in range(2):
                h_a = GQA_GROUP * kh + par
                sinks.append(jnp.where(upper, sink_ref[h_a], sink_ref[h_a + 2]) * LOG2E)
                scores.append(jnp.dot(q2, placed(k_all[kv_sl], par),
                                      preferred_element_type=F32))
                values.append(placed(v_all[kv_sl], par))
    s = jnp.concatenate(scores, axis=0)
    ti = lax.broadcasted_iota(jnp.int32, s.shape, 0) % t
    si = lax.broadcasted_iota(jnp.int32, s.shape, 1)
    dist = w + ti - si
    p = _sink_softmax(s, (dist >= 0) & (dist < WINDOW), jnp.concatenate(sinks, axis=0)).astype(BF16)
    for b in range(bb):
        for kh in range(N_KV_HEADS):
            acc = None
            for par in range(2):
                u = (b * N_KV_HEADS + kh) * 2 + par
                pv = lax.dot_general(p[u * unit:(u + 1) * unit], values[u],
                                     (((1,), (1,)), ((), ())), preferred_element_type=F32)
                acc = pv if acc is None else acc + pv
            for c in range(2):
                o_ref[b * t:(b + 1) * t, (2 * kh + c) * pair_w:(2 * kh + c + 1) * pair_w] = (
                    acc[c * t:(c + 1) * t])


def _attn_sample(q, k, v, kc_t, vc_t, k_out, v_out, sinks, layer, *, batch, t):
    w = kc_t.shape[-1]
    bb = w // t
    row = lambda c: pl.BlockSpec((bb * t, c), lambda i: (i, 0))
    buf = pl.BlockSpec((None, bb, KV_W, w), lambda i: (layer, i, 0, 0))
    anywhere = pl.BlockSpec(memory_space=pl.ANY)
    return pl.pallas_call(
        functools.partial(_attn_sample_body, bb=bb, t=t),
        grid=(batch // bb,),
        in_specs=[pl.BlockSpec(memory_space=pltpu.SMEM),
                  row(ATTN_W), row(KV_W), row(KV_W), buf, buf, anywhere, anywhere],
        out_specs=[row(ATTN_W), buf, buf],
        out_shape=[jax.ShapeDtypeStruct((batch * t, ATTN_W), F32),
                   jax.ShapeDtypeStruct(k_out.shape, F32),
                   jax.ShapeDtypeStruct(v_out.shape, F32)],
        input_output_aliases={6: 1, 7: 2},
        compiler_params=_params(1),
        name="attn_sample",
    )(sinks, q, k, v, kc_t, vc_t, k_out, v_out)


def _ssm_body(u_ref, *refs, r, tc, has_h0, n_cast):
    h0_refs, refs = refs[:2 * has_h0], refs[2 * has_h0:]
    (are_ref, aim_ref, ldt_ref, bre_ref, bim_ref, cre_ref, cim_ref, d_ref, wglu_ref,
     bglu_ref), refs = refs[:10], refs[10:]
    cast_in, (o_ref, s_ref), refs = refs[:n_cast], refs[n_cast:n_cast + 2], refs[n_cast + 2:]
    cast_out, (abar_scr, bbar_scr, h_scr, *x_scrs) = refs[:n_cast], refs[n_cast:]
    c = pl.program_id(0)
    for src, dst in zip(cast_in, cast_out):
        dst[...] = src[...].astype(BF16)

    @pl.when(c == 0)
    def _():
        ar = are_ref[...]
        ai = aim_ref[...]
        dt = jnp.exp(ldt_ref[...])
        decay = jnp.exp(dt * ar)
        abr = decay * jnp.cos(dt * ai)
        abi = decay * jnp.sin(dt * ai)
        den = ar * ar + ai * ai
        nr = abr - 1.0
        fr = (nr * ar + abi * ai) / den
        fi = (abi * ar - nr * ai) / den
        abar_scr[0:1, :] = abr
        abar_scr[1:2, :] = abi
        br = bre_ref[...]
        bi = bim_ref[...]
        bbar_scr[:, :SSM_FLAT] = (fr * br - fi * bi).astype(BF16)
        bbar_scr[:, SSM_FLAT:] = (fr * bi + fi * br).astype(BF16)
        if has_h0:
            h_scr[:, :SSM_FLAT] = h0_refs[0][...].T
            h_scr[:, SSM_FLAT:] = h0_refs[1][...].T
        else:
            h_scr[...] = jnp.zeros_like(h_scr)

    u = jnp.swapaxes(u_ref[...], 0, 1).reshape(tc * r, SSM_W)
    ub = u.astype(BF16)

    y = None
    for lc in range(SSM_FLAT // SCAN_LANES):
        lanes = slice(lc * SCAN_LANES, (lc + 1) * SCAN_LANES)
        re_sl = lanes
        im_sl = slice(SSM_FLAT + lc * SCAN_LANES, SSM_FLAT + (lc + 1) * SCAN_LANES)
        xr_scr, xi_scr = x_scrs[2 * lc], x_scrs[2 * lc + 1]
        xr_scr[...] = jnp.dot(ub, bbar_scr[:, re_sl], preferred_element_type=F32)
        xi_scr[...] = jnp.dot(ub, bbar_scr[:, im_sl], preferred_element_type=F32)
        ar = jnp.broadcast_to(abar_scr[0:1, lanes], (8, SCAN_LANES))
        ai = jnp.broadcast_to(abar_scr[1:2, lanes], (8, SCAN_LANES))
        for rg in range(r // 8):
            rg_sl = slice(rg * 8, (rg + 1) * 8)
            hr, hi = h_scr[rg_sl, re_sl], h_scr[rg_sl, im_sl]
            for t in range(tc):
                rows = slice(t * r + rg * 8, t * r + rg * 8 + 8)
                hr, hi = (ar * hr - ai * hi + xr_scr[rows, :],
                          ar * hi + ai * hr + xi_scr[rows, :])
                xr_scr[rows, :] = hr
                xi_scr[rows, :] = hi
            h_scr[rg_sl, re_sl] = hr
            h_scr[rg_sl, im_sl] = hi
        part = (jnp.dot(xr_scr[...].astype(BF16), cre_ref[lanes, :], preferred_element_type=F32)
                - jnp.dot(xi_scr[...].astype(BF16), cim_ref[lanes, :],
                          preferred_element_type=F32))
        y = part if y is None else y + part
    y = jax.nn.gelu(y + d_ref[...] * u)
    gl = jnp.dot(y.astype(BF16), wglu_ref[...], preferred_element_type=F32) + bglu_ref[...]
    o_ref[...] = jnp.swapaxes((y * jax.nn.sigmoid(gl)).reshape(tc, r, SSM_W), 0, 1).astype(
        o_ref.dtype)

    @pl.when(c == pl.num_programs(0) - 1)
    def _():
        if has_h0:
            s_ref[0] = h_scr[:, :SSM_FLAT].T
            s_ref[1] = h_scr[:, SSM_FLAT:].T
        else:
            s_ref[...] = h_scr[...]


def _cast_slabs(stacks, layer, steps):
    args, in_specs, out_specs, out_shapes = [], [], [], []
    for w in stacks:
        n_l, rows, cols = w.shape
        slab = rows // steps
        args.append(w.reshape(n_l * rows, cols))
        in_specs.append(pl.BlockSpec((slab, cols), lambda i: (layer * steps + i, 0)))
        out_specs.append(pl.BlockSpec((slab, cols), lambda i: (i, 0)))
        out_shapes.append(jax.ShapeDtypeStruct((rows, cols), BF16))
    return args, in_specs, out_specs, out_shapes


def _ssm(u, h0_t, sp, layer, *, tc, out_dtype, cast=None):
    r, steps, _ = u.shape
    blk = tc * r
    has_h0 = h0_t is not None
    seq_blk = pl.BlockSpec((r, tc, SSM_W), lambda c: (0, c, 0))
    state_shape = (2, SSM_FLAT, r) if has_h0 else (r, 2 * SSM_FLAT)
    c_args, c_in, c_out, c_shapes = _cast_slabs(*cast, steps // tc) if cast else ([], [], [], [])
    out = pl.pallas_call(
        functools.partial(_ssm_body, r=r, tc=tc, has_h0=has_h0, n_cast=len(c_args)),
        grid=(steps // tc,),
        in_specs=[seq_blk]
        + [_layer((SSM_FLAT, r), layer)] * (2 * has_h0)
        + [_layer((1, SSM_FLAT), layer)] * 3
        + [_layer((SSM_W, SSM_FLAT), layer)] * 2
        + [_layer((SSM_FLAT, SSM_W), layer)] * 2
        + [_layer((1, SSM_W), layer), _layer((SSM_W, SSM_W), layer), _layer((1, SSM_W), layer)]
        + c_in,
        out_specs=[seq_blk, _full(state_shape)] + c_out,
        out_shape=[jax.ShapeDtypeStruct((r, steps, SSM_W), out_dtype),
                   jax.ShapeDtypeStruct(state_shape, F32)] + c_shapes,
        scratch_shapes=[pltpu.VMEM((2, SSM_FLAT), F32),
                        pltpu.VMEM((SSM_W, 2 * SSM_FLAT), BF16),
                        pltpu.VMEM((r, 2 * SSM_FLAT), F32)]
        + [pltpu.VMEM((blk, SCAN_LANES), F32)] * (2 * SSM_FLAT // SCAN_LANES),
        compiler_params=_params(1),
        name="ssm",
    )(u, *(h0_t if has_h0 else ()), sp['a_re'], sp['a_im'], sp['log_dt'], sp['b_re'], sp['b_im'],
      sp['c_re'], sp['c_im'], sp['d'], sp['w_glu'], sp['b_glu'], *c_args)
    return out[0], out[1], tuple(out[2:])


def _ssm_params(p):
    eye = jnp.eye(SSM_GROUPS, dtype=F32)
    depth = p['ssm_a_re'].shape[0]

    def b_blockdiag(b):
        return jnp.einsum('lgpc,gh->lgchp', b, eye).reshape(depth, SSM_W, SSM_FLAT)

    def c_blockdiag(c):
        return jnp.einsum('lgcp,gh->lgphc', c, eye).reshape(depth, SSM_FLAT, SSM_W)

    return {
        'a_re': p['ssm_a_re'].reshape(depth, 1, SSM_FLAT),
        'a_im': p['ssm_a_im'].reshape(depth, 1, SSM_FLAT),
        'log_dt': jnp.repeat(p['ssm_log_dt'], SSM_STATE, axis=1).reshape(depth, 1, SSM_FLAT),
        'b_re': b_blockdiag(p['ssm_b_re']),
        'b_im': b_blockdiag(p['ssm_b_im']),
        'c_re': c_blockdiag(p['ssm_c_re']).astype(BF16),
        'c_im': c_blockdiag(p['ssm_c_im']).astype(BF16),
        'd': p['ssm_d'].reshape(depth, 1, SSM_W),
        'w_glu': p['ssm_w_glu'].astype(BF16),
        'b_glu': p['ssm_b_glu'].reshape(depth, 1, SSM_W),
    }


def _outproj_body(x_ref, oa_ref, os_ref, gu_ref, gvn_ref, ws_ref, bs_ref, go_ref, w_ref, o_ref,
                  *, chunk):
    tm = x_ref.shape[0]
    ri = lax.broadcasted_iota(jnp.int32, (CHUNK, CHUNK), 0)
    ci = lax.broadcasted_iota(jnp.int32, (CHUNK, CHUNK), 1)
    causal = (ri // chunk == ci // chunk) & (ci <= ri)
    lane = lax.broadcasted_iota(jnp.int32, (1, GM_W), 1)
    w_heads = [jnp.where(causal, ws_ref[h], 0.0).astype(BF16) for h in range(GM_HEADS)]
    go = go_ref[...]
    per_pass = OUTPROJ_PASS_ROWS // CHUNK
    for r0 in range(0, tm, OUTPROJ_PASS_ROWS):
        rs = slice(r0, r0 + OUTPROJ_PASS_ROWS)
        zs = []
        for cblk in range(per_pass):
            vn = gvn_ref[r0 + cblk * CHUNK:r0 + (cblk + 1) * CHUNK, :].astype(BF16)
            z = bs_ref[...]
            for h in range(GM_HEADS):
                z_h = jnp.dot(w_heads[h], vn, preferred_element_type=F32)
                head = (lane >= h * GM_HEAD_DIM) & (lane < (h + 1) * GM_HEAD_DIM)
                z = z + jnp.where(head, z_h, 0.0)
            zs.append(z)
        o = jnp.concatenate([
            _rms(oa_ref[rs, :].astype(F32), go[:, :ATTN_W]),
            _rms(os_ref[rs, :].astype(F32), go[:, ATTN_W:ATTN_W + SSM_W]),
            _rms(gu_ref[rs, :].astype(F32) * jnp.concatenate(zs, axis=0),
                 go[:, ATTN_W + SSM_W:])], axis=-1)
        o_ref[rs, :] = x_ref[rs, :] + jnp.dot(o.astype(BF16), w_ref[...],
                                              preferred_element_type=F32)


def _outproj(x, o_attn, o_ssm, gu, gvn, ws, bs, g_out, w_bf16, j, *, chunk):
    rows, d = x.shape
    tm = min(ROW_TILE, rows)
    row = lambda w: pl.BlockSpec((tm, w), lambda i: (i, 0))
    return pl.pallas_call(
        functools.partial(_outproj_body, chunk=chunk),
        grid=(rows // tm,),
        in_specs=[row(d), row(ATTN_W), row(SSM_W), row(GM_W), row(GM_W),
                  _layer((GM_HEADS, CHUNK, CHUNK), j), _layer((CHUNK, GM_W), j),
                  _full((1, d)), _layer((d, d), j)],
        out_specs=row(d),
        out_shape=jax.ShapeDtypeStruct((rows, d), F32),
        compiler_params=_params(1),
        name="outproj",
    )(x, o_attn, o_ssm, gu, gvn, ws, bs, g_out, w_bf16)


def _gmlp_params(p, chunk):
    ws = p['gmlp_w_s'][:, :, :chunk, :chunk]
    bs = p['gmlp_b_s'][:, :, :chunk]
    if chunk < CHUNK:
        pick = (jnp.arange(CHUNK)[:, None] % chunk == jnp.arange(chunk)[None, :]).astype(F32)
        ws = jnp.einsum('rc,lhcd,sd->lhrs', pick, ws, pick, precision=lax.Precision.HIGHEST)
        bs = jnp.einsum('rc,lhc->lhr', pick, bs, precision=lax.Precision.HIGHEST)
    bs = jnp.repeat(bs.transpose(0, 2, 1), GM_HEAD_DIM, axis=2)
    return ws, bs


def _swiglu_cols(h, wg_ref, wu_ref, wd_ref, n_chunks):
    ff = wg_ref.shape[-1]
    fc = ff // n_chunks
    acc = None
    for j in range(n_chunks):
        sl = slice(j * fc, (j + 1) * fc)
        a = jnp.dot(h, wg_ref[:, sl], preferred_element_type=F32)
        b = jnp.dot(h, wu_ref[:, sl], preferred_element_type=F32)
        m = (jax.nn.silu(a) * b).astype(BF16)
        y = jnp.dot(m, wd_ref[sl, :], preferred_element_type=F32)
        acc = y if acc is None else acc + y
    return acc


def _ffn_body(x_ref, g_ref, wg_ref, wu_ref, wd_ref, *rest, final_norm, n_cast):
    gf_ref = rest[0] if final_norm else None
    rest = rest[int(final_norm):]
    cast_in, o_ref, cast_out = rest[:n_cast], rest[n_cast], rest[n_cast + 1:]
    x = x_ref[...]
    h = _rms(x, g_ref[...]).astype(BF16)
    y = x + _swiglu_cols(h, wg_ref, wu_ref, wd_ref, 2)
    o_ref[...] = _rms(y, gf_ref[...]) if final_norm else y
    if n_cast:
        eg_ref, eu_ref, ed_ref = cast_in
        egu_ref, edo_ref = cast_out
        ffe = eg_ref.shape[-1]
        egu_ref[:, :ffe] = eg_ref[...].astype(BF16)
        egu_ref[:, ffe:] = eu_ref[...].astype(BF16)
        edo_ref[...] = ed_ref[...].astype(BF16)


def _ffn(x, g, wg, wu, wd, final_g, experts=None):
    rows, d = x.shape
    ff = wg.shape[-1]
    tm = min(ROW_TILE, rows)
    steps = rows // tm
    row = pl.BlockSpec((tm, d), lambda i: (i, 0))
    once = lambda shape: pl.BlockSpec(shape, lambda i: (0, 0), pipeline_mode=pl.Buffered(1))
    final_norm = final_g is not None
    cast_args, cast_in, cast_out, cast_shapes = [], [], [], []
    if experts is not None:
        eg, eu, ed, le = experts
        n_l, n_e, _, ffe = eg.shape
        gu_rows, d_rows = n_e * d // steps, n_e * ffe // steps
        slab = lambda r, c: pl.BlockSpec((r, c), lambda i: (le * steps + i, 0))
        cast_args = [eg.reshape(n_l * n_e * d, ffe), eu.reshape(n_l * n_e * d, ffe),
                     ed.reshape(n_l * n_e * ffe, d)]
        cast_in = [slab(gu_rows, ffe), slab(gu_rows, ffe), slab(d_rows, d)]
        cast_out = [pl.BlockSpec((gu_rows, 2 * ffe), lambda i: (i, 0)),
                    pl.BlockSpec((d_rows, d), lambda i: (i, 0))]
        cast_shapes = [jax.ShapeDtypeStruct((n_e * d, 2 * ffe), BF16),
                       jax.ShapeDtypeStruct((n_e * ffe, d), BF16)]
    out = pl.pallas_call(
        functools.partial(_ffn_body, final_norm=final_norm, n_cast=len(cast_in)),
        grid=(steps,),
        in_specs=[row, _full((1, d)), once((d, ff)), once((d, ff)), once((ff, d))]
        + [_full((1, d))] * final_norm + cast_in,
        out_specs=[row] + cast_out,
        out_shape=[jax.ShapeDtypeStruct((rows, d), F32)] + cast_shapes,
        compiler_params=_params(1),
        name="ffn",
    )(x, g, wg, wu, wd, *([final_g] * final_norm), *cast_args)
    if experts is None:
        return out[0], None
    return out[0], (out[1].reshape(n_e, d, 2 * ffe), out[2].reshape(n_e, ffe, d))


def _split_bf16(x):
    hi = x.astype(BF16)
    lo = (x - hi.astype(F32)).astype(BF16)
    return hi, lo


def _router_comb(hf, wrt_ref, brt_ref):
    h_hi, h_lo = _split_bf16(hf)
    w_hi, w_lo = _split_bf16(wrt_ref[...])
    nt = lambda a, b: lax.dot_general(a, b, (((1,), (1,)), ((), ())), preferred_element_type=F32)
    logits = nt(w_hi, h_hi) + (nt(w_lo, h_hi) + nt(w_hi, h_lo)) + brt_ref[...]
    n_e = logits.shape[0]
    row = lax.broadcasted_iota(jnp.int32, logits.shape, 0).astype(F32)
    far = float(n_e)
    m1 = jnp.max(logits, axis=0, keepdims=True)
    i1 = jnp.min(jnp.where(logits == m1, row, far), axis=0, keepdims=True)
    rest = jnp.where(row == i1, -jnp.inf, logits)
    m2 = jnp.max(rest, axis=0, keepdims=True)
    i2 = jnp.min(jnp.where(rest == m2, row, far), axis=0, keepdims=True)
    e2 = jnp.exp(m2 - m1)
    den = 1.0 + e2
    comb = jnp.where(row == i1, 1.0 / den, 0.0) + jnp.where(row == i2, e2 / den, 0.0)
    sel = jnp.where((row == i1) | (row == i2), 1.0, 0.0)
    return comb, sel


def _moe_body(x_ref, g_ref, wr_ref, br_ref, wgu_ref, wd_ref, *rest, final_norm):
    gf_ref = rest[0] if final_norm else None
    o_ref, h_scr, gate_t_scr, rank_t_scr, upper_scr, cnt_smem = rest[int(final_norm):]
    i = pl.program_id(0)
    e = pl.program_id(1)
    tm = x_ref.shape[0]
    ff = wd_ref.shape[0]

    @pl.when((i == 0) & (e == 0))
    def _():
        ri = lax.broadcasted_iota(jnp.int32, (tm, tm), 0)
        ci = lax.broadcasted_iota(jnp.int32, (tm, tm), 1)
        upper_scr[...] = jnp.where(ri < ci, 1.0, 0.0).astype(BF16)

    @pl.when(e == 0)
    def _():
        x = x_ref[...]
        hf = _rms(x, g_ref[...])
        h_scr[...] = hf.astype(BF16)
        comb, sel = _router_comb(hf, wr_ref, br_ref)
        rank = jnp.dot(sel.astype(BF16), upper_scr[...], preferred_element_type=F32)
        rank_t_scr[...] = jnp.where(sel > 0.0, rank, -1.0)
        gate_t_scr[...] = comb
        cnt = jnp.sum(sel, axis=1, keepdims=True)
        tail0 = tm - min(MOE_TAIL_WINDOW, tm)
        for k in range(N_EXPERTS):
            cnt_smem[k] = cnt[k, 0].astype(jnp.int32)
            cnt_smem[N_EXPERTS + k] = rank[k, tail0].astype(jnp.int32)
        o_ref[...] = x

    n_chunks = (cnt_smem[e] + (MOE_CHUNK - 1)) // MOE_CHUNK
    n_pairs = n_chunks // 2

    def run_chunk(first_slot, size, t0):
        base = first_slot.astype(F32)
        slot_col = lax.broadcasted_iota(jnp.int32, (size, 1), 0).astype(F32) + base
        hit = rank_t_scr[pl.ds(e, 1), t0:] == slot_col
        onehot = jnp.where(hit, 1.0, 0.0).astype(BF16)
        xe = jnp.dot(onehot, h_scr[t0:, :], preferred_element_type=F32).astype(BF16)
        ab = jnp.dot(xe, wgu_ref[...], preferred_element_type=F32)
        m = (jax.nn.silu(ab[:, :ff]) * ab[:, ff:]).astype(BF16)
        gate = jnp.sum(jnp.where(hit, gate_t_scr[pl.ds(e, 1), t0:], 0.0), axis=-1, keepdims=True)
        ye = (jnp.dot(m, wd_ref[...], preferred_element_type=F32) * gate).astype(BF16)
        sr = min(MOE_SCATTER_ROWS, tm)
        for r0 in range(t0, tm, sr):
            o_ref[r0:r0 + sr, :] += lax.dot_general(
                onehot[:, r0 - t0:r0 - t0 + sr], ye, (((0,), (0,)), ((), ())),
                preferred_element_type=F32)

    def pair(c, carry):
        run_chunk(c * (2 * MOE_CHUNK), 2 * MOE_CHUNK, 0)
        return carry

    lax.fori_loop(0, n_pairs, pair, 0)

    tail_slot = n_pairs * (2 * MOE_CHUNK)
    has_tail = n_chunks % 2 == 1
    in_window = cnt_smem[N_EXPERTS + e] <= tail_slot

    @pl.when(has_tail & in_window)
    def _():
        run_chunk(tail_slot, MOE_CHUNK, tm - min(MOE_TAIL_WINDOW, tm))

    @pl.when(has_tail & jnp.logical_not(in_window))
    def _():
        run_chunk(tail_slot, MOE_CHUNK, 0)

    if final_norm:
        @pl.when(e == pl.num_programs(1) - 1)
        def _():
            o_ref[...] = _rms(o_ref[...], gf_ref[...])


def _moe(x, g, wr_t, br_t, wgu, wd, final_g):
    rows, d = x.shape
    n_e, ff, _ = wd.shape
    tm = min(MOE_ROW_TILE, rows)
    row = pl.BlockSpec((tm, d), lambda i, e: (i, 0))
    final_norm = final_g is not None
    return pl.pallas_call(
        functools.partial(_moe_body, final_norm=final_norm),
        grid=(rows // tm, n_e),
        in_specs=[row, _full((1, d)), _full((n_e, d)), _full((n_e, 1)),
                  pl.BlockSpec((None, d, 2 * ff), lambda i, e: (e, 0, 0)),
                  pl.BlockSpec((None, ff, d), lambda i, e: (e, 0, 0))]
        + [_full((1, d))] * final_norm,
        out_specs=row,
        out_shape=jax.ShapeDtypeStruct((rows, d), F32),
        scratch_shapes=[pltpu.VMEM((tm, d), BF16),
                        pltpu.VMEM((n_e, tm), F32),
                        pltpu.VMEM((n_e, tm), F32),
                        pltpu.VMEM((tm, tm), BF16),
                        pltpu.SMEM((2 * N_EXPERTS,), jnp.int32)],
        compiler_params=_params(2),
        name="moe",
    )(x, g, wr_t, br_t, wgu, wd, *([final_g] * final_norm))


def _trunk(x, p, past, dense_w, expert_w):
    b, l, d = x.shape
    depth = p['w_in'].shape[0]
    rows = b * l
    xr = x.reshape(rows, d)
    prompt = past is None
    act_dtype = BF16 if prompt else F32
    new = {'k': [], 'v': [], 're': [], 'im': [], 'gv': []}
    sp = p['ssm']
    ws, bs = _gmlp_params(p, min(l, CHUNK))
    if not prompt:
        w = past[0].shape[2]
        kc_t = past[0].transpose(0, 1, 3, 4, 2).reshape(depth, b, KV_W, w)
        vc_t = past[1].transpose(0, 1, 3, 4, 2).reshape(depth, b, KV_W, w)
        k_out = jnp.zeros_like(kc_t)
        v_out = jnp.zeros_like(vc_t)
        h0_t = tuple(s.transpose(0, 2, 3, 1).reshape(depth, SSM_FLAT, b) for s in past[2:])
    for i in range(depth):
        g_mix = p['norm_mix_g'][i].reshape(1, d)
        gv = p['gmlp_v_norm_g'][i].reshape(1, GM_W)
        q, k, v, u, gu, gvn = _inproj(xr, g_mix, p['w_in_bf16'], gv, i, act_dtype)
        u = u.reshape(b, l, SSM_W)
        sinks = p['attn_sinks'][i]
        if prompt:
            o_attn = _attn_prompt(q, k, v, sinks, batch=b, seq=l)
            nw = min(WINDOW, l)
            k_win = k.reshape(b, l, N_KV_HEADS, HEAD_DIM)[:, l - nw:]
            v_win = v.reshape(b, l, N_KV_HEADS, HEAD_DIM)[:, l - nw:]
            dense_todo = i % 2 == 0 and i // 2 not in dense_w
            o_ssm, s_fin, cast = _ssm(
                u, None, sp, i, tc=CHUNK, out_dtype=act_dtype,
                cast=((p['ffn_w_gate'], p['ffn_w_up'], p['ffn_w_down']), i // 2)
                if dense_todo else None)
            if dense_todo:
                dense_w[i // 2] = cast
            s_re = s_fin[:, :SSM_FLAT].reshape(b, SSM_GROUPS, SSM_STATE)
            s_im = s_fin[:, SSM_FLAT:].reshape(b, SSM_GROUPS, SSM_STATE)
        else:
            o_attn, k_out, v_out = _attn_sample(q, k, v, kc_t, vc_t, k_out, v_out, sinks, i,
                                                batch=b, t=l)
            o_ssm, s_fin, _ = _ssm(u, h0_t, sp, i, tc=l, out_dtype=act_dtype)
            s_re, s_im = (s.reshape(SSM_GROUPS, SSM_STATE, b).transpose(2, 0, 1) for s in s_fin)
            new['gv'].append(gvn.reshape(b, l, GM_HEADS, GM_HEAD_DIM))
        xr = _outproj(xr, o_attn, o_ssm.reshape(rows, SSM_W), gu, gvn, ws, bs,
                      p['mix_out_norm_g'][i].reshape(1, d), p['w_out_bf16'], i,
                      chunk=min(l, CHUNK))
        g_ffn = p['norm_ffn_g'][i].reshape(1, d)
        final_g = p['final_norm_g'].reshape(1, d) if i == depth - 1 else None
        j = i // 2
        if i % 2 == 0:
            todo = prompt and i + 1 < depth and j not in expert_w
            if j not in dense_w:
                dense_w[j] = tuple(p[n][j].astype(BF16)
                                   for n in ('ffn_w_gate', 'ffn_w_up', 'ffn_w_down'))
            xr, cast = _ffn(xr, g_ffn, *dense_w[j], final_g,
                            (p['moe_w_gate'], p['moe_w_up'], p['moe_w_down'], j) if todo else None)
            if todo:
                expert_w[j] = cast
        else:
            wr = p['moe_w_router'][j].T
            br = p['moe_b_router'][j].reshape(N_EXPERTS, 1)
            if j not in expert_w:
                expert_w[j] = (jnp.concatenate([p['moe_w_gate'][j].astype(BF16),
                                                p['moe_w_up'][j].astype(BF16)], axis=-1),
                               p['moe_w_down'][j].astype(BF16))
            xr = _moe(xr, g_ffn, wr, br, *expert_w[j], final_g)
        if prompt:
            new['k'].append(k_win)
            new['v'].append(v_win)
        new['re'].append(s_re)
        new['im'].append(s_im)
    for name in ('re', 'im', 'gv') + (('k', 'v') if prompt else ()):
        new[name] = jnp.stack(new[name]) if new[name] else None
    if not prompt:
        back = lambda c: c.reshape(depth, b, N_KV_HEADS, HEAD_DIM, w).transpose(0, 1, 4, 2, 3)
        new['k'], new['v'] = back(k_out), back(v_out)
    return xr.reshape(b, l, d), new


def kernel(x_prompt, x_sample, cache_k_win, cache_v_win, state_ssm_re, state_ssm_im,
           norm_mix_g, w_in, attn_sinks, ssm_a_re, ssm_a_im, ssm_log_dt, ssm_b_re, ssm_b_im,
           ssm_c_re, ssm_c_im, ssm_d, ssm_w_glu, ssm_b_glu, gmlp_v_norm_g, gmlp_w_s, gmlp_b_s,
           mix_out_norm_g, w_out, norm_ffn_g, ffn_w_gate, ffn_w_up, ffn_w_down,
           moe_w_router, moe_b_router, moe_w_gate, moe_w_up, moe_w_down, final_norm_g):
    params = {
        'norm_mix_g': norm_mix_g, 'w_in': w_in, 'attn_sinks': attn_sinks,
        'ssm_a_re': ssm_a_re, 'ssm_a_im': ssm_a_im, 'ssm_log_dt': ssm_log_dt,
        'ssm_b_re': ssm_b_re, 'ssm_b_im': ssm_b_im, 'ssm_c_re': ssm_c_re, 'ssm_c_im': ssm_c_im,
        'ssm_d': ssm_d, 'ssm_w_glu': ssm_w_glu, 'ssm_b_glu': ssm_b_glu,
        'gmlp_v_norm_g': gmlp_v_norm_g, 'gmlp_w_s': gmlp_w_s, 'gmlp_b_s': gmlp_b_s,
        'mix_out_norm_g': mix_out_norm_g, 'w_out': w_out, 'norm_ffn_g': norm_ffn_g,
        'ffn_w_gate': ffn_w_gate, 'ffn_w_up': ffn_w_up, 'ffn_w_down': ffn_w_down,
        'moe_w_router': moe_w_router, 'moe_b_router': moe_b_router,
        'moe_w_gate': moe_w_gate, 'moe_w_up': moe_w_up, 'moe_w_down': moe_w_down,
        'final_norm_g': final_norm_g,
    }
    for name in ('w_in', 'w_out'):
        params[name + '_bf16'] = params[name].astype(BF16)
    params['ssm'] = _ssm_params(params)
    dense_w, expert_w = {}, {}
    y_p, st_p = _trunk(x_prompt, params, None, dense_w, expert_w)
    y_s, st_s = _trunk(x_sample, params,
                       (cache_k_win, cache_v_win, state_ssm_re, state_ssm_im), dense_w, expert_w)
    return (y_p, y_s,
            st_p['k'], st_p['v'], st_p['re'], st_p['im'],
            st_s['k'], st_s['v'], st_s['re'], st_s['im'], st_s['gv'])
```

```python
import functools
import math

import jax
import jax.numpy as jnp
from jax import lax
from jax.experimental import pallas as pl
from jax.experimental.pallas import tpu as pltpu

F32 = jnp.float32
BF16 = jnp.bfloat16

EPS = 1e-6
HEAD_DIM = 64
N_HEADS = 8
N_KV_HEADS = 2
GQA_GROUP = N_HEADS // N_KV_HEADS
ATTN_W = N_HEADS * HEAD_DIM
KV_W = N_KV_HEADS * HEAD_DIM
WINDOW = 128
ATTN_SCALE = 1.0 / math.sqrt(HEAD_DIM)
LOG2E = math.log2(math.e)
SSM_W = 256
SSM_GROUPS = 16
SSM_STATE = 64
SSM_FLAT = SSM_GROUPS * SSM_STATE
GM_W = 256
GM_HEADS = 4
GM_HEAD_DIM = 64
CHUNK = 128
Q_END = ATTN_W
K_END = Q_END + KV_W
V_END = K_END + KV_W
S_END = V_END + SSM_W
IN_COLS = S_END + 2 * GM_W
N_EXPERTS = 8

VMEM_LIMIT_BYTES = 56 * 1024 * 1024
ROW_TILE = 512
SUB_ROWS = 128
OUTPROJ_PASS_ROWS = 256
ATTN_Q_BLOCKS = 4
MOE_ROW_TILE = 1024
MOE_CHUNK = 128
MOE_SCATTER_ROWS = 512
MOE_TAIL_WINDOW = 256
SCAN_LANES = 512


def _params(n_axes):
    return pltpu.CompilerParams(
        dimension_semantics=("arbitrary",) * n_axes,
        vmem_limit_bytes=VMEM_LIMIT_BYTES)


def _rms(x, g):
    return x * lax.rsqrt(jnp.mean(x * x, axis=-1, keepdims=True) + EPS) * g


def _full(shape):
    return pl.BlockSpec(shape, lambda *_: (0,) * len(shape))


def _group_mean(a, n_groups, width):
    lane = lax.broadcasted_iota(jnp.int32, (1, n_groups * width), 1)
    out = jnp.zeros_like(a)
    for h in range(n_groups):
        m = (lane >= h * width) & (lane < (h + 1) * width)
        s = jnp.sum(jnp.where(m, a, 0.0), axis=-1, keepdims=True) * (1.0 / width)
        out = jnp.where(m, s, out)
    return out


def _inproj_body(x_ref, g_ref, w_ref, gv_ref, q_ref, k_ref, v_ref, u_ref, gu_ref, gvn_ref):
    for r0 in range(0, x_ref.shape[0], SUB_ROWS):
        rs = slice(r0, r0 + SUB_ROWS)
        h = _rms(x_ref[rs, :], g_ref[...])
        z = jnp.dot(h.astype(BF16), w_ref[...], preferred_element_type=F32)
        q_ref[rs, :] = (z[:, :Q_END] * (ATTN_SCALE * LOG2E)).astype(BF16)
        k_ref[rs, :] = z[:, Q_END:K_END]
        v_ref[rs, :] = z[:, K_END:V_END]
        u_ref[rs, :] = z[:, V_END:S_END]
        g = jax.nn.gelu(z[:, S_END:])
        gu_ref[rs, :] = g[:, :GM_W].astype(gu_ref.dtype)
        vv = g[:, GM_W:]
        mu = _group_mean(vv, GM_HEADS, GM_HEAD_DIM)
        var = _group_mean(jnp.square(vv - mu), GM_HEADS, GM_HEAD_DIM)
        gvn_ref[rs, :] = ((vv - mu) * lax.rsqrt(var + EPS) * gv_ref[...]).astype(gvn_ref.dtype)


def _layer(shape, j):
    return pl.BlockSpec((None,) + tuple(shape), lambda *_: (j,) + (0,) * len(shape))


def _inproj(x, g, w_bf16, gv, j, act_dtype):
    rows, d = x.shape
    tm = min(ROW_TILE, rows)
    nt = rows // tm
    row = lambda w: pl.BlockSpec((tm, w), lambda i: (i, 0))
    return pl.pallas_call(
        _inproj_body,
        grid=(nt,),
        in_specs=[row(d), _full((1, d)), _layer((d, IN_COLS), j), _full((1, GM_W))],
        out_specs=[row(ATTN_W), row(KV_W), row(KV_W), row(SSM_W), row(GM_W), row(GM_W)],
        out_shape=[jax.ShapeDtypeStruct((rows, ATTN_W), BF16),
                   jax.ShapeDtypeStruct((rows, KV_W), F32),
                   jax.ShapeDtypeStruct((rows, KV_W), F32),
                   jax.ShapeDtypeStruct((rows, SSM_W), F32),
                   jax.ShapeDtypeStruct((rows, GM_W), act_dtype),
                   jax.ShapeDtypeStruct((rows, GM_W), act_dtype)],
        compiler_params=_params(1),
        name="inproj",
    )(x, g, w_bf16, gv)


def _sink_softmax(s, mask, sink):
    s = jnp.where(mask, s, -jnp.inf)
    m = jnp.maximum(jnp.max(s, axis=-1, keepdims=True), sink)
    e = jnp.exp2(s - m)
    return e / (jnp.sum(e, axis=-1, keepdims=True) + jnp.exp2(sink - m))


def _head_pair_operands(x):
    low = lax.broadcasted_iota(jnp.int32, (1, KV_W), 1) < HEAD_DIM
    swapped = pltpu.roll(x, HEAD_DIM, axis=1)
    keep = lambda a, in_low: jnp.where(low == in_low, a, 0.0).astype(BF16)
    return ((keep(x, True), keep(swapped, False)),
            (keep(swapped, True), keep(x, False)))


def _attn_prompt_body(sink_ref, q_ref, kc_ref, kp_ref, vc_ref, vp_ref, o_ref, *, nq):
    n = pl.program_id(1)
    qi = lax.broadcasted_iota(jnp.int32, (WINDOW, 2 * WINDOW), 0)
    si = lax.broadcasted_iota(jnp.int32, (WINDOW, 2 * WINDOW), 1)
    dist = WINDOW + qi - si
    band = (dist >= 0) & (dist < WINDOW)
    first = band & ((n > 0) | (si >= WINDOW))
    k_ops = _head_pair_operands(jnp.concatenate([kp_ref[...], kc_ref[...]], axis=0))
    v_ops = _head_pair_operands(jnp.concatenate([vp_ref[...], vc_ref[...]], axis=0))
    pair_w = 2 * HEAD_DIM
    for j in range(nq):
        mask = first if j == 0 else band
        q_rows = slice(j * WINDOW, (j + 1) * WINDOW)
        kv_rows = slice(j * WINDOW, (j + 2) * WINDOW)
        for hp in range(N_HEADS // 2):
            kh = (2 * hp) // GQA_GROUP
            qp = q_ref[q_rows, hp * pair_w:(hp + 1) * pair_w]
            acc = None
            for par in range(2):
                sink = sink_ref[2 * hp + par] * LOG2E
                s = lax.dot_general(qp, k_ops[kh][par][kv_rows], (((1,), (1,)), ((), ())),
                                    preferred_element_type=F32)
                s = jnp.where(mask, s, -jnp.inf)
                m = jnp.maximum(jnp.max(s, axis=-1, keepdims=True), sink)
                e = jnp.exp2(s - m)
                den = jnp.sum(e, axis=-1, keepdims=True) + jnp.exp2(sink - m)
                pv = jnp.dot(e.astype(BF16), v_ops[kh][par][kv_rows],
                             preferred_element_type=F32)
                pv = pv * (1.0 / den)
                acc = pv if acc is None else acc + pv
            o_ref[q_rows, hp * pair_w:(hp + 1) * pair_w] = acc.astype(o_ref.dtype)


def _attn_prompt(q, k, v, sinks, *, batch, seq):
    nq = ATTN_Q_BLOCKS
    nb = seq // WINDOW
    steps = nb // nq
    cur = lambda w: pl.BlockSpec((nq * WINDOW, w), lambda b, n: (b * steps + n, 0))
    prev = lambda w: pl.BlockSpec(
        (WINDOW, w), lambda b, n: (b * nb + jnp.maximum(n * nq - 1, 0), 0))
    return pl.pallas_call(
        functools.partial(_attn_prompt_body, nq=nq),
        grid=(batch, steps),
        in_specs=[pl.BlockSpec(memory_space=pltpu.SMEM),
                  cur(ATTN_W), cur(KV_W), prev(KV_W), cur(KV_W), prev(KV_W)],
        out_specs=cur(ATTN_W),
        out_shape=jax.ShapeDtypeStruct((batch * seq, ATTN_W), BF16),
        compiler_params=_params(2),
        name="attn_prompt",
    )(sinks, q, k, k, v, v)


def _attn_sample_body(sink_ref, q_ref, k_ref, v_ref, kc_ref, vc_ref, kin_ref, vin_ref,
                      o_ref, ko_ref, vo_ref, *, bb, t):
    del kin_ref, vin_ref
    w = kc_ref.shape[-1]
    pair_w = 2 * HEAD_DIM
    lane = lax.broadcasted_iota(jnp.int32, (1, w), 1)
    new_col = lane < t
    k_new_t = k_ref[...].T
    v_new_t = v_ref[...].T
    qf = q_ref[...].astype(F32)
    unit = 2 * t
    upper = lax.broadcasted_iota(jnp.int32, (unit, 1), 0) < t
    zeros_half = jnp.zeros((HEAD_DIM, 2 * w), BF16)
    placed = lambda a, par: jnp.concatenate((a, zeros_half) if par == 0 else (zeros_half, a), axis=0)
    scores, sinks, values = [], [], []
    for b in range(bb):
        shift = (w - b * t) % w
        k_cols = jnp.where(new_col, pltpu.roll(k_new_t, shift, axis=1), 0.0)
        v_cols = jnp.where(new_col, pltpu.roll(v_new_t, shift, axis=1), 0.0)
        k_old = kc_ref[b]
        v_old = vc_ref[b]
        ko_ref[b] = pltpu.roll(jnp.where(new_col, k_cols, k_old), w - t, axis=1)
        vo_ref[b] = pltpu.roll(jnp.where(new_col, v_cols, v_old), w - t, axis=1)
        k_all = jnp.concatenate([k_old, k_cols], axis=1).astype(BF16)
        v_all = jnp.concatenate([v_old, v_cols], axis=1).astype(BF16)
        for kh in range(N_KV_HEADS):
            kv_sl = slice(kh * HEAD_DIM, (kh + 1) * HEAD_DIM)
            q2 = jnp.concatenate(
                [qf[b * t:(b + 1) * t, (2 * kh + c) * pair_w:(2 * kh + c + 1) * pair_w]
                 for c in range(2)], axis=0).astype(BF16)
            for par in range(2):
                h_a = GQA_GROUP * kh + par
                sinks.append(jnp.where(upper, sink_ref[h_a], sink_ref[h_a + 2]) * LOG2E)
                scores.append(jnp.dot(q2, placed(k_all[kv_sl], par),
                                      preferred_element_type=F32))
                values.append(placed(v_all[kv_sl], par))
    s = jnp.concatenate(scores, axis=0)
    ti = lax.broadcasted_iota(jnp.int32, s.shape, 0) % t
    si = lax.broadcasted_iota(jnp.int32, s.shape, 1)
    dist = w + ti - si
    p = _sink_softmax(s, (dist >= 0) & (dist < WINDOW), jnp.concatenate(sinks, axis=0)).astype(BF16)
    for b in range(bb):
        for kh in range(N_KV_HEADS):
            acc = None
            for par in range(2):
                u = (b * N_KV_HEADS + kh) * 2 + par
                pv = lax.dot_general(p[u * unit:(u + 1) * unit], values[u],
                                     (((1,), (1,)), ((), ())), preferred_element_type=F32)
                acc = pv if acc is None else acc + pv
            for c in range(2):
                o_ref[b * t:(b + 1) * t, (2 * kh + c) * pair_w:(2 * kh + c + 1) * pair_w] = (
                    acc[c * t:(c + 1) * t])


def _attn_sample(q, k, v, kc_t, vc_t, k_out, v_out, sinks, layer, *, batch, t):
    w = kc_t.shape[-1]
    bb = w // t
    row = lambda c: pl.BlockSpec((bb * t, c), lambda i: (i, 0))
    buf = pl.BlockSpec((None, bb, KV_W, w), lambda i: (layer, i, 0, 0))
    anywhere = pl.BlockSpec(memory_space=pl.ANY)
    return pl.pallas_call(
        functools.partial(_attn_sample_body, bb=bb, t=t),
        grid=(batch // bb,),
        in_specs=[pl.BlockSpec(memory_space=pltpu.SMEM),
                  row(ATTN_W), row(KV_W), row(KV_W), buf, buf, anywhere, anywhere],
        out_specs=[row(ATTN_W), buf, buf],
        out_shape=[jax.ShapeDtypeStruct((batch * t, ATTN_W), F32),
                   jax.ShapeDtypeStruct(k_out.shape, F32),
                   jax.ShapeDtypeStruct(v_out.shape, F32)],
        input_output_aliases={6: 1, 7: 2},
        compiler_params=_params(1),
        name="attn_sample",
    )(sinks, q, k, v, kc_t, vc_t, k_out, v_out)


def _ssm_body(u_ref, *refs, r, tc, has_h0, n_cast):
    h0_refs, refs = refs[:2 * has_h0], refs[2 * has_h0:]
    (are_ref, aim_ref, ldt_ref, bre_ref, bim_ref, cre_ref, cim_ref, d_ref, wglu_ref,
     bglu_ref), refs = refs[:10], refs[10:]
    cast_in, (o_ref, s_ref), refs = refs[:n_cast], refs[n_cast:n_cast + 2], refs[n_cast + 2:]
    cast_out, (abar_scr, bbar_scr, h_scr, *x_scrs) = refs[:n_cast], refs[n_cast:]
    c = pl.program_id(0)
    for src, dst in zip(cast_in, cast_out):
        dst[...] = src[...].astype(BF16)

    @pl.when(c == 0)
    def _():
        ar = are_ref[...]
        ai = aim_ref[...]
        dt = jnp.exp(ldt_ref[...])
        decay = jnp.exp(dt * ar)
        abr = decay * jnp.cos(dt * ai)
        abi = decay * jnp.sin(dt * ai)
        den = ar * ar + ai * ai
        nr = abr - 1.0
        fr = (nr * ar + abi * ai) / den
        fi = (abi * ar - nr * ai) / den
        abar_scr[0:1, :] = abr
        abar_scr[1:2, :] = abi
        br = bre_ref[...]
        bi = bim_ref[...]
        bbar_scr[:, :SSM_FLAT] = (fr * br - fi * bi).astype(BF16)
        bbar_scr[:, SSM_FLAT:] = (fr * bi + fi * br).astype(BF16)
        if has_h0:
            h_scr[:, :SSM_FLAT] = h0_refs[0][...].T
            h_scr[:, SSM_FLAT:] = h0_refs[1][...].T
        else:
            h_scr[...] = jnp.zeros_like(h_scr)

    u = jnp.swapaxes(u_ref[...], 0, 1).reshape(tc * r, SSM_W)
    ub = u.astype(BF16)

    y = None
    for lc in range(SSM_FLAT // SCAN_LANES):
        lanes = slice(lc * SCAN_LANES, (lc + 1) * SCAN_LANES)
        re_sl = lanes
        im_sl = slice(SSM_FLAT + lc * SCAN_LANES, SSM_FLAT + (lc + 1) * SCAN_LANES)
        xr_scr, xi_scr = x_scrs[2 * lc], x_scrs[2 * lc + 1]
        xr_scr[...] = jnp.dot(ub, bbar_scr[:, re_sl], preferred_element_type=F32)
        xi_scr[...] = jnp.dot(ub, bbar_scr[:, im_sl], preferred_element_type=F32)
        ar = jnp.broadcast_to(abar_scr[0:1, lanes], (8, SCAN_LANES))
        ai = jnp.broadcast_to(abar_scr[1:2, lanes], (8, SCAN_LANES))
        for rg in range(r // 8):
            rg_sl = slice(rg * 8, (rg + 1) * 8)
            hr, hi = h_scr[rg_sl, re_sl], h_scr[rg_sl, im_sl]
            for t in range(tc):
                rows = slice(t * r + rg * 8, t * r + rg * 8 + 8)
                hr, hi = (ar * hr - ai * hi + xr_scr[rows, :],
                          ar * hi + ai * hr + xi_scr[rows, :])
                xr_scr[rows, :] = hr
                xi_scr[rows, :] = hi
            h_scr[rg_sl, re_sl] = hr
            h_scr[rg_sl, im_sl] = hi
        part = (jnp.dot(xr_scr[...].astype(BF16), cre_ref[lanes, :], preferred_element_type=F32)
                - jnp.dot(xi_scr[...].astype(BF16), cim_ref[lanes, :],
                          preferred_element_type=F32))
        y = part if y is None else y + part
    y = jax.nn.gelu(y + d_ref[...] * u)
    gl = jnp.dot(y.astype(BF16), wglu_ref[...], preferred_element_type=F32) + bglu_ref[...]
    o_ref[...] = jnp.swapaxes((y * jax.nn.sigmoid(gl)).reshape(tc, r, SSM_W), 0, 1).astype(
        o_ref.dtype)

    @pl.when(c == pl.num_programs(0) - 1)
    def _():
        if has_h0:
            s_ref[0] = h_scr[:, :SSM_FLAT].T
            s_ref[1] = h_scr[:, SSM_FLAT:].T
        else:
            s_ref[...] = h_scr[...]


def _cast_slabs(stacks, layer, steps):
    args, in_specs, out_specs, out_shapes = [], [], [], []
    for w in stacks:
        n_l, rows, cols = w.shape
        slab = rows // steps
        args.append(w.reshape(n_l * rows, cols))
        in_specs.append(pl.BlockSpec((slab, cols), lambda i: (layer * steps + i, 0)))
        out_specs.append(pl.BlockSpec((slab, cols), lambda i: (i, 0)))
        out_shapes.append(jax.ShapeDtypeStruct((rows, cols), BF16))
    return args, in_specs, out_specs, out_shapes


def _ssm(u, h0_t, sp, layer, *, tc, out_dtype, cast=None):
    r, steps, _ = u.shape
    blk = tc * r
    has_h0 = h0_t is not None
    seq_blk = pl.BlockSpec((r, tc, SSM_W), lambda c: (0, c, 0))
    state_shape = (2, SSM_FLAT, r) if has_h0 else (r, 2 * SSM_FLAT)
    c_args, c_in, c_out, c_shapes = _cast_slabs(*cast, steps // tc) if cast else ([], [], [], [])
    out = pl.pallas_call(
        functools.partial(_ssm_body, r=r, tc=tc, has_h0=has_h0, n_cast=len(c_args)),
        grid=(steps // tc,),
        in_specs=[seq_blk]
        + [_layer((SSM_FLAT, r), layer)] * (2 * has_h0)
        + [_layer((1, SSM_FLAT), layer)] * 3
        + [_layer((SSM_W, SSM_FLAT), layer)] * 2
        + [_layer((SSM_FLAT, SSM_W), layer)] * 2
        + [_layer((1, SSM_W), layer), _layer((SSM_W, SSM_W), layer), _layer((1, SSM_W), layer)]
        + c_in,
        out_specs=[seq_blk, _full(state_shape)] + c_out,
        out_shape=[jax.ShapeDtypeStruct((r, steps, SSM_W), out_dtype),
                   jax.ShapeDtypeStruct(state_shape, F32)] + c_shapes,
        scratch_shapes=[pltpu.VMEM((2, SSM_FLAT), F32),
                        pltpu.VMEM((SSM_W, 2 * SSM_FLAT), BF16),
                        pltpu.VMEM((r, 2 * SSM_FLAT), F32)]
        + [pltpu.VMEM((blk, SCAN_LANES), F32)] * (2 * SSM_FLAT // SCAN_LANES),
        compiler_params=_params(1),
        name="ssm",
    )(u, *(h0_t if has_h0 else ()), sp['a_re'], sp['a_im'], sp['log_dt'], sp['b_re'], sp['b_im'],
      sp['c_re'], sp['c_im'], sp['d'], sp['w_glu'], sp['b_glu'], *c_args)
    return out[0], out[1], tuple(out[2:])


def _ssm_params(p):
    eye = jnp.eye(SSM_GROUPS, dtype=F32)
    depth = p['ssm_a_re'].shape[0]

    def b_blockdiag(b):
        return jnp.einsum('lgpc,gh->lgchp', b, eye).reshape(depth, SSM_W, SSM_FLAT)

    def c_blockdiag(c):
        return jnp.einsum('lgcp,gh->lgphc', c, eye).reshape(depth, SSM_FLAT, SSM_W)

    return {
        'a_re': p['ssm_a_re'].reshape(depth, 1, SSM_FLAT),
        'a_im': p['ssm_a_im'].reshape(depth, 1, SSM_FLAT),
        'log_dt': jnp.repeat(p['ssm_log_dt'], SSM_STATE, axis=1).reshape(depth, 1, SSM_FLAT),
        'b_re': b_blockdiag(p['ssm_b_re']),
        'b_im': b_blockdiag(p['ssm_b_im']),
        'c_re': c_blockdiag(p['ssm_c_re']).astype(BF16),
        'c_im': c_blockdiag(p['ssm_c_im']).astype(BF16),
        'd': p['ssm_d'].reshape(depth, 1, SSM_W),
        'w_glu': p['ssm_w_glu'].astype(BF16),
        'b_glu': p['ssm_b_glu'].reshape(depth, 1, SSM_W),
    }


def _outproj_body(x_ref, oa_ref, os_ref, gu_ref, gvn_ref, ws_ref, bs_ref, go_ref, w_ref, o_ref,
                  *, chunk):
    tm = x_ref.shape[0]
    ri = lax.broadcasted_iota(jnp.int32, (CHUNK, CHUNK), 0)
    ci = lax.broadcasted_iota(jnp.int32, (CHUNK, CHUNK), 1)
    causal = (ri // chunk == ci // chunk) & (ci <= ri)
    lane = lax.broadcasted_iota(jnp.int32, (1, GM_W), 1)
    w_heads = [jnp.where(causal, ws_ref[h], 0.0).astype(BF16) for h in range(GM_HEADS)]
    go = go_ref[...]
    per_pass = OUTPROJ_PASS_ROWS // CHUNK
    for r0 in range(0, tm, OUTPROJ_PASS_ROWS):
        rs = slice(r0, r0 + OUTPROJ_PASS_ROWS)
        zs = []
        for cblk in range(per_pass):
            vn = gvn_ref[r0 + cblk * CHUNK:r0 + (cblk + 1) * CHUNK, :].astype(BF16)
            z = bs_ref[...]
            for h in range(GM_HEADS):
                z_h = jnp.dot(w_heads[h], vn, preferred_element_type=F32)
                head = (lane >= h * GM_HEAD_DIM) & (lane < (h + 1) * GM_HEAD_DIM)
                z = z + jnp.where(head, z_h, 0.0)
            zs.append(z)
        o = jnp.concatenate([
            _rms(oa_ref[rs, :].astype(F32), go[:, :ATTN_W]),
            _rms(os_ref[rs, :].astype(F32), go[:, ATTN_W:ATTN_W + SSM_W]),
            _rms(gu_ref[rs, :].astype(F32) * jnp.concatenate(zs, axis=0),
                 go[:, ATTN_W + SSM_W:])], axis=-1)
        o_ref[rs, :] = x_ref[rs, :] + jnp.dot(o.astype(BF16), w_ref[...],
                                              preferred_element_type=F32)


def _outproj(x, o_attn, o_ssm, gu, gvn, ws, bs, g_out, w_bf16, j, *, chunk):
    rows, d = x.shape
    tm = min(ROW_TILE, rows)
    row = lambda w: pl.BlockSpec((tm, w), lambda i: (i, 0))
    return pl.pallas_call(
        functools.partial(_outproj_body, chunk=chunk),
        grid=(rows // tm,),
        in_specs=[row(d), row(ATTN_W), row(SSM_W), row(GM_W), row(GM_W),
                  _layer((GM_HEADS, CHUNK, CHUNK), j), _layer((CHUNK, GM_W), j),
                  _full((1, d)), _layer((d, d), j)],
        out_specs=row(d),
        out_shape=jax.ShapeDtypeStruct((rows, d), F32),
        compiler_params=_params(1),
        name="outproj",
    )(x, o_attn, o_ssm, gu, gvn, ws, bs, g_out, w_bf16)


def _gmlp_params(p, chunk):
    ws = p['gmlp_w_s'][:, :, :chunk, :chunk]
    bs = p['gmlp_b_s'][:, :, :chunk]
    if chunk < CHUNK:
        pick = (jnp.arange(CHUNK)[:, None] % chunk == jnp.arange(chunk)[None, :]).astype(F32)
        ws = jnp.einsum('rc,lhcd,sd->lhrs', pick, ws, pick, precision=lax.Precision.HIGHEST)
        bs = jnp.einsum('rc,lhc->lhr', pick, bs, precision=lax.Precision.HIGHEST)
    bs = jnp.repeat(bs.transpose(0, 2, 1), GM_HEAD_DIM, axis=2)
    return ws, bs


def _swiglu_cols(h, wg_ref, wu_ref, wd_ref, n_chunks):
    ff = wg_ref.shape[-1]
    fc = ff // n_chunks
    acc = None
    for j in range(n_chunks):
        sl = slice(j * fc, (j + 1) * fc)
        a = jnp.dot(h, wg_ref[:, sl], preferred_element_type=F32)
        b = jnp.dot(h, wu_ref[:, sl], preferred_element_type=F32)
        m = (jax.nn.silu(a) * b).astype(BF16)
        y = jnp.dot(m, wd_ref[sl, :], preferred_element_type=F32)
        acc = y if acc is None else acc + y
    return acc


def _ffn_body(x_ref, g_ref, wg_ref, wu_ref, wd_ref, *rest, final_norm, n_cast):
    gf_ref = rest[0] if final_norm else None
    rest = rest[int(final_norm):]
    cast_in, o_ref, cast_out = rest[:n_cast], rest[n_cast], rest[n_cast + 1:]
    x = x_ref[...]
    h = _rms(x, g_ref[...]).astype(BF16)
    y = x + _swiglu_cols(h, wg_ref, wu_ref, wd_ref, 2)
    o_ref[...] = _rms(y, gf_ref[...]) if final_norm else y
    if n_cast:
        eg_ref, eu_ref, ed_ref = cast_in
        egu_ref, edo_ref = cast_out
        ffe = eg_ref.shape[-1]
        egu_ref[:, :ffe] = eg_ref[...].astype(BF16)
        egu_ref[:, ffe:] = eu_ref[...].astype(BF16)
        edo_ref[...] = ed_ref[...].astype(BF16)


def _ffn(x, g, wg, wu, wd, final_g, experts=None):
    rows, d = x.shape
    ff = wg.shape[-1]
    tm = min(ROW_TILE, rows)
    steps = rows // tm
    row = pl.BlockSpec((tm, d), lambda i: (i, 0))
    once = lambda shape: pl.BlockSpec(shape, lambda i: (0, 0), pipeline_mode=pl.Buffered(1))
    final_norm = final_g is not None
    cast_args, cast_in, cast_out, cast_shapes = [], [], [], []
    if experts is not None:
        eg, eu, ed, le = experts
        n_l, n_e, _, ffe = eg.shape
        gu_rows, d_rows = n_e * d // steps, n_e * ffe // steps
        slab = lambda r, c: pl.BlockSpec((r, c), lambda i: (le * steps + i, 0))
        cast_args = [eg.reshape(n_l * n_e * d, ffe), eu.reshape(n_l * n_e * d, ffe),
                     ed.reshape(n_l * n_e * ffe, d)]
        cast_in = [slab(gu_rows, ffe), slab(gu_rows, ffe), slab(d_rows, d)]
        cast_out = [pl.BlockSpec((gu_rows, 2 * ffe), lambda i: (i, 0)),
                    pl.BlockSpec((d_rows, d), lambda i: (i, 0))]
        cast_shapes = [jax.ShapeDtypeStruct((n_e * d, 2 * ffe), BF16),
                       jax.ShapeDtypeStruct((n_e * ffe, d), BF16)]
    out = pl.pallas_call(
        functools.partial(_ffn_body, final_norm=final_norm, n_cast=len(cast_in)),
        grid=(steps,),
        in_specs=[row, _full((1, d)), once((d, ff)), once((d, ff)), once((ff, d))]
        + [_full((1, d))] * final_norm + cast_in,
        out_specs=[row] + cast_out,
        out_shape=[jax.ShapeDtypeStruct((rows, d), F32)] + cast_shapes,
        compiler_params=_params(1),
        name="ffn",
    )(x, g, wg, wu, wd, *([final_g] * final_norm), *cast_args)
    if experts is None:
        return out[0], None
    return out[0], (out[1].reshape(n_e, d, 2 * ffe), out[2].reshape(n_e, ffe, d))


def _split_bf16(x):
    hi = x.astype(BF16)
    lo = (x - hi.astype(F32)).astype(BF16)
    return hi, lo


def _router_comb(hf, wrt_ref, brt_ref):
    h_hi, h_lo = _split_bf16(hf)
    w_hi, w_lo = _split_bf16(wrt_ref[...])
    nt = lambda a, b: lax.dot_general(a, b, (((1,), (1,)), ((), ())), preferred_element_type=F32)
    logits = nt(w_hi, h_hi) + (nt(w_lo, h_hi) + nt(w_hi, h_lo)) + brt_ref[...]
    n_e = logits.shape[0]
    row = lax.broadcasted_iota(jnp.int32, logits.shape, 0).astype(F32)
    far = float(n_e)
    m1 = jnp.max(logits, axis=0, keepdims=True)
    i1 = jnp.min(jnp.where(logits == m1, row, far), axis=0, keepdims=True)
    rest = jnp.where(row == i1, -jnp.inf, logits)
    m2 = jnp.max(rest, axis=0, keepdims=True)
    i2 = jnp.min(jnp.where(rest == m2, row, far), axis=0, keepdims=True)
    e2 = jnp.exp(m2 - m1)
    den = 1.0 + e2
    comb = jnp.where(row == i1, 1.0 / den, 0.0) + jnp.where(row == i2, e2 / den, 0.0)
    sel = jnp.where((row == i1) | (row == i2), 1.0, 0.0)
    return comb, sel


def _moe_body(x_ref, g_ref, wr_ref, br_ref, wgu_ref, wd_ref, *rest, final_norm):
    gf_ref = rest[0] if final_norm else None
    o_ref, h_scr, gate_t_scr, rank_t_scr, upper_scr, cnt_smem = rest[int(final_norm):]
    i = pl.program_id(0)
    e = pl.program_id(1)
    tm = x_ref.shape[0]
    ff = wd_ref.shape[0]

    @pl.when((i == 0) & (e == 0))
    def _():
        ri = lax.broadcasted_iota(jnp.int32, (tm, tm), 0)
        ci = lax.broadcasted_iota(jnp.int32, (tm, tm), 1)
        upper_scr[...] = jnp.where(ri < ci, 1.0, 0.0).astype(BF16)

    @pl.when(e == 0)
    def _():
        x = x_ref[...]
        hf = _rms(x, g_ref[...])
        h_scr[...] = hf.astype(BF16)
        comb, sel = _router_comb(hf, wr_ref, br_ref)
        rank = jnp.dot(sel.astype(BF16), upper_scr[...], preferred_element_type=F32)
        rank_t_scr[...] = jnp.where(sel > 0.0, rank, -1.0)
        gate_t_scr[...] = comb
        cnt = jnp.sum(sel, axis=1, keepdims=True)
        tail0 = tm - min(MOE_TAIL_WINDOW, tm)
        for k in range(N_EXPERTS):
            cnt_smem[k] = cnt[k, 0].astype(jnp.int32)
            cnt_smem[N_EXPERTS + k] = rank[k, tail0].astype(jnp.int32)
        o_ref[...] = x

    n_chunks = (cnt_smem[e] + (MOE_CHUNK - 1)) // MOE_CHUNK
    n_pairs = n_chunks // 2

    def run_chunk(first_slot, size, t0):
        base = first_slot.astype(F32)
        slot_col = lax.broadcasted_iota(jnp.int32, (size, 1), 0).astype(F32) + base
        hit = rank_t_scr[pl.ds(e, 1), t0:] == slot_col
        onehot = jnp.where(hit, 1.0, 0.0).astype(BF16)
        xe = jnp.dot(onehot, h_scr[t0:, :], preferred_element_type=F32).astype(BF16)
        ab = jnp.dot(xe, wgu_ref[...], preferred_element_type=F32)
        m = (jax.nn.silu(ab[:, :ff]) * ab[:, ff:]).astype(BF16)
        gate = jnp.sum(jnp.where(hit, gate_t_scr[pl.ds(e, 1), t0:], 0.0), axis=-1, keepdims=True)
        ye = (jnp.dot(m, wd_ref[...], preferred_element_type=F32) * gate).astype(BF16)
        sr = min(MOE_SCATTER_ROWS, tm - t0)
        for r0 in range(t0, tm, sr):
            o_ref[r0:r0 + sr, :] += lax.dot_general(
                onehot[:, r0 - t0:r0 - t0 + sr], ye, (((0,), (0,)), ((), ())),
                preferred_element_type=F32)

    def pair(c, carry):
        run_chunk(c * (2 * MOE_CHUNK), 2 * MOE_CHUNK, 0)
        return carry

    lax.fori_loop(0, n_pairs, pair, 0)

    tail_slot = n_pairs * (2 * MOE_CHUNK)
    has_tail = n_chunks % 2 == 1
    in_window = cnt_smem[N_EXPERTS + e] <= tail_slot

    @pl.when(has_tail & in_window)
    def _():
        run_chunk(tail_slot, MOE_CHUNK, tm - min(MOE_TAIL_WINDOW, tm))

    @pl.when(has_tail & jnp.logical_not(in_window))
    def _():
        run_chunk(tail_slot, MOE_CHUNK, 0)

    if final_norm:
        @pl.when(e == pl.num_programs(1) - 1)
        def _():
            o_ref[...] = _rms(o_ref[...], gf_ref[...])


def _moe(x, g, wr_t, br_t, wgu, wd, final_g):
    rows, d = x.shape
    n_e, ff, _ = wd.shape
    tm = min(MOE_ROW_TILE, rows)
    row = pl.BlockSpec((tm, d), lambda i, e: (i, 0))
    final_norm = final_g is not None
    return pl.pallas_call(
        functools.partial(_moe_body, final_norm=final_norm),
        grid=(rows // tm, n_e),
        in_specs=[row, _full((1, d)), _full((n_e, d)), _full((n_e, 1)),
                  pl.BlockSpec((None, d, 2 * ff), lambda i, e: (e, 0, 0)),
                  pl.BlockSpec((None, ff, d), lambda i, e: (e, 0, 0))]
        + [_full((1, d))] * final_norm,
        out_specs=row,
        out_shape=jax.ShapeDtypeStruct((rows, d), F32),
        scratch_shapes=[pltpu.VMEM((tm, d), BF16),
                        pltpu.VMEM((n_e, tm), F32),
                        pltpu.VMEM((n_e, tm), F32),
                        pltpu.VMEM((tm, tm), BF16),
                        pltpu.SMEM((2 * N_EXPERTS,), jnp.int32)],
        compiler_params=_params(2),
        name="moe",
    )(x, g, wr_t, br_t, wgu, wd, *([final_g] * final_norm))


def _trunk(x, p, past, dense_w, expert_w):
    b, l, d = x.shape
    depth = p['w_in'].shape[0]
    rows = b * l
    xr = x.reshape(rows, d)
    prompt = past is None
    act_dtype = BF16 if prompt else F32
    new = {'k': [], 'v': [], 're': [], 'im': [], 'gv': []}
    sp = p['ssm']
    ws, bs = _gmlp_params(p, min(l, CHUNK))
    if not prompt:
        w = past[0].shape[2]
        kc_t = past[0].transpose(0, 1, 3, 4, 2).reshape(depth, b, KV_W, w)
        vc_t = past[1].transpose(0, 1, 3, 4, 2).reshape(depth, b, KV_W, w)
        k_out = jnp.zeros_like(kc_t)
        v_out = jnp.zeros_like(vc_t)
        h0_t = tuple(s.transpose(0, 2, 3, 1).reshape(depth, SSM_FLAT, b) for s in past[2:])
    for i in range(depth):
        g_mix = p['norm_mix_g'][i].reshape(1, d)
        gv = p['gmlp_v_norm_g'][i].reshape(1, GM_W)
        q, k, v, u, gu, gvn = _inproj(xr, g_mix, p['w_in_bf16'], gv, i, act_dtype)
        u = u.reshape(b, l, SSM_W)
        sinks = p['attn_sinks'][i]
        if prompt:
            o_attn = _attn_prompt(q, k, v, sinks, batch=b, seq=l)
            nw = min(WINDOW, l)
            k_win = k.reshape(b, l, N_KV_HEADS, HEAD_DIM)[:, l - nw:]
            v_win = v.reshape(b, l, N_KV_HEADS, HEAD_DIM)[:, l - nw:]
            dense_todo = i % 2 == 0 and i // 2 not in dense_w
            o_ssm, s_fin, cast = _ssm(
                u, None, sp, i, tc=CHUNK, out_dtype=act_dtype,
                cast=((p['ffn_w_gate'], p['ffn_w_up'], p['ffn_w_down']), i // 2)
                if dense_todo else None)
            if dense_todo:
                dense_w[i // 2] = cast
            s_re = s_fin[:, :SSM_FLAT].reshape(b, SSM_GROUPS, SSM_STATE)
            s_im = s_fin[:, SSM_FLAT:].reshape(b, SSM_GROUPS, SSM_STATE)
        else:
            o_attn, k_out, v_out = _attn_sample(q, k, v, kc_t, vc_t, k_out, v_out, sinks, i,
                                                batch=b, t=l)
            o_ssm, s_fin, _ = _ssm(u, h0_t, sp, i, tc=l, out_dtype=act_dtype)
            s_re, s_im = (s.reshape(SSM_GROUPS, SSM_STATE, b).transpose(2, 0, 1) for s in s_fin)
            new['gv'].append(gvn.reshape(b, l, GM_HEADS, GM_HEAD_DIM))
        xr = _outproj(xr, o_attn, o_ssm.reshape(rows, SSM_W), gu, gvn, ws, bs,
                      p['mix_out_norm_g'][i].reshape(1, d), p['w_out_bf16'], i,
                      chunk=min(l, CHUNK))
        g_ffn = p['norm_ffn_g'][i].reshape(1, d)
        final_g = p['final_norm_g'].reshape(1, d) if i == depth - 1 else None
        j = i // 2
        if i % 2 == 0:
            todo = prompt and i + 1 < depth and j not in expert_w
            if j not in dense_w:
                dense_w[j] = tuple(p[n][j].astype(BF16)
                                   for n in ('ffn_w_gate', 'ffn_w_up', 'ffn_w_down'))
            xr, cast = _ffn(xr, g_ffn, *dense_w[j], final_g,
                            (p['moe_w_gate'], p['moe_w_up'], p['moe_w_down'], j) if todo else None)
            if todo:
                expert_w[j] = cast
        else:
            wr = p['moe_w_router'][j].T
            br = p['moe_b_router'][j].reshape(N_EXPERTS, 1)
            if j not in expert_w:
                expert_w[j] = (jnp.concatenate([p['moe_w_gate'][j].astype(BF16),
                                                p['moe_w_up'][j].astype(BF16)], axis=-1),
                               p['moe_w_down'][j].astype(BF16))
            xr = _moe(xr, g_ffn, wr, br, *expert_w[j], final_g)
        if prompt:
            new['k'].append(k_win)
            new['v'].append(v_win)
        new['re'].append(s_re)
        new['im'].append(s_im)
    for name in ('re', 'im', 'gv') + (('k', 'v') if prompt else ()):
        new[name] = jnp.stack(new[name]) if new[name] else None
    if not prompt:
        back = lambda c: c.reshape(depth, b, N_KV_HEADS, HEAD_DIM, w).transpose(0, 1, 4, 2, 3)
        new['k'], new['v'] = back(k_out), back(v_out)
    return xr.reshape(b, l, d), new


def kernel(x_prompt, x_sample, cache_k_win, cache_v_win, state_ssm_re, state_ssm_im,
           norm_mix_g, w_in, attn_sinks, ssm_a_re, ssm_a_im, ssm_log_dt, ssm_b_re, ssm_b_im,
           ssm_c_re, ssm_c_im, ssm_d, ssm_w_glu, ssm_b_glu, gmlp_v_norm_g, gmlp_w_s, gmlp_b_s,
           mix_out_norm_g, w_out, norm_ffn_g, ffn_w_gate, ffn_w_up, ffn_w_down,
           moe_w_router, moe_b_router, moe_w_gate, moe_w_up, moe_w_down, final_norm_g):
    params = {
        'norm_mix_g': norm_mix_g, 'w_in': w_in, 'attn_sinks': attn_sinks,
        'ssm_a_re': ssm_a_re, 'ssm_a_im': ssm_a_im, 'ssm_log_dt': ssm_log_dt,
        'ssm_b_re': ssm_b_re, 'ssm_b_im': ssm_b_im, 'ssm_c_re': ssm_c_re, 'ssm_c_im': ssm_c_im,
        'ssm_d': ssm_d, 'ssm_w_glu': ssm_w_glu, 'ssm_b_glu': ssm_b_glu,
        'gmlp_v_norm_g': gmlp_v_norm_g, 'gmlp_w_s': gmlp_w_s, 'gmlp_b_s': gmlp_b_s,
        'mix_out_norm_g': mix_out_norm_g, 'w_out': w_out, 'norm_ffn_g': norm_ffn_g,
        'ffn_w_gate': ffn_w_gate, 'ffn_w_up': ffn_w_up, 'ffn_w_down': ffn_w_down,
        'moe_w_router': moe_w_router, 'moe_b_router': moe_b_router,
        'moe_w_gate': moe_w_gate, 'moe_w_up': moe_w_up, 'moe_w_down': moe_w_down,
        'final_norm_g': final_norm_g,
    }
    for name in ('w_in', 'w_out'):
        params[name + '_bf16'] = params[name].astype(BF16)
    params['ssm'] = _ssm_params(params)
    dense_w, expert_w = {}, {}
    y_p, st_p = _trunk(x_prompt, params, None, dense_w, expert_w)
    y_s, st_s = _trunk(x_sample, params,
                       (cache_k_win, cache_v_win, state_ssm_re, state_ssm_im), dense_w, expert_w)
    return (y_p, y_s,
            st_p['k'], st_p['v'], st_p['re'], st_p['im'],
            st_s['k'], st_s['v'], st_s['re'], st_s['im'], st_s['gv'])
```
